```python
import math
import jax
import jax.numpy as jnp
from jax import lax
import numpy as np

D_MODEL = 2048
BATCH = 4
SEQ = 2048
DEPTH = 4
DEC_BATCH = 8
DEC_SEQ = 1
PAST_LEN = 16384
PAGE_SIZE = 128

N_EVEN = (DEPTH + 1) // 2
N_ODD = DEPTH // 2
HEAD_DIM = 128
H_A = D_MODEL // (2 * HEAD_DIM)
DK_A = HEAD_DIM
DV_A = HEAD_DIM
H_B = D_MODEL // (2 * HEAD_DIM)
DH_B = HEAD_DIM
MOBA_BLOCK = 256
MOBA_TOPK = 3
MOBA_QCHUNK = 16
ROT_DIM = DH_B // 4
ROPE_THETA = 500000.0
H_C = 8
DK_C = D_MODEL // H_C
DV_C = D_MODEL // H_C
RET_THETA = 10000.0
SCAN_CHUNK = 64
D_FF = 256 * ((8 * D_MODEL // 3 + 255) // 256)
CONV_W = 3
EPS = 1e-6
A_W = H_A * DK_A
B_W = H_B * DH_B
EVEN_IN = 4 * A_W + 3 * B_W
ODD_IN = 2 * H_C * DK_C + 2 * H_C * DV_C
PAGES_PER_BLOCK = MOBA_BLOCK // PAGE_SIZE

kernel_name = 'hgrn2_moba_retention_convffn_step'


def rms_norm(x, g):
    x32 = x.astype(jnp.float32)
    y = x32 * lax.rsqrt(jnp.mean(x32 * x32, axis=-1, keepdims=True) + EPS)
    return (y * g.astype(jnp.float32)).astype(x.dtype)


def head_rms_norm(o, g):
    return o * lax.rsqrt(jnp.mean(o * o, axis=-1, keepdims=True) + EPS) * g.astype(jnp.float32)


def head_layer_norm(o, g):
    oc = o - jnp.mean(o, axis=-1, keepdims=True)
    return oc * lax.rsqrt(jnp.mean(oc * oc, axis=-1, keepdims=True) + EPS) * g.astype(jnp.float32)


def rope(x, pos, rot_dim, theta):
    half = rot_dim // 2
    inv = theta ** (-jnp.arange(half, dtype=jnp.float32) / half)
    ang = pos.astype(jnp.float32)[:, None] * inv[None, :]
    cos = jnp.cos(ang)[:, None, :]
    sin = jnp.sin(ang)[:, None, :]
    x32 = x.astype(jnp.float32)
    x1 = x32[..., :half]
    x2 = x32[..., half:rot_dim]
    out = jnp.concatenate([x1 * cos - x2 * sin, x2 * cos + x1 * sin, x32[..., rot_dim:]], axis=-1)
    return out.astype(x.dtype)


def gla_scan(q, k, v, logf, s0):
    B, H, T, _ = q.shape
    C = math.gcd(T, SCAN_CHUNK)
    n = T // C
    mask = jnp.tril(jnp.ones((C, C), dtype=bool))

    def split(a):
        return jnp.moveaxis(a.reshape(B, H, n, C, a.shape[-1]), 2, 0)

    def step(S, xs):
        qc, kc, vc, lc = xs
        b = jnp.cumsum(lc, axis=2)
        o_inter = jnp.einsum('bhcd,bhde->bhce', qc * jnp.exp(b), S)
        diff = b[:, :, :, None, :] - b[:, :, None, :, :]
        dec = jnp.exp(jnp.where(mask[:, :, None], diff, -jnp.inf))
        att = jnp.einsum('bhtd,bhsd,bhtsd->bhts', qc, kc, dec)
        o = o_inter + jnp.einsum('bhts,bhse->bhte', att, vc)
        bl = b[:, :, -1:, :]
        S = jnp.exp(bl[:, :, 0, :])[..., None] * S + jnp.einsum('bhsd,bhse->bhde', kc * jnp.exp(bl - b), vc)
        return S, o

    S, o = lax.scan(step, s0, (split(q), split(k), split(v), split(logf)))
    return jnp.moveaxis(o, 0, 2).reshape(B, H, T, -1), S


def retention_scan(q, k, v, s0):
    B, H, T, _ = q.shape
    C = math.gcd(T, SCAN_CHUNK)
    n = T // C
    lg = jnp.log(1.0 - 2.0 ** (-5.0 - jnp.arange(H, dtype=jnp.float32)))
    i = jnp.arange(C, dtype=jnp.float32)
    mask = jnp.tril(jnp.ones((C, C), dtype=bool))
    dmat = jnp.exp(jnp.where(mask, (i[:, None] - i[None, :]) * lg[:, None, None], -jnp.inf))
    inter = jnp.exp((i[None, :] + 1.0) * lg[:, None])
    state_dec = jnp.exp((C - 1.0 - i[None, :]) * lg[:, None])
    chunk_dec = jnp.exp(C * lg)

    def split(a):
        return jnp.moveaxis(a.reshape(B, H, n, C, a.shape[-1]), 2, 0)

    def step(S, xs):
        qc, kc, vc = xs
        att = jnp.einsum('bhtd,bhsd->bhts', qc, kc) * dmat
        o = jnp.einsum('bhts,bhse->bhte', att, vc) + jnp.einsum('bhtd,bhde->bhte', qc, S) * inter[:, :, None]
        S = chunk_dec[:, None, None] * S + jnp.einsum('bhsd,bhse->bhde', kc * state_dec[:, :, None], vc)
        return S, o

    S, o = lax.scan(step, s0, (split(q), split(k), split(v)))
    return jnp.moveaxis(o, 0, 2).reshape(B, H, T, -1), S


def moba_prompt(q, k, v):
    B, T, H, D = q.shape
    scale = D ** -0.5
    nb = -(-T // MOBA_BLOCK)
    tpad = nb * MOBA_BLOCK
    qh = jnp.moveaxis(q, 1, 2)
    pad = ((0, 0), (0, 0), (0, tpad - T), (0, 0))
    kb = jnp.pad(jnp.moveaxis(k, 1, 2), pad).reshape(B, H, nb, MOBA_BLOCK, D)
    vb = jnp.pad(jnp.moveaxis(v, 1, 2), pad).reshape(B, H, nb, MOBA_BLOCK, D)
    kmean = jnp.mean(kb.astype(jnp.float32), axis=3)
    qblk = jnp.arange(T) // MOBA_BLOCK
    kk = max(1, min(MOBA_TOPK, nb - 1))
    gate = jnp.einsum('bhtd,bhjd->bhtj', qh.astype(jnp.float32), kmean)
    gate = jnp.where(jnp.arange(nb)[None, :] < qblk[:, None], gate, -jnp.inf)
    _, idx = lax.top_k(gate, kk)
    valid = idx < qblk[:, None]
    qn = MOBA_QCHUNK
    n = T // qn
    bi = jnp.arange(B)[:, None, None, None]
    hi = jnp.arange(H)[None, :, None, None]

    def chunk(a):
        return jnp.moveaxis(a.reshape(B, H, n, qn, a.shape[-1]), 2, 0)

    def body(xs):
        qc, ic, okc, start = xs
        blk = start // MOBA_BLOCK
        k_own = lax.dynamic_index_in_dim(kb, blk, axis=2, keepdims=False)
        v_own = lax.dynamic_index_in_dim(vb, blk, axis=2, keepdims=False)
        tq = start + jnp.arange(qn)
        tk = blk * MOBA_BLOCK + jnp.arange(MOBA_BLOCK)
        s_own = jnp.einsum('bhqd,bhsd->bhqs', qc, k_own).astype(jnp.float32) * scale
        s_own = jnp.where(tk[None, :] <= tq[:, None], s_own, -jnp.inf)
        k_sel = kb[bi, hi, ic]
        v_sel = vb[bi, hi, ic]
        s_sel = jnp.einsum('bhqd,bhqksd->bhqks', qc, k_sel).astype(jnp.float32) * scale
        s_sel = jnp.where(okc[..., None], s_sel, -jnp.inf).reshape(B, H, qn, kk * MOBA_BLOCK)
        p = jax.nn.softmax(jnp.concatenate([s_sel, s_own], axis=-1), axis=-1).astype(v.dtype)
        p_sel = p[..., :kk * MOBA_BLOCK].reshape(B, H, qn, kk, MOBA_BLOCK)
        return (jnp.einsum('bhqks,bhqksd->bhqd', p_sel, v_sel)
                + jnp.einsum('bhqs,bhsd->bhqd', p[..., kk * MOBA_BLOCK:], v_own))

    starts = jnp.arange(n) * qn
    o = lax.map(body, (chunk(qh), chunk(idx), chunk(valid), starts))
    o = jnp.moveaxis(o, 0, 2).reshape(B, H, T, D)
    return jnp.moveaxis(o, 1, 2)


def gather_blocks(pool, li, page_table, blk_idx):
    B, T, H, kk = blk_idx.shape
    logical = blk_idx[..., None] * PAGES_PER_BLOCK + jnp.arange(PAGES_PER_BLOCK)
    phys = page_table[jnp.arange(B)[:, None, None, None, None], logical]
    rows = pool[li, phys[..., None], jnp.arange(PAGE_SIZE), jnp.arange(H)[:, None, None, None]]
    return rows.reshape(B, T, H, kk, MOBA_BLOCK, pool.shape[-1])


def moba_sample(q, k, v, cache_k, cache_v, li, page_table):
    DB, T, H, D = q.shape
    scale = D ** -0.5
    n_full = PAST_LEN // MOBA_BLOCK
    tail = PAST_LEN - n_full * MOBA_BLOCK
    k_past = cache_k[li, page_table].reshape(DB, PAST_LEN, H, D)
    if tail > 0:
        v_tail = cache_v[li, page_table[:, n_full * PAGES_PER_BLOCK:]].reshape(DB, tail, H, D)
        own_k = jnp.concatenate([k_past[:, n_full * MOBA_BLOCK:].astype(k.dtype), k], axis=1)
        own_v = jnp.concatenate([v_tail.astype(v.dtype), v], axis=1)
    else:
        own_k, own_v = k, v
    jk = jnp.arange(tail + T)
    own_mask = jk[None, :] <= tail + jnp.arange(T)[:, None]
    s_own = jnp.einsum('bthd,bshd->bths', q, own_k).astype(jnp.float32) * scale
    s_own = jnp.where(own_mask[:, None, :], s_own, -jnp.inf)
    if n_full > 0:
        kk = min(MOBA_TOPK, n_full)
        kmean = jnp.mean(k_past[:, :n_full * MOBA_BLOCK].astype(jnp.float32).reshape(DB, n_full, MOBA_BLOCK, H, D), axis=2)
        gate = jnp.einsum('bthd,bjhd->bthj', q.astype(jnp.float32), kmean)
        _, idx = lax.top_k(gate, kk)
        k_sel = gather_blocks(cache_k, li, page_table, idx).astype(q.dtype)
        v_sel = gather_blocks(cache_v, li, page_table, idx).astype(v.dtype)
        s_sel = jnp.einsum('bthd,bthksd->bthks', q, k_sel).astype(jnp.float32).reshape(DB, T, H, kk * MOBA_BLOCK) * scale
        p = jax.nn.softmax(jnp.concatenate([s_sel, s_own], axis=-1), axis=-1).astype(v.dtype)
        p_sel = p[..., :kk * MOBA_BLOCK].reshape(DB, T, H, kk, MOBA_BLOCK)
        return (jnp.einsum('bthks,bthksd->bthd', p_sel, v_sel)
                + jnp.einsum('bths,bshd->bthd', p[..., kk * MOBA_BLOCK:], own_v))
    p = jax.nn.softmax(s_own, axis=-1).astype(v.dtype)
    return jnp.einsum('bths,bshd->bthd', p, own_v)


def even_mix(h, pos, w_in, w_out, gn, lb, s0, attn):
    B, T, _ = h.shape
    z = h @ w_in
    qa, fa, ia, ga, qb, kb, vb = jnp.split(z, [A_W, 2 * A_W, 3 * A_W, 4 * A_W, 4 * A_W + B_W, 4 * A_W + 2 * B_W], axis=-1)

    def heads(a, nh):
        return jnp.moveaxis(a.reshape(B, T, nh, -1), 1, 2).astype(jnp.float32)

    lbh = lb.reshape(H_A, 1, DK_A)
    logf = jnp.logaddexp(jnp.log(lbh), jnp.log1p(-lbh) + jax.nn.log_sigmoid(heads(fa, H_A)))
    ka = -jnp.expm1(logf)
    va = jax.nn.silu(heads(ia, H_A))
    oa, S = gla_scan(heads(qa, H_A), ka, va, logf, s0.astype(jnp.float32))
    oa = head_rms_norm(oa, gn.reshape(H_A, 1, DV_A))
    oa = jnp.moveaxis(oa, 1, 2).reshape(B, T, A_W) * jax.nn.sigmoid(ga.astype(jnp.float32))
    qb = rope(qb.reshape(B, T, H_B, DH_B), pos, ROT_DIM, ROPE_THETA)
    kb = rope(kb.reshape(B, T, H_B, DH_B), pos, ROT_DIM, ROPE_THETA)
    vb = vb.reshape(B, T, H_B, DH_B)
    ob = attn(qb, kb, vb).reshape(B, T, B_W)
    y = jnp.concatenate([oa.astype(h.dtype), ob.astype(h.dtype)], axis=-1) @ w_out
    return y, S, kb, vb


def odd_mix(h, pos, w_in, w_out, gn, s0):
    B, T, _ = h.shape
    z = h @ w_in
    q, k, v, g = jnp.split(z, [H_C * DK_C, 2 * H_C * DK_C, 2 * H_C * DK_C + H_C * DV_C], axis=-1)
    q = rope(q.reshape(B, T, H_C, DK_C), pos, DK_C, RET_THETA)
    k = rope(k.reshape(B, T, H_C, DK_C), pos, DK_C, RET_THETA)
    v = v.reshape(B, T, H_C, DV_C)

    def heads(a):
        return jnp.moveaxis(a, 1, 2).astype(jnp.float32)

    o, S = retention_scan(heads(q), heads(k) * (DK_C ** -0.5), heads(v), s0.astype(jnp.float32))
    o = head_layer_norm(o, gn.reshape(H_C, 1, DV_C))
    o = jnp.moveaxis(o, 1, 2).reshape(B, T, H_C * DV_C) * jax.nn.silu(g.astype(jnp.float32))
    return o.astype(h.dtype) @ w_out, S


def conv_ffn(h, w_in, cw, cb, w_out, buf):
    T = h.shape[1]
    up = h @ w_in
    a, b = up[..., :D_FF], up[..., D_FF:]
    ap = jnp.concatenate([buf.astype(a.dtype), a], axis=1)
    conv = cb.astype(a.dtype)
    for j in range(CONV_W):
        conv = conv + cw[j] * ap[:, j:j + T]
    y = (jax.nn.silu(conv) * b) @ w_out
    return y, ap[:, T:]


def run_trunk(x, pos, attn, s_hgrn, s_ret, s_conv, weights):
    (lb_all, norm_mix, norm_ffn, norm_final, w_in_even, w_out_even, gn_hgrn,
     w_in_odd, w_out_odd, gn_ret, w_ffn_in, conv_w, conv_b, w_ffn_out) = weights
    new_k, new_v, new_h, new_r, new_c = [], [], [], [], []
    for l in range(DEPTH):
        h = rms_norm(x, norm_mix[l])
        if l % 2 == 0:
            e = l // 2

            def attn_e(q, k, v, e=e):
                return attn(e, q, k, v)

            y, S, k_rows, v_rows = even_mix(h, pos, w_in_even[e], w_out_even[e], gn_hgrn[e], lb_all[e], s_hgrn[e], attn_e)
            new_h.append(S)
            new_k.append(k_rows)
            new_v.append(v_rows)
        else:
            r = l // 2
            y, S = odd_mix(h, pos, w_in_odd[r], w_out_odd[r], gn_ret[r], s_ret[r])
            new_r.append(S)
        x = x + y.astype(x.dtype)
        h = rms_norm(x, norm_ffn[l])
        y, buf = conv_ffn(h, w_ffn_in[l], conv_w[l], conv_b[l], w_ffn_out[l], s_conv[l])
        x = x + y.astype(x.dtype)
        new_c.append(buf)
    return (rms_norm(x, norm_final), jnp.stack(new_k), jnp.stack(new_v),
            jnp.stack(new_h), jnp.stack(new_r), jnp.stack(new_c))


def setup_inputs(seed: int = 0) -> dict:
    key = jax.random.key(seed)
    ks = jax.random.split(key, 24)
    f32 = jnp.float32
    n_pages = PAST_LEN // PAGE_SIZE
    used = DEC_BATCH * n_pages
    n_pool = used + max(1, used // 4)

    def nrm(k, shape, s):
        return jax.random.normal(k, shape, f32) * s

    perm = jax.random.permutation(ks[0], n_pool)
    page_table = perm[:used].reshape(DEC_BATCH, n_pages).astype(jnp.int32)
    return {
        'x_prompt': nrm(ks[1], (BATCH, SEQ, D_MODEL), 1.0),
        'x_sample': nrm(ks[2], (DEC_BATCH, DEC_SEQ, D_MODEL), 1.0),
        'cache_k': nrm(ks[3], (N_EVEN, n_pool, PAGE_SIZE, H_B, DH_B), 1.0),
        'cache_v': nrm(ks[4], (N_EVEN, n_pool, PAGE_SIZE, H_B, DH_B), 1.0),
        'page_table': page_table,
        'state_hgrn': nrm(ks[5], (N_EVEN, DEC_BATCH, H_A, DK_A, DV_A), 0.5),
        'state_ret': nrm(ks[6], (N_ODD, DEC_BATCH, H_C, DK_C, DV_C), 1.0),
        'state_conv': nrm(ks[7], (DEPTH, DEC_BATCH, CONV_W - 1, D_FF), 1.0),
        'norm_mix': 1.0 + nrm(ks[8], (DEPTH, D_MODEL), 0.05),
        'norm_ffn': 1.0 + nrm(ks[9], (DEPTH, D_MODEL), 0.05),
        'norm_final': 1.0 + nrm(ks[10], (D_MODEL,), 0.05),
        'w_in_even': nrm(ks[11], (N_EVEN, D_MODEL, EVEN_IN), D_MODEL ** -0.5),
        'w_out_even': nrm(ks[12], (N_EVEN, A_W + B_W, D_MODEL), (A_W + B_W) ** -0.5),
        'gn_hgrn': 1.0 + nrm(ks[13], (N_EVEN, A_W), 0.05),
        'lb_hgrn': nrm(ks[14], (N_EVEN, A_W), 0.5),
        'w_in_odd': nrm(ks[15], (N_ODD, D_MODEL, ODD_IN), D_MODEL ** -0.5),
        'w_out_odd': nrm(ks[16], (N_ODD, H_C * DV_C, D_MODEL), (H_C * DV_C) ** -0.5),
        'gn_ret': 1.0 + nrm(ks[17], (N_ODD, H_C * DV_C), 0.05),
        'w_ffn_in': nrm(ks[18], (DEPTH, D_MODEL, 2 * D_FF), D_MODEL ** -0.5),
        'conv_w': nrm(ks[19], (DEPTH, CONV_W, D_FF), CONV_W ** -0.5),
        'conv_b': nrm(ks[20], (DEPTH, D_FF), 0.01),
        'w_ffn_out': nrm(ks[21], (DEPTH, D_FF, D_MODEL), D_FF ** -0.5),
    }


def reference(x_prompt, x_sample, cache_k, cache_v, page_table, state_hgrn, state_ret, state_conv,
              norm_mix, norm_ffn, norm_final, w_in_even, w_out_even, gn_hgrn, lb_hgrn,
              w_in_odd, w_out_odd, gn_ret, w_ffn_in, conv_w, conv_b, w_ffn_out):
    lb_cum = jnp.cumsum(jax.nn.softmax(lb_hgrn.astype(jnp.float32), axis=0), axis=0)
    lb_all = lb_cum - lb_cum[0:1]
    weights = (lb_all, norm_mix, norm_ffn, norm_final, w_in_even, w_out_even, gn_hgrn,
               w_in_odd, w_out_odd, gn_ret, w_ffn_in, conv_w, conv_b, w_ffn_out)
    bp, tp = x_prompt.shape[0], x_prompt.shape[1]
    ts = x_sample.shape[1]
    h0 = jnp.zeros((N_EVEN, bp, H_A, DK_A, DV_A), jnp.float32)
    r0 = jnp.zeros((N_ODD, bp, H_C, DK_C, DV_C), jnp.float32)
    c0 = jnp.zeros((DEPTH, bp, CONV_W - 1, D_FF), x_prompt.dtype)

    def attn_prompt(e, q, k, v):
        return moba_prompt(q, k, v)

    def attn_sample(e, q, k, v):
        return moba_sample(q, k, v, cache_k, cache_v, e, page_table)

    y_p, k_p, v_p, h_p, r_p, c_p = run_trunk(x_prompt, jnp.arange(tp), attn_prompt, h0, r0, c0, weights)
    y_s, k_s, v_s, h_s, r_s, c_s = run_trunk(x_sample, PAST_LEN + jnp.arange(ts), attn_sample,
                                             state_hgrn, state_ret, state_conv, weights)
    return (y_p, y_s, k_p, v_p, k_s, v_s, h_p, h_s, r_p, r_s, c_p, c_s)
```

```python
import functools

import numpy as np
import jax
import jax.numpy as jnp
from jax import lax
from jax.experimental import pallas as pl
from jax.experimental.pallas import tpu as pltpu

F32 = jnp.float32
BF16 = jnp.bfloat16

HEAD_DIM = 128
MOBA_BLOCK = 256
MOBA_TOPK = 3
PAGE_SIZE = 128
PAGES_PER_BLOCK = MOBA_BLOCK // PAGE_SIZE
ROT_DIM = HEAD_DIM // 4
ROPE_THETA = 500000.0
RET_THETA = 10000.0
H_C = 8
SCAN_CHUNK = 64
SUB_CHUNK = 16
CONV_W = 3
EPS = 1e-6
LANES = 128
VMEM_LIMIT = 56 * 1024 * 1024
NEG_INF = float("-inf")
HIGHEST = lax.Precision.HIGHEST


def _params(*sem):
    return pltpu.CompilerParams(dimension_semantics=sem, vmem_limit_bytes=VMEM_LIMIT)


def _dot(a, b, precision=None):
    return jnp.dot(a, b, preferred_element_type=F32, precision=precision)


def _dot_nt(a, b, precision=None):
    return lax.dot_general(a, b, (((1,), (1,)), ((), ())), preferred_element_type=F32, precision=precision)


def _dot_tn(a, b, precision=None):
    return lax.dot_general(a, b, (((0,), (0,)), ((), ())), preferred_element_type=F32, precision=precision)


def _sigmoid(x):
    return 1.0 / (1.0 + jnp.exp(-x))


def _silu(x):
    return x * _sigmoid(x)


def _log_sigmoid(x):
    return jnp.minimum(x, 0.0) - jnp.log1p(jnp.exp(-jnp.abs(x)))


def _logaddexp(a, b):
    return jnp.maximum(a, b) + jnp.log1p(jnp.exp(-jnp.abs(a - b)))


def _row_to_col(row, n):
    r = lax.broadcasted_iota(jnp.int32, (n, n), 0)
    c = lax.broadcasted_iota(jnp.int32, (n, n), 1)
    return jnp.sum(jnp.where(r == c, jnp.broadcast_to(row, (n, n)), 0.0), axis=1, keepdims=True)


def _rmsnorm_kernel(x_ref, g_ref, o_ref):
    x = x_ref[...]
    y = x * lax.rsqrt(jnp.mean(x * x, axis=-1, keepdims=True) + EPS)
    o_ref[...] = (y * g_ref[...]).astype(o_ref.dtype)


def _rmsnorm(x, g, out_dtype):
    m, d = x.shape
    tm = min(m, 512)
    return pl.pallas_call(
        _rmsnorm_kernel,
        out_shape=jax.ShapeDtypeStruct((m, d), out_dtype),
        grid=(m // tm,),
        in_specs=[pl.BlockSpec((tm, d), lambda i: (i, 0)), pl.BlockSpec((1, d), lambda i: (0, 0))],
        out_specs=pl.BlockSpec((tm, d), lambda i: (i, 0)),
        compiler_params=_params("arbitrary"),
        name="rmsnorm",
    )(x, g.reshape(1, d))


def _mm_kernel(a_ref, w_ref, *rest, has_res):
    if has_res:
        r_ref, o_ref, wb_ref = rest
    else:
        o_ref, wb_ref = rest

    @pl.when(pl.program_id(1) == 0)
    def _():
        wb_ref[...] = w_ref[...].astype(BF16)

    acc = _dot(a_ref[...], wb_ref[...])
    if has_res:
        acc = r_ref[...] + acc
    o_ref[...] = acc.astype(o_ref.dtype)


def _mm(a, w, res=None, out_dtype=F32):
    m, k = a.shape
    n = w.shape[1]
    tm = min(m, 512)
    tn = 1024 if k <= 2048 else 512
    tn = min(tn, n)
    assert m % tm == 0 and n % tn == 0
    in_specs = [pl.BlockSpec((tm, k), lambda j, i: (i, 0)), pl.BlockSpec((k, tn), lambda j, i: (0, j))]
    args = [a, w]
    if res is not None:
        in_specs.append(pl.BlockSpec((tm, tn), lambda j, i: (i, j)))
        args.append(res)
    return pl.pallas_call(
        functools.partial(_mm_kernel, has_res=res is not None),
        out_shape=jax.ShapeDtypeStruct((m, n), out_dtype),
        grid=(n // tn, m // tm),
        in_specs=in_specs,
        out_specs=pl.BlockSpec((tm, tn), lambda j, i: (i, j)),
        scratch_shapes=[pltpu.VMEM((k, tn), BF16)],
        compiler_params=_params("arbitrary", "arbitrary"),
        name="matmul",
    )(*args)


def _hgrn_lower_bound(lbp, e):
    mx = jnp.max(lbp, axis=0, keepdims=True)
    ex = jnp.exp(lbp - mx)
    sm = ex / jnp.sum(ex, axis=0, keepdims=True)
    lb = jnp.zeros_like(sm[0:1])
    for l in range(1, e + 1):
        lb = lb + sm[l:l + 1]
    return lb


def _hgrn_gates(zf, zi, lb):
    logf = _logaddexp(jnp.log(lb), jnp.log1p(-lb) + _log_sigmoid(zf))
    return logf, 1.0 - jnp.exp(logf), _silu(zi)


def _hgrn_prompt_kernel(q_ref, f_ref, i_ref, g_ref, lbp_ref, gn_ref, o_ref, s_ref, st_ref, *, e, n_chunks):
    c = pl.program_id(2)
    C = SCAN_CHUNK
    cs = SUB_CHUNK

    @pl.when(c == 0)
    def _():
        st_ref[...] = jnp.zeros_like(st_ref)

    lb = _hgrn_lower_bound(lbp_ref[...], e)
    gn = gn_ref[...]
    r_i = lax.broadcasted_iota(jnp.int32, (C, C), 0)
    c_i = lax.broadcasted_iota(jnp.int32, (C, C), 1)
    tri = (c_i <= r_i).astype(F32)
    sub_row = lax.broadcasted_iota(jnp.int32, (cs, HEAD_DIM), 0)

    def chunk(ci, st):
        rows = pl.ds(pl.multiple_of(ci * C, C), C)
        zq = q_ref[rows, :]
        logf, kk, vv = _hgrn_gates(f_ref[rows, :], i_ref[rows, :], lb)
        b = _dot(tri, logf, HIGHEST)
        o_inter = _dot_nt((zq * jnp.exp(b)).astype(BF16), st.astype(BF16))
        parts = []
        for i in range(C // cs):
            r0 = i * cs
            qi, bi = zq[r0:r0 + cs], b[r0:r0 + cs]
            ki, vi = kk[r0:r0 + cs], vv[r0:r0 + cs]
            oi = o_inter[r0:r0 + cs]
            if i > 0:
                ref_b = b[r0 - 1:r0]
                qt = qi * jnp.exp(bi - ref_b)
                kt = kk[:r0] * jnp.exp(ref_b - b[:r0])
                att = _dot_nt(qt.astype(BF16), kt.astype(BF16))
                oi = oi + _dot(att.astype(BF16), vv[:r0].astype(BF16))
            for s in range(cs):
                dec = jnp.exp(jnp.where(sub_row >= s, bi - bi[s:s + 1], NEG_INF))
                w = jnp.sum(qi * ki[s:s + 1] * dec, axis=1, keepdims=True)
                oi = oi + w * vi[s:s + 1]
            parts.append(oi)
        o = jnp.concatenate(parts, axis=0)
        on = o * lax.rsqrt(jnp.mean(o * o, axis=1, keepdims=True) + EPS) * gn
        o_ref[rows, :] = (on * _sigmoid(g_ref[rows, :])).astype(o_ref.dtype)
        bl = b[C - 1:C]
        khat = kk * jnp.exp(bl - b)
        return st * jnp.exp(bl) + _dot_tn(vv.astype(BF16), khat.astype(BF16))

    st = lax.fori_loop(0, n_chunks, chunk, st_ref[...])
    st_ref[...] = st

    @pl.when(c == pl.num_programs(2) - 1)
    def _():
        s_ref[...] = st.T


def _hgrn_prompt(z, lb_hgrn, gn, e, n_heads):
    bsz, t, _ = z.shape
    ct = min(t, 512)
    assert t % ct == 0 and ct % SCAN_CHUNK == 0
    n_even = lb_hgrn.shape[0]
    hd = HEAD_DIM

    def col(off):
        return pl.BlockSpec((None, ct, hd), lambda b, h, c, off=off: (b, c, off + h))

    return pl.pallas_call(
        functools.partial(_hgrn_prompt_kernel, e=e, n_chunks=ct // SCAN_CHUNK),
        out_shape=(jax.ShapeDtypeStruct((bsz, t, n_heads * hd), BF16),
                   jax.ShapeDtypeStruct((bsz, n_heads, hd, hd), F32)),
        grid=(bsz, n_heads, t // ct),
        in_specs=[col(0), col(n_heads), col(2 * n_heads), col(3 * n_heads),
                  pl.BlockSpec((n_even, hd), lambda b, h, c: (0, h)),
                  pl.BlockSpec((1, hd), lambda b, h, c: (0, h))],
        out_specs=(pl.BlockSpec((None, ct, hd), lambda b, h, c: (b, c, h)),
                   pl.BlockSpec((None, None, hd, hd), lambda b, h, c: (b, h, 0, 0))),
        scratch_shapes=[pltpu.VMEM((hd, hd), F32)],
        compiler_params=_params("arbitrary", "arbitrary", "arbitrary"),
        name="hgrn_prompt",
    )(z, z, z, z, lb_hgrn, gn.reshape(1, -1))


def _hgrn_sample_kernel(z_ref, s0_ref, lbp_ref, gn_ref, o_ref, s_ref, *, e, n_heads):
    hd = HEAD_DIM
    lb_all = _hgrn_lower_bound(lbp_ref[...], e)
    for h in range(n_heads):
        lanes = slice(h * hd, (h + 1) * hd)
        zq = z_ref[h:h + 1, :]
        zg = z_ref[3 * n_heads + h:3 * n_heads + h + 1, :]
        logf, kk, vv = _hgrn_gates(z_ref[n_heads + h:n_heads + h + 1, :],
                                   z_ref[2 * n_heads + h:2 * n_heads + h + 1, :], lb_all[:, lanes])
        s_new = _row_to_col(jnp.exp(logf), hd) * s0_ref[h] + _row_to_col(kk, hd) * vv
        s_ref[h] = s_new
        o = jnp.sum(_row_to_col(zq, hd) * s_new, axis=0, keepdims=True)
        on = o * lax.rsqrt(jnp.mean(o * o, axis=1, keepdims=True) + EPS) * gn_ref[:, lanes]
        o_ref[h:h + 1, :] = on * _sigmoid(zg)


def _hgrn_sample(z3, s0, lb_hgrn, gn, e, n_heads):
    db, groups, hd = z3.shape
    n_even = lb_hgrn.shape[0]
    w = n_heads * hd
    return pl.pallas_call(
        functools.partial(_hgrn_sample_kernel, e=e, n_heads=n_heads),
        out_shape=(jax.ShapeDtypeStruct((db, n_heads, hd), F32),
                   jax.ShapeDtypeStruct((db, n_heads, hd, hd), F32)),
        grid=(db,),
        in_specs=[pl.BlockSpec((None, groups, hd), lambda b: (b, 0, 0)),
                  pl.BlockSpec((None, n_heads, hd, hd), lambda b: (b, 0, 0, 0)),
                  pl.BlockSpec((n_even, w), lambda b: (0, 0)),
                  pl.BlockSpec((1, w), lambda b: (0, 0))],
        out_specs=(pl.BlockSpec((None, n_heads, hd), lambda b: (b, 0, 0)),
                   pl.BlockSpec((None, n_heads, hd, hd), lambda b: (b, 0, 0, 0))),
        compiler_params=_params("arbitrary"),
        name="hgrn_sample",
    )(z3, s0, lb_hgrn, gn.reshape(1, -1))


def _moba_rope_tables(pos):
    half = ROT_DIM // 2
    inv = ROPE_THETA ** (-np.arange(half, dtype=np.float64) / half)
    ang = np.asarray(pos, np.float64)[:, None] * inv[None, :]
    cos, sin = np.cos(ang), np.sin(ang)
    t = ang.shape[0]
    c = np.ones((t, HEAD_DIM))
    s1 = np.zeros((t, HEAD_DIM))
    s2 = np.zeros((t, HEAD_DIM))
    c[:, :half] = cos
    c[:, half:2 * half] = cos
    s1[:, :half] = -sin
    s2[:, half:2 * half] = sin
    return tuple(jnp.asarray(a, F32) for a in (c, s1, s2))


def _ret_rope_tables(pos, dk):
    half = dk // 2
    inv = RET_THETA ** (-np.arange(half, dtype=np.float64) / half)
    ang = np.asarray(pos, np.float64)[:, None] * inv[None, :]
    return jnp.asarray(np.cos(ang), F32), jnp.asarray(np.sin(ang), F32)


def _moba_rope(x, c, s1, s2):
    n = x.shape[-1]
    half = ROT_DIM // 2
    return x * c + pltpu.roll(x, n - half, 1) * s1 + pltpu.roll(x, half, 1) * s2


def _ret_rope(x, c, s):
    half = x.shape[-1] // 2
    x1, x2 = x[:, :half], x[:, half:]
    return jnp.concatenate([x1 * c - x2 * s, x2 * c + x1 * s], axis=1)


def _moba_prompt_kernel(q_ref, k_ref, v_ref, c_ref, s1_ref, s2_ref, o_ref, ko_ref, vo_ref,
                        kb_s, vb_s, km_s, m_s, l_s, acc_s, *, nb, topk, scale):
    i = pl.program_id(2)
    blk = MOBA_BLOCK

    @pl.when(i == 0)
    def _():
        km_s[...] = jnp.zeros_like(km_s)
        for j in range(nb):
            rows = slice(j * blk, (j + 1) * blk)
            kr = _moba_rope(k_ref[rows, :], c_ref[rows, :], s1_ref[rows, :], s2_ref[rows, :])
            ko_ref[rows, :] = kr
            kb_s[rows, :] = kr.astype(BF16)
            km_s[j:j + 1, :] = jnp.mean(kr, axis=0, keepdims=True)
            v = v_ref[rows, :]
            vo_ref[rows, :] = v
            vb_s[rows, :] = v.astype(BF16)

    rows = pl.ds(pl.multiple_of(i * blk, blk), blk)
    qr = _moba_rope(q_ref[...], c_ref[rows, :], s1_ref[rows, :], s2_ref[rows, :])
    gate = _dot_nt(qr, km_s[...], HIGHEST)
    lane = lax.broadcasted_iota(jnp.int32, gate.shape, 1)
    rank = jnp.zeros(gate.shape, F32)
    for jp in range(nb - 1):
        gj = gate[:, jp:jp + 1]
        beats = jnp.where(gj > gate, 1.0, jnp.where(gj == gate, jnp.where(lane > jp, 1.0, 0.0), 0.0))
        rank = rank + jnp.where(jp < i, beats, 0.0)
    sel = jnp.where(lane < i, jnp.where(rank < topk, 1.0, 0.0), 0.0)

    qb = qr.astype(BF16)
    s = _dot_nt(qb, kb_s[rows, :]) * scale
    r_i = lax.broadcasted_iota(jnp.int32, s.shape, 0)
    c_i = lax.broadcasted_iota(jnp.int32, s.shape, 1)
    s = jnp.where(c_i <= r_i, s, NEG_INF)
    m = jnp.max(s, axis=1, keepdims=True)
    p = jnp.exp(s - m)
    m_s[...] = m
    l_s[...] = jnp.sum(p, axis=1, keepdims=True)
    acc_s[...] = _dot(p.astype(BF16), vb_s[rows, :])

    for j in range(nb - 1):
        @pl.when(j < i)
        def _(j=j):
            krows = slice(j * blk, (j + 1) * blk)
            sj = _dot_nt(qb, kb_s[krows, :]) * scale
            sj = jnp.where(sel[:, j:j + 1] > 0.0, sj, NEG_INF)
            m_old = m_s[...]
            m_new = jnp.maximum(m_old, jnp.max(sj, axis=1, keepdims=True))
            alpha = jnp.exp(m_old - m_new)
            pj = jnp.exp(sj - m_new)
            l_s[...] = alpha * l_s[...] + jnp.sum(pj, axis=1, keepdims=True)
            acc_s[...] = alpha * acc_s[...] + _dot(pj.astype(BF16), vb_s[krows, :])
            m_s[...] = m_new

    o_ref[...] = (acc_s[...] / l_s[...]).astype(o_ref.dtype)


def _moba_prompt(z, tables, n_heads, q_off):
    bsz, t, _ = z.shape
    hd = HEAD_DIM
    blk = MOBA_BLOCK
    assert t % blk == 0
    nb = t // blk
    topk = max(1, min(MOBA_TOPK, nb - 1))
    c, s1, s2 = tables
    full = pl.BlockSpec((t, hd), lambda b, h, i: (0, 0))

    def seq(off):
        return pl.BlockSpec((None, t, hd), lambda b, h, i, off=off: (b, 0, off + h))

    return pl.pallas_call(
        functools.partial(_moba_prompt_kernel, nb=nb, topk=topk, scale=hd ** -0.5),
        out_shape=(jax.ShapeDtypeStruct((bsz, t, n_heads * hd), BF16),
                   jax.ShapeDtypeStruct((bsz, t, n_heads * hd), F32),
                   jax.ShapeDtypeStruct((bsz, t, n_heads * hd), F32)),
        grid=(bsz, n_heads, nb),
        in_specs=[pl.BlockSpec((None, blk, hd), lambda b, h, i: (b, i, q_off + h)),
                  seq(q_off + n_heads), seq(q_off + 2 * n_heads), full, full, full],
        out_specs=(pl.BlockSpec((None, blk, hd), lambda b, h, i: (b, i, h)), seq(0), seq(0)),
        scratch_shapes=[pltpu.VMEM((t, hd), BF16), pltpu.VMEM((t, hd), BF16), pltpu.VMEM((LANES, hd), F32),
                        pltpu.VMEM((blk, 1), F32), pltpu.VMEM((blk, 1), F32), pltpu.VMEM((blk, hd), F32)],
        compiler_params=_params("arbitrary", "arbitrary", "arbitrary"),
        name="moba_prompt",
    )(z, z, z, c, s1, s2)


GATE_PAGES_PER_STEP = 8


def _moba_gate_kernel(pt_ref, q_ref, c_ref, s1_ref, s2_ref, *rest, n_heads, n_blocks, topk):
    pages = rest[:GATE_PAGES_PER_STEP]
    gate_ref, idx_ref = rest[GATE_PAGES_PER_STEP:]
    s = pl.program_id(1)
    hd = HEAD_DIM
    bps = GATE_PAGES_PER_STEP // PAGES_PER_BLOCK

    @pl.when(s == 0)
    def _():
        gate_ref[...] = jnp.full(gate_ref.shape, NEG_INF, F32)

    q = _moba_rope(q_ref[...], c_ref[...], s1_ref[...], s2_ref[...])
    lane = lax.broadcasted_iota(jnp.int32, (1, LANES), 1)
    for blk in range(bps):
        ksum = jnp.zeros_like(q)
        for p in range(PAGES_PER_BLOCK):
            ksum = ksum + jnp.sum(pages[blk * PAGES_PER_BLOCK + p][...], axis=0, keepdims=True)
        prod = q * (ksum * (1.0 / MOBA_BLOCK))
        j = s * bps + blk
        for h in range(n_heads):
            val = jnp.sum(prod[:, h * hd:(h + 1) * hd], axis=1, keepdims=True)
            gate_ref[h:h + 1, :] = jnp.where(lane == j, val, gate_ref[h:h + 1, :])

    @pl.when(s == pl.num_programs(1) - 1)
    def _():
        g = gate_ref[...]
        lane_f = lax.broadcasted_iota(jnp.int32, g.shape, 1).astype(F32)
        out = jnp.zeros(g.shape, F32)
        for r in range(topk):
            mx = jnp.max(g, axis=1, keepdims=True)
            am = jnp.min(jnp.where(g == mx, lane_f, float(LANES)), axis=1, keepdims=True)
            out = jnp.where(lane_f == float(r), am, out)
            g = jnp.where(lane_f == am, NEG_INF, g)
        idx_ref[...] = out.astype(jnp.int32)


def _moba_sample_gate(z_s, cache_k4, page_table, e, tables_wide, n_heads, q_off_wide):
    db = z_s.shape[0]
    n_pages = page_table.shape[1]
    w = n_heads * HEAD_DIM
    n_blocks = n_pages // PAGES_PER_BLOCK
    assert n_pages % GATE_PAGES_PER_STEP == 0 and n_blocks <= LANES
    topk = min(MOBA_TOPK, n_blocks)
    steps = n_pages // GATE_PAGES_PER_STEP
    row = pl.BlockSpec((1, w), lambda b, s, pt: (0, 0))

    def page(p):
        return pl.BlockSpec((None, None, PAGE_SIZE, w),
                            lambda b, s, pt, p=p: (e, pt[b * n_pages + s * GATE_PAGES_PER_STEP + p], 0, 0))

    gate, idx = pl.pallas_call(
        functools.partial(_moba_gate_kernel, n_heads=n_heads, n_blocks=n_blocks, topk=topk),
        out_shape=(jax.ShapeDtypeStruct((db, n_heads, LANES), F32),
                   jax.ShapeDtypeStruct((db, n_heads, LANES), jnp.int32)),
        grid_spec=pltpu.PrefetchScalarGridSpec(
            num_scalar_prefetch=1,
            grid=(db, steps),
            in_specs=[pl.BlockSpec((None, 1, w), lambda b, s, pt: (b, 0, q_off_wide)), row, row, row]
                     + [page(p) for p in range(GATE_PAGES_PER_STEP)],
            out_specs=(pl.BlockSpec((None, n_heads, LANES), lambda b, s, pt: (b, 0, 0)),
                       pl.BlockSpec((None, n_heads, LANES), lambda b, s, pt: (b, 0, 0)))),
        compiler_params=_params("arbitrary", "arbitrary"),
        name="moba_sample_gate",
    )(page_table.reshape(-1), z_s.reshape(db, 1, -1), *tables_wide, *([cache_k4] * GATE_PAGES_PER_STEP))
    return idx[:, :, :topk]


def _moba_sample_attn_kernel(pg_ref, z_ref, c_ref, s1_ref, s2_ref, kp_ref, vp_ref, o_ref, kn_ref, vn_ref,
                             m_s, l_s, acc_s, *, n_heads, q_off, scale):
    h = pl.program_id(1)
    s = pl.program_id(2)
    q = _moba_rope(z_ref[pl.ds(q_off + h, 1), :], c_ref[...], s1_ref[...], s2_ref[...])

    @pl.when(s == 0)
    def _():
        kn = _moba_rope(z_ref[pl.ds(q_off + n_heads + h, 1), :], c_ref[...], s1_ref[...], s2_ref[...])
        vn = z_ref[pl.ds(q_off + 2 * n_heads + h, 1), :]
        kn_ref[...] = kn
        vn_ref[...] = vn
        m_s[...] = jnp.sum(q * kn, axis=1, keepdims=True) * scale
        l_s[...] = jnp.ones_like(l_s)
        acc_s[...] = vn

    q8 = jnp.broadcast_to(q, (8, q.shape[1]))
    sc = _dot_nt(q8, kp_ref[...], HIGHEST)[0:1] * scale
    m_old = m_s[...]
    m_new = jnp.maximum(m_old, jnp.max(sc, axis=1, keepdims=True))
    alpha = jnp.exp(m_old - m_new)
    p = jnp.exp(sc - m_new)
    l_s[...] = alpha * l_s[...] + jnp.sum(p, axis=1, keepdims=True)
    pv = _dot(jnp.broadcast_to(p, (8, p.shape[1])), vp_ref[...], HIGHEST)[0:1]
    acc_s[...] = alpha * acc_s[...] + pv
    m_s[...] = m_new

    @pl.when(s == pl.num_programs(2) - 1)
    def _():
        o_ref[...] = acc_s[...] / l_s[...]


def _moba_sample_attn(z3, cache_k4, cache_v4, pages, e, tables, n_heads, q_off):
    db, groups, hd = z3.shape
    n_sel = pages.shape[2]
    row = pl.BlockSpec((1, hd), lambda b, h, s, pg: (0, 0))

    def page_spec():
        return pl.BlockSpec((None, None, PAGE_SIZE, hd),
                            lambda b, h, s, pg: (e, pg[(b * n_heads + h) * n_sel + s], 0, h))

    out = pl.BlockSpec((None, None, 1, hd), lambda b, h, s, pg: (b, h, 0, 0))
    shape = jax.ShapeDtypeStruct((db, n_heads, 1, hd), F32)
    return pl.pallas_call(
        functools.partial(_moba_sample_attn_kernel, n_heads=n_heads, q_off=q_off, scale=hd ** -0.5),
        out_shape=(shape, shape, shape),
        grid_spec=pltpu.PrefetchScalarGridSpec(
            num_scalar_prefetch=1,
            grid=(db, n_heads, n_sel),
            in_specs=[pl.BlockSpec((None, groups, hd), lambda b, h, s, pg: (b, 0, 0)), row, row, row,
                      page_spec(), page_spec()],
            out_specs=(out, out, out),
            scratch_shapes=[pltpu.VMEM((1, 1), F32), pltpu.VMEM((1, 1), F32), pltpu.VMEM((1, hd), F32)]),
        compiler_params=_params("arbitrary", "arbitrary", "arbitrary"),
        name="moba_sample_attn",
    )(pages.reshape(-1), z3, *tables, cache_k4, cache_v4)


RET_CHUNK = 256


def _ret_log_gamma(h, shape):
    hf = jnp.full(shape, h, jnp.int32).astype(F32)
    return jnp.log(1.0 - jnp.exp2(-5.0 - hf))


def _layer_norm_gate(o, gn, g):
    oc = o - jnp.mean(o, axis=1, keepdims=True)
    on = oc * lax.rsqrt(jnp.mean(oc * oc, axis=1, keepdims=True) + EPS) * gn
    return on * _silu(g)


def _ret_prompt_kernel(q_ref, k_ref, v_ref, g_ref, c_ref, s_ref, gn_ref, o_ref, so_ref, st_s, *, dk):
    h = pl.program_id(1)
    c = pl.program_id(2)
    C = RET_CHUNK

    @pl.when(c == 0)
    def _():
        st_s[...] = jnp.zeros_like(st_s)

    rows = pl.ds(pl.multiple_of(c * C, C), C)
    cos, sin = c_ref[rows, :], s_ref[rows, :]
    qr = _ret_rope(q_ref[...], cos, sin)
    kr = _ret_rope(k_ref[...], cos, sin) * (dk ** -0.5)
    v = v_ref[...]

    t_sq = lax.broadcasted_iota(jnp.int32, (C, C), 0)
    s_sq = lax.broadcasted_iota(jnp.int32, (C, C), 1)
    lg = _ret_log_gamma(h, (1, 1))
    dmat = jnp.exp(jnp.where(s_sq <= t_sq, (t_sq - s_sq).astype(F32) * lg, NEG_INF))
    t_w = lax.broadcasted_iota(jnp.int32, qr.shape, 0).astype(F32)
    inter = jnp.exp((t_w + 1.0) * lg)
    state_dec = jnp.exp((C - 1.0 - t_w) * lg)
    chunk_dec = jnp.exp(float(C) * lg)

    st = st_s[...]
    qb = qr.astype(BF16)
    att = _dot_nt(qb, kr.astype(BF16)) * dmat
    o = _dot(att.astype(BF16), v.astype(BF16)) + _dot(qb, st.astype(BF16)) * inter
    st_new = chunk_dec * st + _dot_tn((kr * state_dec).astype(BF16), v.astype(BF16))
    st_s[...] = st_new
    o_ref[...] = _layer_norm_gate(o, gn_ref[...], g_ref[...]).astype(o_ref.dtype)

    @pl.when(c == pl.num_programs(2) - 1)
    def _():
        so_ref[...] = st_new


def _ret_prompt(z, tables, gn, n_heads):
    bsz, t, width = z.shape
    dk = width // (4 * n_heads)
    C = RET_CHUNK
    assert t % C == 0
    cos, sin = tables
    full = pl.BlockSpec((t, dk // 2), lambda b, h, c: (0, 0))

    def col(off):
        return pl.BlockSpec((None, C, dk), lambda b, h, c, off=off: (b, c, off + h))

    return pl.pallas_call(
        functools.partial(_ret_prompt_kernel, dk=dk),
        out_shape=(jax.ShapeDtypeStruct((bsz, t, n_heads * dk), BF16),
                   jax.ShapeDtypeStruct((bsz, n_heads, dk, dk), F32)),
        grid=(bsz, n_heads, t // C),
        in_specs=[col(0), col(n_heads), col(2 * n_heads), col(3 * n_heads), full, full,
                  pl.BlockSpec((1, dk), lambda b, h, c: (0, h))],
        out_specs=(pl.BlockSpec((None, C, dk), lambda b, h, c: (b, c, h)),
                   pl.BlockSpec((None, None, dk, dk), lambda b, h, c: (b, h, 0, 0))),
        scratch_shapes=[pltpu.VMEM((dk, dk), F32)],
        compiler_params=_params("arbitrary", "arbitrary", "arbitrary"),
        name="retention_prompt",
    )(z, z, z, z, cos, sin, gn.reshape(1, -1))


def _ret_sample_kernel(z_ref, s0_ref, c_ref, s_ref, gn_ref, o_ref, so_ref, *, n_heads, dk):
    h = pl.program_id(1)
    cos, sin = c_ref[...], s_ref[...]
    qr = _ret_rope(z_ref[pl.ds(h, 1), :], cos, sin)
    kr = _ret_rope(z_ref[pl.ds(n_heads + h, 1), :], cos, sin) * (dk ** -0.5)
    v = z_ref[pl.ds(2 * n_heads + h, 1), :]
    g = z_ref[pl.ds(3 * n_heads + h, 1), :]
    gamma = jnp.exp(_ret_log_gamma(h, (1, 1)))
    s0 = s0_ref[...]
    qs = jnp.sum(_row_to_col(qr, dk) * s0, axis=0, keepdims=True)
    o = jnp.sum(qr * kr, axis=1, keepdims=True) * v + qs * gamma
    so_ref[...] = gamma * s0 + _row_to_col(kr, dk) * v
    o_ref[...] = _layer_norm_gate(o, gn_ref[pl.ds(h, 1), :], g)


def _ret_sample(z4, s0, tables, gn, n_heads):
    db, groups, dk = z4.shape
    cos, sin = tables
    row = pl.BlockSpec((1, dk // 2), lambda b, h: (0, 0))
    return pl.pallas_call(
        functools.partial(_ret_sample_kernel, n_heads=n_heads, dk=dk),
        out_shape=(jax.ShapeDtypeStruct((db, n_heads, 1, dk), F32),
                   jax.ShapeDtypeStruct((db, n_heads, dk, dk), F32)),
        grid=(db, n_heads),
        in_specs=[pl.BlockSpec((None, groups, dk), lambda b, h: (b, 0, 0)),
                  pl.BlockSpec((None, None, dk, dk), lambda b, h: (b, h, 0, 0)), row, row,
                  pl.BlockSpec((n_heads, dk), lambda b, h: (0, 0))],
        out_specs=(pl.BlockSpec((None, None, 1, dk), lambda b, h: (b, h, 0, 0)),
                   pl.BlockSpec((None, None, dk, dk), lambda b, h: (b, h, 0, 0))),
        compiler_params=_params("arbitrary", "arbitrary"),
        name="retention_sample",
    )(z4, s0, cos, sin, gn.reshape(n_heads, dk))


def _conv_gate_prompt_kernel(a_ref, g_ref, prev_ref, cw_ref, cb_ref, o_ref, nb_ref):
    t = pl.program_id(2)
    a = a_ref[...]
    tt = a.shape[0]
    has_prev = t > 0
    p0 = jnp.where(has_prev, prev_ref[6:7, :], 0.0)
    p1 = jnp.where(has_prev, prev_ref[7:8, :], 0.0)
    row = lax.broadcasted_iota(jnp.int32, a.shape, 0)
    a1 = jnp.where(row == 0, p1, pltpu.roll(a, 1, 0))
    a2 = jnp.where(row == 0, p0, jnp.where(row == 1, p1, pltpu.roll(a, 2, 0)))
    conv = cb_ref[...] + cw_ref[0:1, :] * a2
    conv = conv + cw_ref[1:2, :] * a1
    conv = conv + cw_ref[2:3, :] * a
    o_ref[...] = (_silu(conv) * g_ref[...]).astype(o_ref.dtype)

    @pl.when(t == pl.num_programs(2) - 1)
    def _():
        nb_ref[...] = a_ref[pl.ds(tt - (CONV_W - 1), CONV_W - 1), :]


def _conv_gate_prompt(up, cw, cb):
    bsz, t, f2 = up.shape
    f = f2 // 2
    tt = min(t, 256)
    tf = 512
    assert t % tt == 0 and f % tf == 0 and tt % 8 == 0 and t >= CONV_W - 1
    nf = f // tf
    return pl.pallas_call(
        _conv_gate_prompt_kernel,
        out_shape=(jax.ShapeDtypeStruct((bsz, t, f), BF16),
                   jax.ShapeDtypeStruct((bsz, CONV_W - 1, f), F32)),
        grid=(bsz, nf, t // tt),
        in_specs=[pl.BlockSpec((None, tt, tf), lambda b, j, i: (b, i, j)),
                  pl.BlockSpec((None, tt, tf), lambda b, j, i: (b, i, nf + j)),
                  pl.BlockSpec((None, 8, tf), lambda b, j, i: (b, jnp.maximum(i * (tt // 8) - 1, 0), j)),
                  pl.BlockSpec((CONV_W, tf), lambda b, j, i: (0, j)),
                  pl.BlockSpec((1, tf), lambda b, j, i: (0, j))],
        out_specs=(pl.BlockSpec((None, tt, tf), lambda b, j, i: (b, i, j)),
                   pl.BlockSpec((None, CONV_W - 1, tf), lambda b, j, i: (b, 0, j))),
        compiler_params=_params("arbitrary", "arbitrary", "arbitrary"),
        name="conv_gate_prompt",
    )(up, up, up, cw, cb.reshape(1, f))


def _conv_gate_sample_kernel(a_ref, g_ref, buf_ref, cw_ref, cb_ref, o_ref, nb_ref):
    a = a_ref[...]
    conv = cb_ref[...] + cw_ref[0:1, :] * buf_ref[0]
    conv = conv + cw_ref[1:2, :] * buf_ref[1]
    conv = conv + cw_ref[2:3, :] * a
    o_ref[...] = (_silu(conv) * g_ref[...]).astype(o_ref.dtype)
    nb_ref[0] = buf_ref[1]
    nb_ref[1] = a


def _conv_gate_sample(up, buf_t, cw, cb):
    db, f2 = up.shape
    f = f2 // 2
    tf = 512
    nf = f // tf
    return pl.pallas_call(
        _conv_gate_sample_kernel,
        out_shape=(jax.ShapeDtypeStruct((db, f), BF16),
                   jax.ShapeDtypeStruct((CONV_W - 1, db, f), F32)),
        grid=(nf,),
        in_specs=[pl.BlockSpec((db, tf), lambda j: (0, j)),
                  pl.BlockSpec((db, tf), lambda j: (0, nf + j)),
                  pl.BlockSpec((CONV_W - 1, db, tf), lambda j: (0, 0, j)),
                  pl.BlockSpec((CONV_W, tf), lambda j: (0, j)),
                  pl.BlockSpec((1, tf), lambda j: (0, j))],
        out_specs=(pl.BlockSpec((db, tf), lambda j: (0, j)),
                   pl.BlockSpec((CONV_W - 1, db, tf), lambda j: (0, 0, j))),
        compiler_params=_params("arbitrary"),
        name="conv_gate_sample",
    )(up, up, buf_t, cw, cb.reshape(1, f))


def _ffn_prompt(x, l, bsz, t, norm_ffn, w_ffn_in, conv_w, conv_b, w_ffn_out):
    h = _rmsnorm(x, norm_ffn[l], BF16)
    up = _mm(h, w_ffn_in[l])
    act, buf = _conv_gate_prompt(up.reshape(bsz, t, -1), conv_w[l], conv_b[l])
    return _mm(act.reshape(bsz * t, -1), w_ffn_out[l], res=x), buf


def _prompt_trunk(x_prompt, norm_mix, norm_ffn, norm_final, w_in_even, w_out_even, gn_hgrn, lb_hgrn,
                  w_in_odd, w_out_odd, gn_ret, w_ffn_in, conv_w, conv_b, w_ffn_out):
    bsz, t, d = x_prompt.shape
    depth = norm_mix.shape[0]
    n_a = gn_hgrn.shape[1] // HEAD_DIM
    n_b = (w_in_even.shape[2] - 4 * gn_hgrn.shape[1]) // (3 * HEAD_DIM)
    pos = np.arange(t)
    moba_tables = _moba_rope_tables(pos)
    ret_tables = _ret_rope_tables(pos, w_in_odd.shape[2] // (4 * H_C))
    x = x_prompt.reshape(bsz * t, d)
    new_k, new_v, new_h, new_r, new_c = [], [], [], [], []
    for l in range(depth):
        h = _rmsnorm(x, norm_mix[l], BF16)
        if l % 2 == 0:
            e = l // 2
            z = _mm(h, w_in_even[e]).reshape(bsz, t, -1)
            oa, s_h = _hgrn_prompt(z, lb_hgrn, gn_hgrn[e], e, n_a)
            ob, k_rot, v_rows = _moba_prompt(z, moba_tables, n_b, 4 * n_a)
            mix = jnp.concatenate([oa, ob], axis=-1).reshape(bsz * t, -1)
            x = _mm(mix, w_out_even[e], res=x)
            new_h.append(s_h)
            new_k.append(k_rot.reshape(bsz, t, n_b, HEAD_DIM))
            new_v.append(v_rows.reshape(bsz, t, n_b, HEAD_DIM))
        else:
            r = l // 2
            z = _mm(h, w_in_odd[r]).reshape(bsz, t, -1)
            o, s_r = _ret_prompt(z, ret_tables, gn_ret[r], H_C)
            x = _mm(o.reshape(bsz * t, -1), w_out_odd[r], res=x)
            new_r.append(s_r)
        x, buf = _ffn_prompt(x, l, bsz, t, norm_ffn, w_ffn_in, conv_w, conv_b, w_ffn_out)
        new_c.append(buf)
    y = _rmsnorm(x, norm_final, F32).reshape(bsz, t, d)
    return (y, jnp.stack(new_k), jnp.stack(new_v), jnp.stack(new_h), jnp.stack(new_r), jnp.stack(new_c))


def _sample_trunk(x_sample, cache_k, cache_v, page_table, state_hgrn, state_ret, state_conv,
                  norm_mix, norm_ffn, norm_final, w_in_even, w_out_even, gn_hgrn, lb_hgrn,
                  w_in_odd, w_out_odd, gn_ret, w_ffn_in, conv_w, conv_b, w_ffn_out):
    db, ts, d = x_sample.shape
    assert ts == 1
    depth = norm_mix.shape[0]
    n_a = gn_hgrn.shape[1] // HEAD_DIM
    n_b = (w_in_even.shape[2] - 4 * gn_hgrn.shape[1]) // (3 * HEAD_DIM)
    assert n_a == n_b
    n_pages = page_table.shape[1]
    assert n_pages % PAGES_PER_BLOCK == 0
    pos = np.array([n_pages * PAGE_SIZE])
    moba_tables = _moba_rope_tables(pos)
    moba_tables_wide = tuple(jnp.tile(a, (1, n_b)) for a in moba_tables)
    dk_c = w_in_odd.shape[2] // (4 * H_C)
    ret_tables = _ret_rope_tables(pos, dk_c)
    n_even, n_pool = cache_k.shape[0], cache_k.shape[1]
    cache_k4 = cache_k.reshape(n_even, n_pool, PAGE_SIZE, n_b * HEAD_DIM)
    cache_v4 = cache_v.reshape(n_even, n_pool, PAGE_SIZE, n_b * HEAD_DIM)
    x = x_sample.reshape(db, d)
    new_k, new_v, new_h, new_r, new_c = [], [], [], [], []
    for l in range(depth):
        h = _rmsnorm(x, norm_mix[l], BF16)
        if l % 2 == 0:
            e = l // 2
            z = _mm(h, w_in_even[e])
            z3 = z.reshape(db, -1, HEAD_DIM)
            oa, s_h = _hgrn_sample(z3, state_hgrn[e], lb_hgrn, gn_hgrn[e], e, n_a)
            idx = _moba_sample_gate(z, cache_k4, page_table, e, moba_tables_wide, n_b, 4 * n_a // n_b)
            logical = idx[..., None] * PAGES_PER_BLOCK + jnp.arange(PAGES_PER_BLOCK, dtype=jnp.int32)
            pages = jnp.take_along_axis(page_table[:, None, :], logical.reshape(db, n_b, -1), axis=2)
            ob, k_rot, v_rows = _moba_sample_attn(z3, cache_k4, cache_v4, pages, e, moba_tables, n_b, 4 * n_a)
            mix = jnp.concatenate([oa.reshape(db, -1), ob.reshape(db, -1)], axis=-1).astype(BF16)
            x = _mm(mix, w_out_even[e], res=x)
            new_h.append(s_h)
            new_k.append(k_rot.reshape(db, ts, n_b, HEAD_DIM))
            new_v.append(v_rows.reshape(db, ts, n_b, HEAD_DIM))
        else:
            r = l // 2
            z = _mm(h, w_in_odd[r])
            o, s_r = _ret_sample(z.reshape(db, 4 * H_C, dk_c), state_ret[r], ret_tables, gn_ret[r], H_C)
            x = _mm(o.reshape(db, -1).astype(BF16), w_out_odd[r], res=x)
            new_r.append(s_r)
        h = _rmsnorm(x, norm_ffn[l], BF16)
        up = _mm(h, w_ffn_in[l])
        act, buf_t = _conv_gate_sample(up, jnp.swapaxes(state_conv[l], 0, 1), conv_w[l], conv_b[l])
        x = _mm(act, w_ffn_out[l], res=x)
        new_c.append(jnp.swapaxes(buf_t, 0, 1))
    y = _rmsnorm(x, norm_final, F32).reshape(db, ts, d)
    return (y, jnp.stack(new_k), jnp.stack(new_v), jnp.stack(new_h), jnp.stack(new_r), jnp.stack(new_c))


def kernel(x_prompt, x_sample, cache_k, cache_v, page_table, state_hgrn, state_ret, state_conv,
           norm_mix, norm_ffn, norm_final, w_in_even, w_out_even, gn_hgrn, lb_hgrn,
           w_in_odd, w_out_odd, gn_ret, w_ffn_in, conv_w, conv_b, w_ffn_out):
    weights = (norm_mix, norm_ffn, norm_final, w_in_even, w_out_even, gn_hgrn, lb_hgrn,
               w_in_odd, w_out_odd, gn_ret, w_ffn_in, conv_w, conv_b, w_ffn_out)
    y_p, k_p, v_p, h_p, r_p, c_p = _prompt_trunk(x_prompt, *weights)
    y_s, k_s, v_s, h_s, r_s, c_s = _sample_trunk(x_sample, cache_k, cache_v, page_table,
                                                 state_hgrn, state_ret, state_conv, *weights)
    return (y_p, y_s, k_p, v_p, k_s, v_s, h_p, h_s, r_p, r_s, c_p, c_s)
```

```python
import functools

import numpy as np
import jax
import jax.numpy as jnp
from jax import lax
from jax.experimental import pallas as pl
from jax.experimental.pallas import tpu as pltpu

F32 = jnp.float32
BF16 = jnp.bfloat16

HEAD_DIM = 128
MOBA_BLOCK = 256
MOBA_TOPK = 3
PAGE_SIZE = 128
PAGES_PER_BLOCK = MOBA_BLOCK // PAGE_SIZE
ROT_DIM = HEAD_DIM // 4
ROPE_THETA = 500000.0
RET_THETA = 10000.0
H_C = 8
SCAN_CHUNK = 64
SUB_CHUNK = 16
SUBLANES = 8
CONV_W = 3
EPS = 1e-6
LANES = 128
VMEM_LIMIT = 56 * 1024 * 1024
NEG_INF = float("-inf")
HIGHEST = lax.Precision.HIGHEST


def _params(*sem):
    return pltpu.CompilerParams(dimension_semantics=sem, vmem_limit_bytes=VMEM_LIMIT)


def _dot(a, b, precision=None):
    return jnp.dot(a, b, preferred_element_type=F32, precision=precision)


def _dot_nt(a, b, precision=None):
    return lax.dot_general(a, b, (((1,), (1,)), ((), ())), preferred_element_type=F32, precision=precision)


def _dot_tn(a, b, precision=None):
    return lax.dot_general(a, b, (((0,), (0,)), ((), ())), preferred_element_type=F32, precision=precision)


def _sigmoid(x):
    return 1.0 / (1.0 + jnp.exp(-x))


def _silu(x):
    return x * _sigmoid(x)


def _log_sigmoid(x):
    return jnp.minimum(x, 0.0) - jnp.log1p(jnp.exp(-jnp.abs(x)))


def _logaddexp(a, b):
    return jnp.maximum(a, b) + jnp.log1p(jnp.exp(-jnp.abs(a - b)))


def _row_to_col(row, n):
    r = lax.broadcasted_iota(jnp.int32, (n, n), 0)
    c = lax.broadcasted_iota(jnp.int32, (n, n), 1)
    return jnp.sum(jnp.where(r == c, jnp.broadcast_to(row, (n, n)), 0.0), axis=1, keepdims=True)


def _rmsnorm_kernel(x_ref, g_ref, o_ref):
    x = x_ref[...]
    y = x * lax.rsqrt(jnp.mean(x * x, axis=-1, keepdims=True) + EPS)
    o_ref[...] = (y * g_ref[...]).astype(o_ref.dtype)


def _rmsnorm(x, g, l, out_dtype):
    m, d = x.shape
    tm = min(m, 512)
    return pl.pallas_call(
        _rmsnorm_kernel,
        out_shape=jax.ShapeDtypeStruct((m, d), out_dtype),
        grid=(m // tm,),
        in_specs=[pl.BlockSpec((tm, d), lambda i: (i, 0)), pl.BlockSpec((None, 1, d), lambda i: (l, 0, 0))],
        out_specs=pl.BlockSpec((tm, d), lambda i: (i, 0)),
        compiler_params=_params("arbitrary"),
        name="rmsnorm",
    )(x, g.reshape(-1, 1, d))


def _mm_kernel(a_ref, w_ref, *rest, has_res):
    if has_res:
        r_ref, o_ref, wb_ref = rest
    else:
        o_ref, wb_ref = rest

    @pl.when(pl.program_id(1) == 0)
    def _():
        wb_ref[...] = w_ref[...].astype(BF16)

    acc = _dot(a_ref[...], wb_ref[...])
    if has_res:
        acc = r_ref[...] + acc
    o_ref[...] = acc.astype(o_ref.dtype)


def _mm(a, w, l, res=None, out_dtype=F32):
    m, k = a.shape
    n = w.shape[2]
    tm = min(m, 512)
    tn = 1024 if k <= 2048 else 512
    tn = min(tn, n)
    assert m % tm == 0 and n % tn == 0
    in_specs = [pl.BlockSpec((tm, k), lambda j, i: (i, 0)), pl.BlockSpec((None, k, tn), lambda j, i: (l, 0, j))]
    args = [a, w]
    if res is not None:
        in_specs.append(pl.BlockSpec((tm, tn), lambda j, i: (i, j)))
        args.append(res)
    return pl.pallas_call(
        functools.partial(_mm_kernel, has_res=res is not None),
        out_shape=jax.ShapeDtypeStruct((m, n), out_dtype),
        grid=(n // tn, m // tm),
        in_specs=in_specs,
        out_specs=pl.BlockSpec((tm, tn), lambda j, i: (i, j)),
        scratch_shapes=[pltpu.VMEM((k, tn), BF16)],
        compiler_params=_params("arbitrary", "arbitrary"),
        name="matmul",
    )(*args)


def _conv_gate(a, g, p0, p1, cw_ref, cb_ref):
    row = lax.broadcasted_iota(jnp.int32, a.shape, 0)
    a1 = jnp.where(row == 0, p1, pltpu.roll(a, 1, 0))
    a2 = jnp.where(row == 0, p0, jnp.where(row == 1, p1, pltpu.roll(a, 2, 0)))
    conv = cb_ref[...] + cw_ref[0:1, :] * a2
    conv = conv + cw_ref[1:2, :] * a1
    conv = conv + cw_ref[2:3, :] * a
    return _silu(conv) * g


def _ffn_in_kernel(h_ref, wa_ref, wb_ref, cw_ref, cb_ref, o_ref, nb_ref, wa_s, wb_s, carry_s, *, tiles_per_seq):
    i = pl.program_id(1)

    @pl.when(i == 0)
    def _():
        wa_s[...] = wa_ref[...].astype(BF16)
        wb_s[...] = wb_ref[...].astype(BF16)

    h = h_ref[...]
    a = _dot(h, wa_s[...])
    g = _dot(h, wb_s[...])
    tm = a.shape[0]
    seq_start = (i % tiles_per_seq) == 0
    p0 = jnp.where(seq_start, 0.0, carry_s[SUBLANES - 2:SUBLANES - 1, :])
    p1 = jnp.where(seq_start, 0.0, carry_s[SUBLANES - 1:SUBLANES, :])
    o_ref[...] = _conv_gate(a, g, p0, p1, cw_ref, cb_ref).astype(o_ref.dtype)
    carry_s[...] = a[tm - SUBLANES:tm, :]

    @pl.when((i % tiles_per_seq) == tiles_per_seq - 1)
    def _():
        nb_ref[...] = a[tm - (CONV_W - 1):tm, :]


def _ffn_in_prompt(h, w, cw, cb, l, bsz, t):
    m, k = h.shape
    f = w.shape[2] // 2
    tm = min(t, 512)
    tn = 512
    assert t % tm == 0 and f % tn == 0 and tm >= SUBLANES
    nf = f // tn
    tiles_per_seq = t // tm
    return pl.pallas_call(
        functools.partial(_ffn_in_kernel, tiles_per_seq=tiles_per_seq),
        out_shape=(jax.ShapeDtypeStruct((m, f), BF16),
                   jax.ShapeDtypeStruct((bsz, CONV_W - 1, f), F32)),
        grid=(nf, m // tm),
        in_specs=[pl.BlockSpec((tm, k), lambda j, i: (i, 0)),
                  pl.BlockSpec((None, k, tn), lambda j, i: (l, 0, j)),
                  pl.BlockSpec((None, k, tn), lambda j, i: (l, 0, nf + j)),
                  pl.BlockSpec((None, CONV_W, tn), lambda j, i: (l, 0, j)),
                  pl.BlockSpec((None, 1, tn), lambda j, i: (l, 0, j))],
        out_specs=(pl.BlockSpec((tm, tn), lambda j, i: (i, j)),
                   pl.BlockSpec((None, CONV_W - 1, tn), lambda j, i: (i // tiles_per_seq, 0, j))),
        scratch_shapes=[pltpu.VMEM((k, tn), BF16), pltpu.VMEM((k, tn), BF16), pltpu.VMEM((SUBLANES, tn), F32)],
        compiler_params=_params("arbitrary", "arbitrary"),
        name="ffn_in_conv",
    )(h, w, w, cw, cb.reshape(-1, 1, f))


def _conv_gate_sample_kernel(a_ref, g_ref, buf_ref, cw_ref, cb_ref, o_ref, nb_ref):
    a = a_ref[...]
    conv = cb_ref[...] + cw_ref[0:1, :] * buf_ref[0]
    conv = conv + cw_ref[1:2, :] * buf_ref[1]
    conv = conv + cw_ref[2:3, :] * a
    o_ref[...] = (_silu(conv) * g_ref[...]).astype(o_ref.dtype)
    nb_ref[0] = buf_ref[1]
    nb_ref[1] = a


def _conv_gate_sample(up, buf_t, cw, cb, l):
    db, f2 = up.shape
    f = f2 // 2
    tf = 512
    nf = f // tf
    return pl.pallas_call(
        _conv_gate_sample_kernel,
        out_shape=(jax.ShapeDtypeStruct((db, f), BF16),
                   jax.ShapeDtypeStruct((CONV_W - 1, db, f), F32)),
        grid=(nf,),
        in_specs=[pl.BlockSpec((db, tf), lambda j: (0, j)),
                  pl.BlockSpec((db, tf), lambda j: (0, nf + j)),
                  pl.BlockSpec((None, CONV_W - 1, db, tf), lambda j: (l, 0, 0, j)),
                  pl.BlockSpec((None, CONV_W, tf), lambda j: (l, 0, j)),
                  pl.BlockSpec((None, 1, tf), lambda j: (l, 0, j))],
        out_specs=(pl.BlockSpec((db, tf), lambda j: (0, j)),
                   pl.BlockSpec((CONV_W - 1, db, tf), lambda j: (0, 0, j))),
        compiler_params=_params("arbitrary"),
        name="conv_gate_sample",
    )(up, up, buf_t, cw, cb.reshape(-1, 1, f))


HGRN_HEADS_PER_STEP = 2


def _hgrn_lower_bound(lbp, e):
    mx = jnp.max(lbp, axis=0, keepdims=True)
    ex = jnp.exp(lbp - mx)
    sm = ex / jnp.sum(ex, axis=0, keepdims=True)
    lb = jnp.zeros_like(sm[0:1])
    for l in range(1, e + 1):
        lb = lb + sm[l:l + 1]
    return lb


def _hgrn_gates(zf, zi, lb):
    logf = _logaddexp(jnp.log(lb), jnp.log1p(-lb) + _log_sigmoid(zf))
    return logf, 1.0 - jnp.exp(logf), _silu(zi)


def _hgrn_diag(qi, ki, vi, bi):
    half = SUBLANES
    row = lax.broadcasted_iota(jnp.int32, (half, HEAD_DIM), 0)
    q_lo, q_hi = qi[:half], qi[half:]
    b_lo, b_hi = bi[:half], bi[half:]
    o_lo = jnp.zeros((half, HEAD_DIM), F32)
    o_hi = jnp.zeros((half, HEAD_DIM), F32)
    for s in range(SUB_CHUNK):
        ks, vs, bs = ki[s:s + 1], vi[s:s + 1], bi[s:s + 1]
        if s < half:
            dec = jnp.exp(jnp.where(row >= s, b_lo - bs, NEG_INF))
            o_lo = o_lo + jnp.sum(q_lo * ks * dec, axis=1, keepdims=True) * vs
            dec = jnp.exp(b_hi - bs)
        else:
            dec = jnp.exp(jnp.where(row >= s - half, b_hi - bs, NEG_INF))
        o_hi = o_hi + jnp.sum(q_hi * ks * dec, axis=1, keepdims=True) * vs
    return jnp.concatenate([o_lo, o_hi], axis=0)


def _hgrn_chunk(zq, zf, zi, zg, lb, gn, tri, st):
    C, cs = SCAN_CHUNK, SUB_CHUNK
    logf, kk, vv = _hgrn_gates(zf, zi, lb)
    b = _dot(tri, logf, HIGHEST)
    o_inter = _dot_nt((zq * jnp.exp(b)).astype(BF16), st.astype(BF16))
    parts = []
    for i in range(C // cs):
        r0 = i * cs
        qi, bi = zq[r0:r0 + cs], b[r0:r0 + cs]
        oi = o_inter[r0:r0 + cs] + _hgrn_diag(qi, kk[r0:r0 + cs], vv[r0:r0 + cs], bi)
        if i > 0:
            ref_b = b[r0 - 1:r0]
            qt = qi * jnp.exp(bi - ref_b)
            kt = kk[:r0] * jnp.exp(ref_b - b[:r0])
            att = _dot_nt(qt.astype(BF16), kt.astype(BF16))
            oi = oi + _dot(att.astype(BF16), vv[:r0].astype(BF16))
        parts.append(oi)
    o = jnp.concatenate(parts, axis=0)
    on = o * lax.rsqrt(jnp.mean(o * o, axis=1, keepdims=True) + EPS) * gn
    bl = b[C - 1:C]
    khat = kk * jnp.exp(bl - b)
    st_new = st * jnp.exp(bl) + _dot_tn(vv.astype(BF16), khat.astype(BF16))
    return on * _sigmoid(zg), st_new


def _hgrn_prompt_kernel(q_ref, f_ref, i_ref, g_ref, lbp_ref, gn_ref, o_ref, s_ref, st_ref, *, e, n_chunks):
    c = pl.program_id(2)
    C = SCAN_CHUNK
    hp = HGRN_HEADS_PER_STEP
    hd = HEAD_DIM

    @pl.when(c == 0)
    def _():
        st_ref[...] = jnp.zeros_like(st_ref)

    lb = _hgrn_lower_bound(lbp_ref[...], e)
    gn = gn_ref[...]
    r_i = lax.broadcasted_iota(jnp.int32, (C, C), 0)
    c_i = lax.broadcasted_iota(jnp.int32, (C, C), 1)
    tri = (c_i <= r_i).astype(F32)

    def chunk(ci, carry):
        rows = pl.ds(pl.multiple_of(ci * C, C), C)
        zq, zf, zi, zg = q_ref[rows, :], f_ref[rows, :], i_ref[rows, :], g_ref[rows, :]
        outs = []
        for hh in range(hp):
            ln = slice(hh * hd, (hh + 1) * hd)
            o, st_new = _hgrn_chunk(zq[:, ln], zf[:, ln], zi[:, ln], zg[:, ln], lb[:, ln], gn[:, ln], tri, st_ref[hh])
            st_ref[hh] = st_new
            outs.append(o)
        o_ref[rows, :] = jnp.concatenate(outs, axis=1).astype(o_ref.dtype)
        return carry

    lax.fori_loop(0, n_chunks, chunk, 0)

    @pl.when(c == pl.num_programs(2) - 1)
    def _():
        for hh in range(hp):
            s_ref[hh] = st_ref[hh].T


def _hgrn_prompt(z, lb_hgrn, gn, e, n_heads):
    bsz, t, _ = z.shape
    ct = min(t, 512)
    hp = HGRN_HEADS_PER_STEP
    assert t % ct == 0 and ct % SCAN_CHUNK == 0 and n_heads % hp == 0
    n_even = lb_hgrn.shape[0]
    hd = HEAD_DIM
    ng = n_heads // hp

    def col(off):
        return pl.BlockSpec((None, ct, hp * hd), lambda b, h, c, off=off: (b, c, off * ng + h))

    return pl.pallas_call(
        functools.partial(_hgrn_prompt_kernel, e=e, n_chunks=ct // SCAN_CHUNK),
        out_shape=(jax.ShapeDtypeStruct((bsz, t, n_heads * hd), BF16),
                   jax.ShapeDtypeStruct((bsz, n_heads, hd, hd), F32)),
        grid=(bsz, ng, t // ct),
        in_specs=[col(0), col(1), col(2), col(3),
                  pl.BlockSpec((n_even, hp * hd), lambda b, h, c: (0, h)),
                  pl.BlockSpec((None, 1, hp * hd), lambda b, h, c: (e, 0, h))],
        out_specs=(pl.BlockSpec((None, ct, hp * hd), lambda b, h, c: (b, c, h)),
                   pl.BlockSpec((None, hp, hd, hd), lambda b, h, c: (b, h, 0, 0))),
        scratch_shapes=[pltpu.VMEM((hp, hd, hd), F32)],
        compiler_params=_params("arbitrary", "arbitrary", "arbitrary"),
        name="hgrn_prompt",
    )(z, z, z, z, lb_hgrn, gn.reshape(n_even, 1, -1))


def _hgrn_sample_kernel(z_ref, s0_ref, lbp_ref, gn_ref, o_ref, s_ref, *, e, n_heads):
    hd = HEAD_DIM
    lb_all = _hgrn_lower_bound(lbp_ref[...], e)
    for h in range(n_heads):
        lanes = slice(h * hd, (h + 1) * hd)
        zq = z_ref[h:h + 1, :]
        zg = z_ref[3 * n_heads + h:3 * n_heads + h + 1, :]
        logf, kk, vv = _hgrn_gates(z_ref[n_heads + h:n_heads + h + 1, :],
                                   z_ref[2 * n_heads + h:2 * n_heads + h + 1, :], lb_all[:, lanes])
        s_new = _row_to_col(jnp.exp(logf), hd) * s0_ref[h] + _row_to_col(kk, hd) * vv
        s_ref[h] = s_new
        o = jnp.sum(_row_to_col(zq, hd) * s_new, axis=0, keepdims=True)
        on = o * lax.rsqrt(jnp.mean(o * o, axis=1, keepdims=True) + EPS) * gn_ref[:, lanes]
        o_ref[h:h + 1, :] = on * _sigmoid(zg)


def _hgrn_sample(z3, s0, lb_hgrn, gn, e, n_heads):
    db, groups, hd = z3.shape
    n_even = lb_hgrn.shape[0]
    w = n_heads * hd
    return pl.pallas_call(
        functools.partial(_hgrn_sample_kernel, e=e, n_heads=n_heads),
        out_shape=(jax.ShapeDtypeStruct((db, n_heads, hd), F32),
                   jax.ShapeDtypeStruct((db, n_heads, hd, hd), F32)),
        grid=(db,),
        in_specs=[pl.BlockSpec((None, groups, hd), lambda b: (b, 0, 0)),
                  pl.BlockSpec((None, None, n_heads, hd, hd), lambda b: (e, b, 0, 0, 0)),
                  pl.BlockSpec((n_even, w), lambda b: (0, 0)),
                  pl.BlockSpec((None, 1, w), lambda b: (e, 0, 0))],
        out_specs=(pl.BlockSpec((None, n_heads, hd), lambda b: (b, 0, 0)),
                   pl.BlockSpec((None, n_heads, hd, hd), lambda b: (b, 0, 0, 0))),
        compiler_params=_params("arbitrary"),
        name="hgrn_sample",
    )(z3, s0, lb_hgrn, gn.reshape(n_even, 1, -1))


def _moba_rope_tables(pos):
    half = ROT_DIM // 2
    inv = ROPE_THETA ** (-np.arange(half, dtype=np.float64) / half)
    ang = np.asarray(pos, np.float64)[:, None] * inv[None, :]
    cos, sin = np.cos(ang), np.sin(ang)
    t = ang.shape[0]
    c = np.ones((t, HEAD_DIM))
    s1 = np.zeros((t, HEAD_DIM))
    s2 = np.zeros((t, HEAD_DIM))
    c[:, :half] = cos
    c[:, half:2 * half] = cos
    s1[:, :half] = -sin
    s2[:, half:2 * half] = sin
    return tuple(jnp.asarray(a, F32) for a in (c, s1, s2))


def _ret_rope_tables(pos, dk):
    half = dk // 2
    inv = RET_THETA ** (-np.arange(half, dtype=np.float64) / half)
    ang = np.asarray(pos, np.float64)[:, None] * inv[None, :]
    return jnp.asarray(np.cos(ang), F32), jnp.asarray(np.sin(ang), F32)


def _moba_rope(x, c, s1, s2):
    n = x.shape[-1]
    half = ROT_DIM // 2
    return x * c + pltpu.roll(x, n - half, 1) * s1 + pltpu.roll(x, half, 1) * s2


def _ret_rope(x, c, s):
    half = x.shape[-1] // 2
    x1, x2 = x[:, :half], x[:, half:]
    return jnp.concatenate([x1 * c - x2 * s, x2 * c + x1 * s], axis=1)


def _moba_select(qr, km, n_past, topk):
    gate = _dot_nt(qr, km, HIGHEST)
    lane = lax.broadcasted_iota(jnp.int32, gate.shape, 1)
    rank = jnp.zeros(gate.shape, F32)
    for jp in range(n_past):
        gj = gate[:, jp:jp + 1]
        rank = rank + jnp.where(gj > gate, 1.0, jnp.where(gj == gate, jnp.where(lane > jp, 1.0, 0.0), 0.0))
    return jnp.where(rank < topk, 1.0, 0.0)


def _moba_prompt_kernel(q_ref, k_ref, v_ref, c_ref, s1_ref, s2_ref, o_ref, ko_ref, vo_ref,
                        kb_s, vb_s, km_s, *, nb, topk, scale):
    i = pl.program_id(2)
    blk = MOBA_BLOCK

    @pl.when(i == 0)
    def _():
        km_s[...] = jnp.zeros_like(km_s)
        for j in range(nb):
            rows = slice(j * blk, (j + 1) * blk)
            kr = _moba_rope(k_ref[rows, :], c_ref[rows, :], s1_ref[rows, :], s2_ref[rows, :])
            ko_ref[rows, :] = kr
            kb_s[rows, :] = kr.astype(BF16)
            km_s[j:j + 1, :] = jnp.mean(kr, axis=0, keepdims=True)
            v = v_ref[rows, :]
            vo_ref[rows, :] = v
            vb_s[rows, :] = v.astype(BF16)

    rows = pl.ds(pl.multiple_of(i * blk, blk), blk)
    qr = _moba_rope(q_ref[...], c_ref[rows, :], s1_ref[rows, :], s2_ref[rows, :])
    qb = qr.astype(BF16)
    r_i = lax.broadcasted_iota(jnp.int32, (blk, blk), 0)
    c_i = lax.broadcasted_iota(jnp.int32, (blk, blk), 1)

    for ii in range(nb):
        @pl.when(i == ii)
        def _(ii=ii):
            ncol = (ii + 1) * blk
            s = _dot_nt(qb, kb_s[0:ncol, :]) * scale
            pieces = []
            if ii > topk:
                sel = _moba_select(qr, km_s[...], ii, topk)
            for j in range(ii):
                sj = s[:, j * blk:(j + 1) * blk]
                if ii > topk:
                    sj = jnp.where(sel[:, j:j + 1] > 0.0, sj, NEG_INF)
                pieces.append(sj)
            pieces.append(jnp.where(c_i <= r_i, s[:, ii * blk:], NEG_INF))
            s = jnp.concatenate(pieces, axis=1)
            m = jnp.max(s, axis=1, keepdims=True)
            p = jnp.exp(s - m)
            l = jnp.sum(p, axis=1, keepdims=True)
            o_ref[...] = (_dot(p.astype(BF16), vb_s[0:ncol, :]) / l).astype(o_ref.dtype)


def _moba_prompt(z, tables, n_heads, q_off):
    bsz, t, _ = z.shape
    hd = HEAD_DIM
    blk = MOBA_BLOCK
    assert t % blk == 0
    nb = t // blk
    assert nb <= LANES
    topk = max(1, min(MOBA_TOPK, nb - 1))
    c, s1, s2 = tables
    full = pl.BlockSpec((t, hd), lambda b, h, i: (0, 0))

    def seq(off):
        return pl.BlockSpec((None, t, hd), lambda b, h, i, off=off: (b, 0, off + h))

    return pl.pallas_call(
        functools.partial(_moba_prompt_kernel, nb=nb, topk=topk, scale=hd ** -0.5),
        out_shape=(jax.ShapeDtypeStruct((bsz, t, n_heads * hd), BF16),
                   jax.ShapeDtypeStruct((bsz, t, n_heads * hd), F32),
                   jax.ShapeDtypeStruct((bsz, t, n_heads * hd), F32)),
        grid=(bsz, n_heads, nb),
        in_specs=[pl.BlockSpec((None, blk, hd), lambda b, h, i: (b, i, q_off + h)),
                  seq(q_off + n_heads), seq(q_off + 2 * n_heads), full, full, full],
        out_specs=(pl.BlockSpec((None, blk, hd), lambda b, h, i: (b, i, h)), seq(0), seq(0)),
        scratch_shapes=[pltpu.VMEM((t, hd), BF16), pltpu.VMEM((t, hd), BF16), pltpu.VMEM((LANES, hd), F32)],
        compiler_params=_params("arbitrary", "arbitrary", "arbitrary"),
        name="moba_prompt",
    )(z, z, z, c, s1, s2)


GATE_PAGES_PER_STEP = 8


def _moba_gate_kernel(pt_ref, z_ref, c_ref, s1_ref, s2_ref, *rest, n_heads, q_off, topk):
    pages = rest[:GATE_PAGES_PER_STEP]
    gate_ref, idx_ref = rest[GATE_PAGES_PER_STEP:]
    s = pl.program_id(1)
    bps = GATE_PAGES_PER_STEP // PAGES_PER_BLOCK

    @pl.when(s == 0)
    def _():
        gate_ref[...] = jnp.full(gate_ref.shape, NEG_INF, F32)

    q = _moba_rope(z_ref[q_off:q_off + n_heads, :], c_ref[...], s1_ref[...], s2_ref[...])
    lane = lax.broadcasted_iota(jnp.int32, gate_ref.shape, 1)
    for blk in range(bps):
        ksum = jnp.zeros_like(q)
        for p in range(PAGES_PER_BLOCK):
            ksum = ksum + jnp.sum(pages[blk * PAGES_PER_BLOCK + p][...], axis=0)
        val = jnp.sum(q * (ksum * (1.0 / MOBA_BLOCK)), axis=1, keepdims=True)
        gate_ref[...] = jnp.where(lane == s * bps + blk, val, gate_ref[...])

    @pl.when(s == pl.num_programs(1) - 1)
    def _():
        g = gate_ref[...]
        lane_f = lane.astype(F32)
        out = jnp.zeros(g.shape, F32)
        for r in range(topk):
            mx = jnp.max(g, axis=1, keepdims=True)
            am = jnp.min(jnp.where(g == mx, lane_f, float(LANES)), axis=1, keepdims=True)
            out = jnp.where(lane_f == float(r), am, out)
            g = jnp.where(lane_f == am, NEG_INF, g)
        idx_ref[...] = out.astype(jnp.int32)


def _moba_sample_gate(z3, cache_k, page_table, e, tables, n_heads, q_off):
    db, groups, hd = z3.shape
    n_pages = page_table.shape[1]
    n_blocks = n_pages // PAGES_PER_BLOCK
    assert n_pages % GATE_PAGES_PER_STEP == 0 and n_blocks <= LANES
    topk = min(MOBA_TOPK, n_blocks)
    steps = n_pages // GATE_PAGES_PER_STEP
    row = pl.BlockSpec((1, hd), lambda b, s, pt: (0, 0))

    def page(p):
        return pl.BlockSpec((None, None, PAGE_SIZE, n_heads, hd),
                            lambda b, s, pt, p=p: (e, pt[b * n_pages + s * GATE_PAGES_PER_STEP + p], 0, 0, 0))

    gate, idx = pl.pallas_call(
        functools.partial(_moba_gate_kernel, n_heads=n_heads, q_off=q_off, topk=topk),
        out_shape=(jax.ShapeDtypeStruct((db, n_heads, LANES), F32),
                   jax.ShapeDtypeStruct((db, n_heads, LANES), jnp.int32)),
        grid_spec=pltpu.PrefetchScalarGridSpec(
            num_scalar_prefetch=1,
            grid=(db, steps),
            in_specs=[pl.BlockSpec((None, groups, hd), lambda b, s, pt: (b, 0, 0)), row, row, row]
                     + [page(p) for p in range(GATE_PAGES_PER_STEP)],
            out_specs=(pl.BlockSpec((None, n_heads, LANES), lambda b, s, pt: (b, 0, 0)),
                       pl.BlockSpec((None, n_heads, LANES), lambda b, s, pt: (b, 0, 0)))),
        compiler_params=_params("arbitrary", "arbitrary"),
        name="moba_sample_gate",
    )(page_table.reshape(-1), z3, *tables, *([cache_k] * GATE_PAGES_PER_STEP))
    return idx[:, :, :topk]


def _moba_sample_attn_kernel(pg_ref, z_ref, c_ref, s1_ref, s2_ref, *rest, n_heads, n_sel, q_off, scale):
    k_pages, v_pages = rest[:n_sel], rest[n_sel:2 * n_sel]
    o_ref, kn_ref, vn_ref = rest[2 * n_sel:]
    h = pl.program_id(1)
    hd = HEAD_DIM
    rows = PAGE_SIZE * n_heads
    q = _moba_rope(z_ref[pl.ds(q_off + h, 1), :], c_ref[...], s1_ref[...], s2_ref[...])
    kn = _moba_rope(z_ref[pl.ds(q_off + n_heads + h, 1), :], c_ref[...], s1_ref[...], s2_ref[...])
    vn = z_ref[pl.ds(q_off + 2 * n_heads + h, 1), :]
    kn_ref[...] = kn
    vn_ref[...] = vn
    s_own = jnp.sum(q * kn, axis=1, keepdims=True) * scale

    q8 = jnp.broadcast_to(q, (SUBLANES, hd)).astype(BF16)
    col = lax.broadcasted_iota(jnp.int32, (1, rows), 1)
    mine = (col % n_heads) == h
    scores = []
    for kp in k_pages:
        sc = _dot_nt(q8, kp[...].reshape(rows, hd).astype(BF16))[0:1] * scale
        scores.append(jnp.where(mine, sc, NEG_INF))
    m = s_own
    for sc in scores:
        m = jnp.maximum(m, jnp.max(sc, axis=1, keepdims=True))
    l = jnp.exp(s_own - m)
    acc = l * vn
    for sc, vp in zip(scores, v_pages):
        p = jnp.exp(sc - m)
        l = l + jnp.sum(p, axis=1, keepdims=True)
        p8 = jnp.broadcast_to(p, (SUBLANES, rows)).astype(BF16)
        acc = acc + _dot(p8, vp[...].reshape(rows, hd).astype(BF16))[0:1]
    o_ref[...] = acc / l


def _moba_sample_attn(z3, cache_k, cache_v, pages, e, tables, n_heads, q_off):
    db, groups, hd = z3.shape
    n_sel = pages.shape[2]
    row = pl.BlockSpec((1, hd), lambda b, h, pg: (0, 0))

    def page_spec(s):
        return pl.BlockSpec((None, None, PAGE_SIZE, n_heads, hd),
                            lambda b, h, pg, s=s: (e, pg[(b * n_heads + h) * n_sel + s], 0, 0, 0))

    out = pl.BlockSpec((None, None, 1, hd), lambda b, h, pg: (b, h, 0, 0))
    shape = jax.ShapeDtypeStruct((db, n_heads, 1, hd), F32)
    return pl.pallas_call(
        functools.partial(_moba_sample_attn_kernel, n_heads=n_heads, n_sel=n_sel, q_off=q_off, scale=hd ** -0.5),
        out_shape=(shape, shape, shape),
        grid_spec=pltpu.PrefetchScalarGridSpec(
            num_scalar_prefetch=1,
            grid=(db, n_heads),
            in_specs=[pl.BlockSpec((None, groups, hd), lambda b, h, pg: (b, 0, 0)), row, row, row]
                     + [page_spec(s) for s in range(n_sel)] * 2,
            out_specs=(out, out, out)),
        compiler_params=_params("arbitrary", "arbitrary"),
        name="moba_sample_attn",
    )(pages.reshape(-1), z3, *tables, *([cache_k] * n_sel), *([cache_v] * n_sel))


RET_CHUNK = 256


def _ret_log_gamma(h, shape):
    hf = jnp.full(shape, h, jnp.int32).astype(F32)
    return jnp.log(1.0 - jnp.exp2(-5.0 - hf))


def _layer_norm_gate(o, gn, g):
    oc = o - jnp.mean(o, axis=1, keepdims=True)
    on = oc * lax.rsqrt(jnp.mean(oc * oc, axis=1, keepdims=True) + EPS) * gn
    return on * _silu(g)


def _ret_prompt_kernel(q_ref, k_ref, v_ref, g_ref, c_ref, s_ref, gn_ref, o_ref, so_ref, st_s, *, dk):
    h = pl.program_id(1)
    c = pl.program_id(2)
    C = RET_CHUNK

    @pl.when(c == 0)
    def _():
        st_s[...] = jnp.zeros_like(st_s)

    rows = pl.ds(pl.multiple_of(c * C, C), C)
    cos, sin = c_ref[rows, :], s_ref[rows, :]
    qr = _ret_rope(q_ref[...], cos, sin)
    kr = _ret_rope(k_ref[...], cos, sin) * (dk ** -0.5)
    v = v_ref[...]

    t_sq = lax.broadcasted_iota(jnp.int32, (C, C), 0)
    s_sq = lax.broadcasted_iota(jnp.int32, (C, C), 1)
    lg = _ret_log_gamma(h, (1, 1))
    dmat = jnp.exp(jnp.where(s_sq <= t_sq, (t_sq - s_sq).astype(F32) * lg, NEG_INF))
    t_w = lax.broadcasted_iota(jnp.int32, qr.shape, 0).astype(F32)
    inter = jnp.exp((t_w + 1.0) * lg)
    state_dec = jnp.exp((C - 1.0 - t_w) * lg)
    chunk_dec = jnp.exp(float(C) * lg)

    st = st_s[...]
    qb = qr.astype(BF16)
    att = _dot_nt(qb, kr.astype(BF16)) * dmat
    o = _dot(att.astype(BF16), v.astype(BF16)) + _dot(qb, st.astype(BF16)) * inter
    st_new = chunk_dec * st + _dot_tn((kr * state_dec).astype(BF16), v.astype(BF16))
    st_s[...] = st_new
    o_ref[...] = _layer_norm_gate(o, gn_ref[...], g_ref[...]).astype(o_ref.dtype)

    @pl.when(c == pl.num_programs(2) - 1)
    def _():
        so_ref[...] = st_new


def _ret_prompt(z, tables, gn, r, n_heads):
    bsz, t, width = z.shape
    dk = width // (4 * n_heads)
    C = RET_CHUNK
    assert t % C == 0
    cos, sin = tables
    full = pl.BlockSpec((t, dk // 2), lambda b, h, c: (0, 0))

    def col(off):
        return pl.BlockSpec((None, C, dk), lambda b, h, c, off=off: (b, c, off + h))

    return pl.pallas_call(
        functools.partial(_ret_prompt_kernel, dk=dk),
        out_shape=(jax.ShapeDtypeStruct((bsz, t, n_heads * dk), BF16),
                   jax.ShapeDtypeStruct((bsz, n_heads, dk, dk), F32)),
        grid=(bsz, n_heads, t // C),
        in_specs=[col(0), col(n_heads), col(2 * n_heads), col(3 * n_heads), full, full,
                  pl.BlockSpec((None, 1, dk), lambda b, h, c: (r, 0, h))],
        out_specs=(pl.BlockSpec((None, C, dk), lambda b, h, c: (b, c, h)),
                   pl.BlockSpec((None, None, dk, dk), lambda b, h, c: (b, h, 0, 0))),
        scratch_shapes=[pltpu.VMEM((dk, dk), F32)],
        compiler_params=_params("arbitrary", "arbitrary", "arbitrary"),
        name="retention_prompt",
    )(z, z, z, z, cos, sin, gn.reshape(gn.shape[0], 1, -1))


def _ret_sample_kernel(z_ref, s0_ref, c_ref, s_ref, gn_ref, o_ref, so_ref, *, n_heads, dk):
    h = pl.program_id(1)
    cos, sin = c_ref[...], s_ref[...]
    qr = _ret_rope(z_ref[pl.ds(h, 1), :], cos, sin)
    kr = _ret_rope(z_ref[pl.ds(n_heads + h, 1), :], cos, sin) * (dk ** -0.5)
    v = z_ref[pl.ds(2 * n_heads + h, 1), :]
    g = z_ref[pl.ds(3 * n_heads + h, 1), :]
    gamma = jnp.exp(_ret_log_gamma(h, (1, 1)))
    s0 = s0_ref[...]
    qs = jnp.sum(_row_to_col(qr, dk) * s0, axis=0, keepdims=True)
    o = jnp.sum(qr * kr, axis=1, keepdims=True) * v + qs * gamma
    so_ref[...] = gamma * s0 + _row_to_col(kr, dk) * v
    o_ref[...] = _layer_norm_gate(o, gn_ref[pl.ds(h, 1), :], g)


def _ret_sample(z4, s0, tables, gn, r, n_heads):
    db, groups, dk = z4.shape
    cos, sin = tables
    row = pl.BlockSpec((1, dk // 2), lambda b, h: (0, 0))
    return pl.pallas_call(
        functools.partial(_ret_sample_kernel, n_heads=n_heads, dk=dk),
        out_shape=(jax.ShapeDtypeStruct((db, n_heads, 1, dk), F32),
                   jax.ShapeDtypeStruct((db, n_heads, dk, dk), F32)),
        grid=(db, n_heads),
        in_specs=[pl.BlockSpec((None, groups, dk), lambda b, h: (b, 0, 0)),
                  pl.BlockSpec((None, None, None, dk, dk), lambda b, h: (r, b, h, 0, 0)), row, row,
                  pl.BlockSpec((None, n_heads, dk), lambda b, h: (r, 0, 0))],
        out_specs=(pl.BlockSpec((None, None, 1, dk), lambda b, h: (b, h, 0, 0)),
                   pl.BlockSpec((None, None, dk, dk), lambda b, h: (b, h, 0, 0))),
        compiler_params=_params("arbitrary", "arbitrary"),
        name="retention_sample",
    )(z4, s0, cos, sin, gn)


def _prompt_trunk(x_prompt, norm_mix, norm_ffn, norm_final, w_in_even, w_out_even, gn_hgrn, lb_hgrn,
                  w_in_odd, w_out_odd, gn_ret, w_ffn_in, conv_w, conv_b, w_ffn_out):
    bsz, t, d = x_prompt.shape
    depth = norm_mix.shape[0]
    n_a = gn_hgrn.shape[1] // HEAD_DIM
    n_b = (w_in_even.shape[2] - 4 * gn_hgrn.shape[1]) // (3 * HEAD_DIM)
    pos = np.arange(t)
    moba_tables = _moba_rope_tables(pos)
    ret_tables = _ret_rope_tables(pos, w_in_odd.shape[2] // (4 * H_C))
    x = x_prompt.reshape(bsz * t, d)
    new_k, new_v, new_h, new_r, new_c = [], [], [], [], []
    for l in range(depth):
        h = _rmsnorm(x, norm_mix, l, BF16)
        if l % 2 == 0:
            e = l // 2
            z = _mm(h, w_in_even, e).reshape(bsz, t, -1)
            oa, s_h = _hgrn_prompt(z, lb_hgrn, gn_hgrn, e, n_a)
            ob, k_rot, v_rows = _moba_prompt(z, moba_tables, n_b, 4 * n_a)
            mix = jnp.concatenate([oa, ob], axis=-1).reshape(bsz * t, -1)
            x = _mm(mix, w_out_even, e, res=x)
            new_h.append(s_h)
            new_k.append(k_rot.reshape(bsz, t, n_b, HEAD_DIM))
            new_v.append(v_rows.reshape(bsz, t, n_b, HEAD_DIM))
        else:
            r = l // 2
            z = _mm(h, w_in_odd, r).reshape(bsz, t, -1)
            o, s_r = _ret_prompt(z, ret_tables, gn_ret, r, H_C)
            x = _mm(o.reshape(bsz * t, -1), w_out_odd, r, res=x)
            new_r.append(s_r)
        h = _rmsnorm(x, norm_ffn, l, BF16)
        act, buf = _ffn_in_prompt(h, w_ffn_in, conv_w, conv_b, l, bsz, t)
        x = _mm(act, w_ffn_out, l, res=x)
        new_c.append(buf)
    y = _rmsnorm(x, norm_final.reshape(1, d), 0, F32).reshape(bsz, t, d)
    return (y, jnp.stack(new_k), jnp.stack(new_v), jnp.stack(new_h), jnp.stack(new_r), jnp.stack(new_c))


def _sample_trunk(x_sample, cache_k, cache_v, page_table, state_hgrn, state_ret, state_conv,
                  norm_mix, norm_ffn, norm_final, w_in_even, w_out_even, gn_hgrn, lb_hgrn,
                  w_in_odd, w_out_odd, gn_ret, w_ffn_in, conv_w, conv_b, w_ffn_out):
    db, ts, d = x_sample.shape
    assert ts == 1
    depth = norm_mix.shape[0]
    n_a = gn_hgrn.shape[1] // HEAD_DIM
    n_b = (w_in_even.shape[2] - 4 * gn_hgrn.shape[1]) // (3 * HEAD_DIM)
    assert n_b == SUBLANES
    n_pages = page_table.shape[1]
    assert n_pages % PAGES_PER_BLOCK == 0
    pos = np.array([n_pages * PAGE_SIZE])
    moba_tables = _moba_rope_tables(pos)
    dk_c = w_in_odd.shape[2] // (4 * H_C)
    ret_tables = _ret_rope_tables(pos, dk_c)
    gn_ret3 = gn_ret.reshape(gn_ret.shape[0], H_C, dk_c)
    conv_buf = jnp.swapaxes(state_conv, 1, 2)
    x = x_sample.reshape(db, d)
    new_k, new_v, new_h, new_r, new_c = [], [], [], [], []
    for l in range(depth):
        h = _rmsnorm(x, norm_mix, l, BF16)
        if l % 2 == 0:
            e = l // 2
            z3 = _mm(h, w_in_even, e).reshape(db, -1, HEAD_DIM)
            oa, s_h = _hgrn_sample(z3, state_hgrn, lb_hgrn, gn_hgrn, e, n_a)
            idx = _moba_sample_gate(z3, cache_k, page_table, e, moba_tables, n_b, 4 * n_a)
            logical = idx[..., None] * PAGES_PER_BLOCK + jnp.arange(PAGES_PER_BLOCK, dtype=jnp.int32)
            pages = jnp.take_along_axis(page_table[:, None, :], logical.reshape(db, n_b, -1), axis=2)
            ob, k_rot, v_rows = _moba_sample_attn(z3, cache_k, cache_v, pages, e, moba_tables, n_b, 4 * n_a)
            mix = jnp.concatenate([oa.reshape(db, -1), ob.reshape(db, -1)], axis=-1).astype(BF16)
            x = _mm(mix, w_out_even, e, res=x)
            new_h.append(s_h)
            new_k.append(k_rot.reshape(db, ts, n_b, HEAD_DIM))
            new_v.append(v_rows.reshape(db, ts, n_b, HEAD_DIM))
        else:
            r = l // 2
            z = _mm(h, w_in_odd, r)
            o, s_r = _ret_sample(z.reshape(db, 4 * H_C, dk_c), state_ret, ret_tables, gn_ret3, r, H_C)
            x = _mm(o.reshape(db, -1).astype(BF16), w_out_odd, r, res=x)
            new_r.append(s_r)
        h = _rmsnorm(x, norm_ffn, l, BF16)
        up = _mm(h, w_ffn_in, l)
        act, buf_t = _conv_gate_sample(up, conv_buf, conv_w, conv_b, l)
        x = _mm(act, w_ffn_out, l, res=x)
        new_c.append(jnp.swapaxes(buf_t, 0, 1))
    y = _rmsnorm(x, norm_final.reshape(1, d), 0, F32).reshape(db, ts, d)
    return (y, jnp.stack(new_k), jnp.stack(new_v), jnp.stack(new_h), jnp.stack(new_r), jnp.stack(new_c))


def kernel(x_prompt, x_sample, cache_k, cache_v, page_table, state_hgrn, state_ret, state_conv,
           norm_mix, norm_ffn, norm_final, w_in_even, w_out_even, gn_hgrn, lb_hgrn,
           w_in_odd, w_out_odd, gn_ret, w_ffn_in, conv_w, conv_b, w_ffn_out):
    weights = (norm_mix, norm_ffn, norm_final, w_in_even, w_out_even, gn_hgrn, lb_hgrn,
               w_in_odd, w_out_odd, gn_ret, w_ffn_in, conv_w, conv_b, w_ffn_out)
    y_p, k_p, v_p, h_p, r_p, c_p = _prompt_trunk(x_prompt, *weights)
    y_s, k_s, v_s, h_s, r_s, c_s = _sample_trunk(x_sample, cache_k, cache_v, page_table,
                                                 state_hgrn, state_ret, state_conv, *weights)
    return (y_p, y_s, k_p, v_p, k_s, v_s, h_p, h_s, r_p, r_s, c_p, c_s)
```

```python
import functools

import numpy as np
import jax
import jax.numpy as jnp
from jax import lax
from jax.experimental import pallas as pl
from jax.experimental.pallas import tpu as pltpu

F32 = jnp.float32
BF16 = jnp.bfloat16

HEAD_DIM = 128
MOBA_BLOCK = 256
MOBA_TOPK = 3
PAGE_SIZE = 128
PAGES_PER_BLOCK = MOBA_BLOCK // PAGE_SIZE
ROT_DIM = HEAD_DIM // 4
ROPE_THETA = 500000.0
RET_THETA = 10000.0
H_C = 8
SCAN_CHUNK = 64
SUB_CHUNK = 16
SUBLANES = 8
CONV_W = 3
EPS = 1e-6
LANES = 128
VMEM_LIMIT = 56 * 1024 * 1024
NEG_INF = float("-inf")
HIGHEST = lax.Precision.HIGHEST


def _params(*sem):
    return pltpu.CompilerParams(dimension_semantics=sem, vmem_limit_bytes=VMEM_LIMIT)


def _dot(a, b, precision=None):
    return jnp.dot(a, b, preferred_element_type=F32, precision=precision)


def _dot_nt(a, b, precision=None):
    return lax.dot_general(a, b, (((1,), (1,)), ((), ())), preferred_element_type=F32, precision=precision)


def _dot_tn(a, b, precision=None):
    return lax.dot_general(a, b, (((0,), (0,)), ((), ())), preferred_element_type=F32, precision=precision)


def _sigmoid(x):
    return 1.0 / (1.0 + jnp.exp(-x))


def _silu(x):
    return x * _sigmoid(x)


def _log_sigmoid(x):
    return jnp.minimum(x, 0.0) - jnp.log1p(jnp.exp(-jnp.abs(x)))


def _logaddexp(a, b):
    return jnp.maximum(a, b) + jnp.log1p(jnp.exp(-jnp.abs(a - b)))


def _row_to_col(row, n):
    r = lax.broadcasted_iota(jnp.int32, (n, n), 0)
    c = lax.broadcasted_iota(jnp.int32, (n, n), 1)
    return jnp.sum(jnp.where(r == c, jnp.broadcast_to(row, (n, n)), 0.0), axis=1, keepdims=True)


def _rmsnorm_kernel(x_ref, g_ref, o_ref):
    x = x_ref[...]
    y = x * lax.rsqrt(jnp.mean(x * x, axis=-1, keepdims=True) + EPS)
    o_ref[...] = (y * g_ref[...]).astype(o_ref.dtype)


def _rmsnorm(x, g, l, out_dtype):
    m, d = x.shape
    tm = min(m, 512)
    return pl.pallas_call(
        _rmsnorm_kernel,
        out_shape=jax.ShapeDtypeStruct((m, d), out_dtype),
        grid=(m // tm,),
        in_specs=[pl.BlockSpec((tm, d), lambda i: (i, 0)), pl.BlockSpec((None, 1, d), lambda i: (l, 0, 0))],
        out_specs=pl.BlockSpec((tm, d), lambda i: (i, 0)),
        compiler_params=_params("arbitrary"),
        name="rmsnorm",
    )(x, g.reshape(-1, 1, d))


def _split_dot(a_refs, wb_ref):
    kp = wb_ref.shape[0] // len(a_refs)
    acc = None
    for p, a_ref in enumerate(a_refs):
        part = _dot(a_ref[...], wb_ref[p * kp:(p + 1) * kp, :])
        acc = part if acc is None else acc + part
    return acc


def _mm_kernel(*refs, n_parts, has_res):
    a_refs, refs = refs[:n_parts], refs[n_parts:]
    as_refs, refs = refs[:n_parts], refs[n_parts:]
    w_ref, refs = refs[0], refs[1:]
    if has_res:
        r_ref, rs_ref, o_ref, os_ref, wb_ref = refs
    else:
        o_ref, os_ref, wb_ref = refs

    @pl.when(pl.program_id(1) == 0)
    def _():
        wb_ref[...] = w_ref[...].astype(BF16)
        acc_s = _split_dot(as_refs, wb_ref)
        if has_res:
            acc_s = rs_ref[...] + acc_s
        os_ref[...] = acc_s

    acc = _split_dot(a_refs, wb_ref)
    if has_res:
        acc = r_ref[...] + acc
    o_ref[...] = acc


def _mm(a_parts, as_parts, w, l, res=None, res_s=None):
    n_parts = len(a_parts)
    m, kp = a_parts[0].shape
    ms = as_parts[0].shape[0]
    k, n = w.shape[1], w.shape[2]
    assert kp * n_parts == k
    tm = min(m, 1024 if k <= 2048 else 512)
    tn = min(n, 1024 if k <= 2048 else 512)
    assert m % tm == 0 and n % tn == 0
    in_specs = ([pl.BlockSpec((tm, kp), lambda j, i: (i, 0))] * n_parts
                + [pl.BlockSpec((ms, kp), lambda j, i: (0, 0))] * n_parts
                + [pl.BlockSpec((None, k, tn), lambda j, i: (l, 0, j))])
    args = [*a_parts, *as_parts, w]
    if res is not None:
        in_specs += [pl.BlockSpec((tm, tn), lambda j, i: (i, j)), pl.BlockSpec((ms, tn), lambda j, i: (0, j))]
        args += [res, res_s]
    return pl.pallas_call(
        functools.partial(_mm_kernel, n_parts=n_parts, has_res=res is not None),
        out_shape=(jax.ShapeDtypeStruct((m, n), F32), jax.ShapeDtypeStruct((ms, n), F32)),
        grid=(n // tn, m // tm),
        in_specs=in_specs,
        out_specs=(pl.BlockSpec((tm, tn), lambda j, i: (i, j)), pl.BlockSpec((ms, tn), lambda j, i: (0, j))),
        scratch_shapes=[pltpu.VMEM((k, tn), BF16)],
        compiler_params=_params("arbitrary", "arbitrary"),
        name="matmul",
    )(*args)


def _conv_gate(a, g, p0, p1, cw_ref, cb_ref):
    row = lax.broadcasted_iota(jnp.int32, a.shape, 0)
    a1 = jnp.where(row == 0, p1, pltpu.roll(a, 1, 0))
    a2 = jnp.where(row == 0, p0, jnp.where(row == 1, p1, pltpu.roll(a, 2, 0)))
    conv = cb_ref[...] + cw_ref[0:1, :] * a2
    conv = conv + cw_ref[1:2, :] * a1
    conv = conv + cw_ref[2:3, :] * a
    return _silu(conv) * g


def _ffn_in_kernel(h_ref, hs_ref, bufs_ref, wa_ref, wb_ref, cw_ref, cb_ref, o_ref, nb_ref, os_ref, nbs_ref,
                   wa_s, wb_s, carry_s, *, tiles_per_seq):
    i = pl.program_id(1)

    @pl.when(i == 0)
    def _():
        wa_s[...] = wa_ref[...].astype(BF16)
        wb_s[...] = wb_ref[...].astype(BF16)
        hs = hs_ref[...]
        a_s = _dot(hs, wa_s[...])
        conv = cb_ref[...] + cw_ref[0:1, :] * bufs_ref[0]
        conv = conv + cw_ref[1:2, :] * bufs_ref[1]
        conv = conv + cw_ref[2:3, :] * a_s
        os_ref[...] = (_silu(conv) * _dot(hs, wb_s[...])).astype(os_ref.dtype)
        nbs_ref[0] = bufs_ref[1]
        nbs_ref[1] = a_s

    h = h_ref[...]
    a = _dot(h, wa_s[...])
    g = _dot(h, wb_s[...])
    tm = a.shape[0]
    seq_start = (i % tiles_per_seq) == 0
    p0 = jnp.where(seq_start, 0.0, carry_s[SUBLANES - 2:SUBLANES - 1, :])
    p1 = jnp.where(seq_start, 0.0, carry_s[SUBLANES - 1:SUBLANES, :])
    o_ref[...] = _conv_gate(a, g, p0, p1, cw_ref, cb_ref).astype(o_ref.dtype)
    carry_s[...] = a[tm - SUBLANES:tm, :]

    @pl.when((i % tiles_per_seq) == tiles_per_seq - 1)
    def _():
        nb_ref[...] = a[tm - (CONV_W - 1):tm, :]


def _ffn_in(h, h_s, buf_s, w, cw, cb, l, bsz, t):
    m, k = h.shape
    db = h_s.shape[0]
    f = w.shape[2] // 2
    tm = min(t, 1024)
    tn = 512
    assert t % tm == 0 and f % tn == 0 and tm >= SUBLANES
    nf = f // tn
    tiles_per_seq = t // tm
    return pl.pallas_call(
        functools.partial(_ffn_in_kernel, tiles_per_seq=tiles_per_seq),
        out_shape=(jax.ShapeDtypeStruct((m, f), BF16),
                   jax.ShapeDtypeStruct((bsz, CONV_W - 1, f), F32),
                   jax.ShapeDtypeStruct((db, f), BF16),
                   jax.ShapeDtypeStruct((CONV_W - 1, db, f), F32)),
        grid=(nf, m // tm),
        in_specs=[pl.BlockSpec((tm, k), lambda j, i: (i, 0)),
                  pl.BlockSpec((db, k), lambda j, i: (0, 0)),
                  pl.BlockSpec((None, CONV_W - 1, db, tn), lambda j, i: (l, 0, 0, j)),
                  pl.BlockSpec((None, k, tn), lambda j, i: (l, 0, j)),
                  pl.BlockSpec((None, k, tn), lambda j, i: (l, 0, nf + j)),
                  pl.BlockSpec((None, CONV_W, tn), lambda j, i: (l, 0, j)),
                  pl.BlockSpec((None, 1, tn), lambda j, i: (l, 0, j))],
        out_specs=(pl.BlockSpec((tm, tn), lambda j, i: (i, j)),
                   pl.BlockSpec((None, CONV_W - 1, tn), lambda j, i: (i // tiles_per_seq, 0, j)),
                   pl.BlockSpec((db, tn), lambda j, i: (0, j)),
                   pl.BlockSpec((CONV_W - 1, db, tn), lambda j, i: (0, 0, j))),
        scratch_shapes=[pltpu.VMEM((k, tn), BF16), pltpu.VMEM((k, tn), BF16), pltpu.VMEM((SUBLANES, tn), F32)],
        compiler_params=_params("arbitrary", "arbitrary"),
        name="ffn_in_conv",
    )(h, h_s, buf_s, w, w, cw, cb.reshape(-1, 1, f))


HGRN_HEADS_PER_STEP = 4


def _hgrn_lower_bound(lbp, e):
    mx = jnp.max(lbp, axis=0, keepdims=True)
    ex = jnp.exp(lbp - mx)
    sm = ex / jnp.sum(ex, axis=0, keepdims=True)
    lb = jnp.zeros_like(sm[0:1])
    for l in range(1, e + 1):
        lb = lb + sm[l:l + 1]
    return lb


def _hgrn_gates(zf, zi, lb):
    logf = _logaddexp(jnp.log(lb), jnp.log1p(-lb) + _log_sigmoid(zf))
    return logf, 1.0 - jnp.exp(logf), _silu(zi)


def _hgrn_diag(qi, ki, vi, bi):
    half = SUBLANES
    row = lax.broadcasted_iota(jnp.int32, (half, HEAD_DIM), 0)
    q_lo, q_hi = qi[:half], qi[half:]
    b_lo, b_hi = bi[:half], bi[half:]
    o_lo = jnp.zeros((half, HEAD_DIM), F32)
    o_hi = jnp.zeros((half, HEAD_DIM), F32)
    for s in range(SUB_CHUNK):
        ks, vs, bs = ki[s:s + 1], vi[s:s + 1], bi[s:s + 1]
        if s < half:
            dec = jnp.exp(jnp.where(row >= s, b_lo - bs, NEG_INF))
            o_lo = o_lo + jnp.sum(q_lo * ks * dec, axis=1, keepdims=True) * vs
            dec = jnp.exp(b_hi - bs)
        else:
            dec = jnp.exp(jnp.where(row >= s - half, b_hi - bs, NEG_INF))
        o_hi = o_hi + jnp.sum(q_hi * ks * dec, axis=1, keepdims=True) * vs
    return jnp.concatenate([o_lo, o_hi], axis=0)


def _hgrn_chunk(zq, zf, zi, zg, lb, gn, tri, st):
    C, cs = SCAN_CHUNK, SUB_CHUNK
    logf, kk, vv = _hgrn_gates(zf, zi, lb)
    b = _dot(tri, logf, HIGHEST)
    o_inter = _dot_nt((zq * jnp.exp(b)).astype(BF16), st.astype(BF16))
    parts = []
    for i in range(C // cs):
        r0 = i * cs
        qi, bi = zq[r0:r0 + cs], b[r0:r0 + cs]
        oi = o_inter[r0:r0 + cs] + _hgrn_diag(qi, kk[r0:r0 + cs], vv[r0:r0 + cs], bi)
        if i > 0:
            ref_b = b[r0 - 1:r0]
            qt = qi * jnp.exp(bi - ref_b)
            kt = kk[:r0] * jnp.exp(ref_b - b[:r0])
            att = _dot_nt(qt.astype(BF16), kt.astype(BF16))
            oi = oi + _dot(att.astype(BF16), vv[:r0].astype(BF16))
        parts.append(oi)
    o = jnp.concatenate(parts, axis=0)
    on = o * lax.rsqrt(jnp.mean(o * o, axis=1, keepdims=True) + EPS) * gn
    bl = b[C - 1:C]
    khat = kk * jnp.exp(bl - b)
    st_new = st * jnp.exp(bl) + _dot_tn(vv.astype(BF16), khat.astype(BF16))
    return on * _sigmoid(zg), st_new


def _hgrn_prompt_kernel(q_ref, f_ref, i_ref, g_ref, lbp_ref, gn_ref, o_ref, s_ref, st_ref, *, e, n_chunks):
    c = pl.program_id(2)
    C = SCAN_CHUNK
    hp = HGRN_HEADS_PER_STEP
    hd = HEAD_DIM

    @pl.when(c == 0)
    def _():
        st_ref[...] = jnp.zeros_like(st_ref)

    lb = _hgrn_lower_bound(lbp_ref[...], e)
    gn = gn_ref[...]
    r_i = lax.broadcasted_iota(jnp.int32, (C, C), 0)
    c_i = lax.broadcasted_iota(jnp.int32, (C, C), 1)
    tri = (c_i <= r_i).astype(F32)

    def chunk(ci, carry):
        rows = pl.ds(pl.multiple_of(ci * C, C), C)
        zq, zf, zi, zg = q_ref[rows, :], f_ref[rows, :], i_ref[rows, :], g_ref[rows, :]
        outs = []
        for hh in range(hp):
            ln = slice(hh * hd, (hh + 1) * hd)
            o, st_new = _hgrn_chunk(zq[:, ln], zf[:, ln], zi[:, ln], zg[:, ln], lb[:, ln], gn[:, ln], tri, st_ref[hh])
            st_ref[hh] = st_new
            outs.append(o)
        o_ref[rows, :] = jnp.concatenate(outs, axis=1).astype(o_ref.dtype)
        return carry

    lax.fori_loop(0, n_chunks, chunk, 0)

    @pl.when(c == pl.num_programs(2) - 1)
    def _():
        for hh in range(hp):
            s_ref[hh] = st_ref[hh].T


def _hgrn_prompt(z, lb_hgrn, gn, e, n_heads):
    bsz, t, _ = z.shape
    ct = min(t, 512)
    hp = HGRN_HEADS_PER_STEP
    assert t % ct == 0 and ct % SCAN_CHUNK == 0 and n_heads % hp == 0
    n_even = lb_hgrn.shape[0]
    hd = HEAD_DIM
    ng = n_heads // hp

    def col(off):
        return pl.BlockSpec((None, ct, hp * hd), lambda b, h, c, off=off: (b, c, off * ng + h))

    return pl.pallas_call(
        functools.partial(_hgrn_prompt_kernel, e=e, n_chunks=ct // SCAN_CHUNK),
        out_shape=(jax.ShapeDtypeStruct((bsz, t, n_heads * hd), BF16),
                   jax.ShapeDtypeStruct((bsz, n_heads, hd, hd), F32)),
        grid=(bsz, ng, t // ct),
        in_specs=[col(0), col(1), col(2), col(3),
                  pl.BlockSpec((n_even, hp * hd), lambda b, h, c: (0, h)),
                  pl.BlockSpec((None, 1, hp * hd), lambda b, h, c: (e, 0, h))],
        out_specs=(pl.BlockSpec((None, ct, hp * hd), lambda b, h, c: (b, c, h)),
                   pl.BlockSpec((None, hp, hd, hd), lambda b, h, c: (b, h, 0, 0))),
        scratch_shapes=[pltpu.VMEM((hp, hd, hd), F32)],
        compiler_params=_params("arbitrary", "arbitrary", "arbitrary"),
        name="hgrn_prompt",
    )(z, z, z, z, lb_hgrn, gn.reshape(n_even, 1, -1))


def _hgrn_sample_kernel(z_ref, s0_ref, lbp_ref, gn_ref, o_ref, s_ref, *, e, n_heads):
    hd = HEAD_DIM
    lb_all = _hgrn_lower_bound(lbp_ref[...], e)
    for h in range(n_heads):
        lanes = slice(h * hd, (h + 1) * hd)
        zq = z_ref[h:h + 1, :]
        zg = z_ref[3 * n_heads + h:3 * n_heads + h + 1, :]
        logf, kk, vv = _hgrn_gates(z_ref[n_heads + h:n_heads + h + 1, :],
                                   z_ref[2 * n_heads + h:2 * n_heads + h + 1, :], lb_all[:, lanes])
        s_new = _row_to_col(jnp.exp(logf), hd) * s0_ref[h] + _row_to_col(kk, hd) * vv
        s_ref[h] = s_new
        o = jnp.sum(_row_to_col(zq, hd) * s_new, axis=0, keepdims=True)
        on = o * lax.rsqrt(jnp.mean(o * o, axis=1, keepdims=True) + EPS) * gn_ref[:, lanes]
        o_ref[h:h + 1, :] = on * _sigmoid(zg)


def _hgrn_sample(z3, s0, lb_hgrn, gn, e, n_heads):
    db, groups, hd = z3.shape
    n_even = lb_hgrn.shape[0]
    w = n_heads * hd
    return pl.pallas_call(
        functools.partial(_hgrn_sample_kernel, e=e, n_heads=n_heads),
        out_shape=(jax.ShapeDtypeStruct((db, n_heads, hd), F32),
                   jax.ShapeDtypeStruct((db, n_heads, hd, hd), F32)),
        grid=(db,),
        in_specs=[pl.BlockSpec((None, groups, hd), lambda b: (b, 0, 0)),
                  pl.BlockSpec((None, None, n_heads, hd, hd), lambda b: (e, b, 0, 0, 0)),
                  pl.BlockSpec((n_even, w), lambda b: (0, 0)),
                  pl.BlockSpec((None, 1, w), lambda b: (e, 0, 0))],
        out_specs=(pl.BlockSpec((None, n_heads, hd), lambda b: (b, 0, 0)),
                   pl.BlockSpec((None, n_heads, hd, hd), lambda b: (b, 0, 0, 0))),
        compiler_params=_params("arbitrary"),
        name="hgrn_sample",
    )(z3, s0, lb_hgrn, gn.reshape(n_even, 1, -1))


def _moba_rope_tables(pos):
    half = ROT_DIM // 2
    inv = ROPE_THETA ** (-np.arange(half, dtype=np.float64) / half)
    ang = np.asarray(pos, np.float64)[:, None] * inv[None, :]
    cos, sin = np.cos(ang), np.sin(ang)
    t = ang.shape[0]
    c = np.ones((t, HEAD_DIM))
    s1 = np.zeros((t, HEAD_DIM))
    s2 = np.zeros((t, HEAD_DIM))
    c[:, :half] = cos
    c[:, half:2 * half] = cos
    s1[:, :half] = -sin
    s2[:, half:2 * half] = sin
    return tuple(jnp.asarray(a, F32) for a in (c, s1, s2))


def _ret_rope_tables(pos, dk):
    half = dk // 2
    inv = RET_THETA ** (-np.arange(half, dtype=np.float64) / half)
    ang = np.asarray(pos, np.float64)[:, None] * inv[None, :]
    return jnp.asarray(np.cos(ang), F32), jnp.asarray(np.sin(ang), F32)


def _moba_rope(x, c, s1, s2):
    n = x.shape[-1]
    half = ROT_DIM // 2
    return x * c + pltpu.roll(x, n - half, 1) * s1 + pltpu.roll(x, half, 1) * s2


def _ret_rope(x, c, s):
    half = x.shape[-1] // 2
    x1, x2 = x[:, :half], x[:, half:]
    return jnp.concatenate([x1 * c - x2 * s, x2 * c + x1 * s], axis=1)


def _moba_select(qr, km, n_past, topk):
    gate = _dot_nt(qr, km, HIGHEST)
    lane = lax.broadcasted_iota(jnp.int32, gate.shape, 1)
    rank = jnp.zeros(gate.shape, F32)
    for jp in range(n_past):
        gj = gate[:, jp:jp + 1]
        rank = rank + jnp.where(gj > gate, 1.0, jnp.where(gj == gate, jnp.where(lane > jp, 1.0, 0.0), 0.0))
    return jnp.where(rank < topk, 1.0, 0.0)


def _moba_prompt_kernel(q_ref, k_ref, v_ref, c_ref, s1_ref, s2_ref, o_ref, ko_ref, vo_ref,
                        kb_s, vb_s, km_s, *, nb, topk, scale):
    i = pl.program_id(2)
    blk = MOBA_BLOCK

    @pl.when(i == 0)
    def _():
        km_s[...] = jnp.zeros_like(km_s)
        for j in range(nb):
            rows = slice(j * blk, (j + 1) * blk)
            kr = _moba_rope(k_ref[rows, :], c_ref[rows, :], s1_ref[rows, :], s2_ref[rows, :])
            ko_ref[rows, :] = kr
            kb_s[rows, :] = kr.astype(BF16)
            km_s[j:j + 1, :] = jnp.mean(kr, axis=0, keepdims=True)
            v = v_ref[rows, :]
            vo_ref[rows, :] = v
            vb_s[rows, :] = v.astype(BF16)

    rows = pl.ds(pl.multiple_of(i * blk, blk), blk)
    qr = _moba_rope(q_ref[...], c_ref[rows, :], s1_ref[rows, :], s2_ref[rows, :])
    qb = (qr * scale).astype(BF16)
    r_i = lax.broadcasted_iota(jnp.int32, (blk, blk), 0)
    c_i = lax.broadcasted_iota(jnp.int32, (blk, blk), 1)

    for ii in range(nb):
        @pl.when(i == ii)
        def _(ii=ii):
            ncol = (ii + 1) * blk
            s = _dot_nt(qb, kb_s[0:ncol, :])
            pieces = []
            if ii > topk:
                sel = _moba_select(qr, km_s[...], ii, topk)
            for j in range(ii):
                sj = s[:, j * blk:(j + 1) * blk]
                if ii > topk:
                    sj = jnp.where(sel[:, j:j + 1] > 0.0, sj, NEG_INF)
                pieces.append(sj)
            pieces.append(jnp.where(c_i <= r_i, s[:, ii * blk:], NEG_INF))
            s = jnp.concatenate(pieces, axis=1)
            m = jnp.max(s, axis=1, keepdims=True)
            p = jnp.exp(s - m)
            l = jnp.sum(p, axis=1, keepdims=True)
            o_ref[...] = (_dot(p.astype(BF16), vb_s[0:ncol, :]) / l).astype(o_ref.dtype)


def _moba_prompt(z, tables, n_heads, q_off):
    bsz, t, _ = z.shape
    hd = HEAD_DIM
    blk = MOBA_BLOCK
    assert t % blk == 0
    nb = t // blk
    assert nb <= LANES
    topk = max(1, min(MOBA_TOPK, nb - 1))
    c, s1, s2 = tables
    full = pl.BlockSpec((t, hd), lambda b, h, i: (0, 0))

    def seq(off):
        return pl.BlockSpec((None, t, hd), lambda b, h, i, off=off: (b, 0, off + h))

    return pl.pallas_call(
        functools.partial(_moba_prompt_kernel, nb=nb, topk=topk, scale=hd ** -0.5),
        out_shape=(jax.ShapeDtypeStruct((bsz, t, n_heads * hd), BF16),
                   jax.ShapeDtypeStruct((bsz, t, n_heads * hd), F32),
                   jax.ShapeDtypeStruct((bsz, t, n_heads * hd), F32)),
        grid=(bsz, n_heads, nb),
        in_specs=[pl.BlockSpec((None, blk, hd), lambda b, h, i: (b, i, q_off + h)),
                  seq(q_off + n_heads), seq(q_off + 2 * n_heads), full, full, full],
        out_specs=(pl.BlockSpec((None, blk, hd), lambda b, h, i: (b, i, h)), seq(0), seq(0)),
        scratch_shapes=[pltpu.VMEM((t, hd), BF16), pltpu.VMEM((t, hd), BF16), pltpu.VMEM((LANES, hd), F32)],
        compiler_params=_params("arbitrary", "arbitrary", "arbitrary"),
        name="moba_prompt",
    )(z, z, z, c, s1, s2)


GATE_PAGES_PER_STEP = 8


def _moba_gate_kernel(pt_ref, z_ref, c_ref, s1_ref, s2_ref, *rest, n_heads, q_off, topk):
    pages = rest[:GATE_PAGES_PER_STEP]
    gate_ref, idx_ref = rest[GATE_PAGES_PER_STEP:]
    s = pl.program_id(1)
    bps = GATE_PAGES_PER_STEP // PAGES_PER_BLOCK

    @pl.when(s == 0)
    def _():
        gate_ref[...] = jnp.full(gate_ref.shape, NEG_INF, F32)

    q = _moba_rope(z_ref[q_off:q_off + n_heads, :], c_ref[...], s1_ref[...], s2_ref[...])
    lane = lax.broadcasted_iota(jnp.int32, gate_ref.shape, 1)
    for blk in range(bps):
        ksum = jnp.zeros_like(q)
        for p in range(PAGES_PER_BLOCK):
            ksum = ksum + jnp.sum(pages[blk * PAGES_PER_BLOCK + p][...], axis=0)
        val = jnp.sum(q * (ksum * (1.0 / MOBA_BLOCK)), axis=1, keepdims=True)
        gate_ref[...] = jnp.where(lane == s * bps + blk, val, gate_ref[...])

    @pl.when(s == pl.num_programs(1) - 1)
    def _():
        g = gate_ref[...]
        lane_f = lane.astype(F32)
        out = jnp.zeros(g.shape, F32)
        for r in range(topk):
            mx = jnp.max(g, axis=1, keepdims=True)
            am = jnp.min(jnp.where(g == mx, lane_f, float(LANES)), axis=1, keepdims=True)
            out = jnp.where(lane_f == float(r), am, out)
            g = jnp.where(lane_f == am, NEG_INF, g)
        idx_ref[...] = out.astype(jnp.int32)


def _moba_sample_gate(z3, cache_k, page_table, e, tables, n_heads, q_off):
    db, groups, hd = z3.shape
    n_pages = page_table.shape[1]
    n_blocks = n_pages // PAGES_PER_BLOCK
    assert n_pages % GATE_PAGES_PER_STEP == 0 and n_blocks <= LANES
    topk = min(MOBA_TOPK, n_blocks)
    steps = n_pages // GATE_PAGES_PER_STEP
    row = pl.BlockSpec((1, hd), lambda b, s, pt: (0, 0))

    def page(p):
        return pl.BlockSpec((None, None, PAGE_SIZE, n_heads, hd),
                            lambda b, s, pt, p=p: (e, pt[b * n_pages + s * GATE_PAGES_PER_STEP + p], 0, 0, 0))

    gate, idx = pl.pallas_call(
        functools.partial(_moba_gate_kernel, n_heads=n_heads, q_off=q_off, topk=topk),
        out_shape=(jax.ShapeDtypeStruct((db, n_heads, LANES), F32),
                   jax.ShapeDtypeStruct((db, n_heads, LANES), jnp.int32)),
        grid_spec=pltpu.PrefetchScalarGridSpec(
            num_scalar_prefetch=1,
            grid=(db, steps),
            in_specs=[pl.BlockSpec((None, groups, hd), lambda b, s, pt: (b, 0, 0)), row, row, row]
                     + [page(p) for p in range(GATE_PAGES_PER_STEP)],
            out_specs=(pl.BlockSpec((None, n_heads, LANES), lambda b, s, pt: (b, 0, 0)),
                       pl.BlockSpec((None, n_heads, LANES), lambda b, s, pt: (b, 0, 0)))),
        compiler_params=_params("arbitrary", "arbitrary"),
        name="moba_sample_gate",
    )(page_table.reshape(-1), z3, *tables, *([cache_k] * GATE_PAGES_PER_STEP))
    return idx[:, :, :topk]


def _moba_sample_attn_kernel(pg_ref, z_ref, c_ref, s1_ref, s2_ref, *rest, n_heads, n_sel, q_off, scale):
    k_pages, v_pages = rest[:n_sel], rest[n_sel:2 * n_sel]
    o_ref, kn_ref, vn_ref = rest[2 * n_sel:]
    h = pl.program_id(1)
    hd = HEAD_DIM
    rows = PAGE_SIZE * n_heads
    q = _moba_rope(z_ref[pl.ds(q_off + h, 1), :], c_ref[...], s1_ref[...], s2_ref[...])
    kn = _moba_rope(z_ref[pl.ds(q_off + n_heads + h, 1), :], c_ref[...], s1_ref[...], s2_ref[...])
    vn = z_ref[pl.ds(q_off + 2 * n_heads + h, 1), :]
    kn_ref[...] = kn
    vn_ref[...] = vn
    s_own = jnp.sum(q * kn, axis=1, keepdims=True) * scale

    q8 = jnp.broadcast_to(q, (SUBLANES, hd)).astype(BF16)
    col = lax.broadcasted_iota(jnp.int32, (1, rows), 1)
    mine = (col % n_heads) == h
    scores = []
    for kp in k_pages:
        sc = _dot_nt(q8, kp[...].reshape(rows, hd).astype(BF16))[0:1] * scale
        scores.append(jnp.where(mine, sc, NEG_INF))
    m = s_own
    for sc in scores:
        m = jnp.maximum(m, jnp.max(sc, axis=1, keepdims=True))
    l = jnp.exp(s_own - m)
    acc = l * vn
    for sc, vp in zip(scores, v_pages):
        p = jnp.exp(sc - m)
        l = l + jnp.sum(p, axis=1, keepdims=True)
        p8 = jnp.broadcast_to(p, (SUBLANES, rows)).astype(BF16)
        acc = acc + _dot(p8, vp[...].reshape(rows, hd).astype(BF16))[0:1]
    o_ref[...] = acc / l


def _moba_sample_attn(z3, cache_k, cache_v, pages, e, tables, n_heads, q_off):
    db, groups, hd = z3.shape
    n_sel = pages.shape[2]
    row = pl.BlockSpec((1, hd), lambda b, h, pg: (0, 0))

    def page_spec(s):
        return pl.BlockSpec((None, None, PAGE_SIZE, n_heads, hd),
                            lambda b, h, pg, s=s: (e, pg[(b * n_heads + h) * n_sel + s], 0, 0, 0))

    out = pl.BlockSpec((None, None, 1, hd), lambda b, h, pg: (b, h, 0, 0))
    shape = jax.ShapeDtypeStruct((db, n_heads, 1, hd), F32)
    return pl.pallas_call(
        functools.partial(_moba_sample_attn_kernel, n_heads=n_heads, n_sel=n_sel, q_off=q_off, scale=hd ** -0.5),
        out_shape=(shape, shape, shape),
        grid_spec=pltpu.PrefetchScalarGridSpec(
            num_scalar_prefetch=1,
            grid=(db, n_heads),
            in_specs=[pl.BlockSpec((None, groups, hd), lambda b, h, pg: (b, 0, 0)), row, row, row]
                     + [page_spec(s) for s in range(n_sel)] * 2,
            out_specs=(out, out, out)),
        compiler_params=_params("arbitrary", "arbitrary"),
        name="moba_sample_attn",
    )(pages.reshape(-1), z3, *tables, *([cache_k] * n_sel), *([cache_v] * n_sel))


RET_CHUNK = 256


def _ret_log_gamma(h, shape):
    hf = jnp.full(shape, h, jnp.int32).astype(F32)
    return jnp.log(1.0 - jnp.exp2(-5.0 - hf))


def _layer_norm_gate(o, gn, g):
    oc = o - jnp.mean(o, axis=1, keepdims=True)
    on = oc * lax.rsqrt(jnp.mean(oc * oc, axis=1, keepdims=True) + EPS) * gn
    return on * _silu(g)


RET_HEADS_PER_STEP = 2


def _ret_prompt_kernel(q_ref, k_ref, v_ref, g_ref, c_ref, s_ref, gn_ref, o_ref, so_ref, st_s, *, dk):
    c = pl.program_id(2)
    C = RET_CHUNK
    hp = RET_HEADS_PER_STEP

    @pl.when(c == 0)
    def _():
        st_s[...] = jnp.zeros_like(st_s)

    rows = pl.ds(pl.multiple_of(c * C, C), C)
    cos, sin = c_ref[rows, :], s_ref[rows, :]
    t_sq = lax.broadcasted_iota(jnp.int32, (C, C), 0)
    s_sq = lax.broadcasted_iota(jnp.int32, (C, C), 1)
    t_w = lax.broadcasted_iota(jnp.int32, (C, dk), 0).astype(F32)

    for hh in range(hp):
        ln = slice(hh * dk, (hh + 1) * dk)
        qr = _ret_rope(q_ref[:, ln], cos, sin)
        kr = _ret_rope(k_ref[:, ln], cos, sin) * (dk ** -0.5)
        vb = v_ref[:, ln].astype(BF16)
        lg = _ret_log_gamma(pl.program_id(1) * hp + hh, (1, 1))
        dmat = jnp.exp(jnp.where(s_sq <= t_sq, (t_sq - s_sq).astype(F32) * lg, NEG_INF))
        inter = jnp.exp((t_w + 1.0) * lg)
        state_dec = jnp.exp((C - 1.0 - t_w) * lg)
        chunk_dec = jnp.exp(float(C) * lg)

        st = st_s[hh]
        qb = qr.astype(BF16)
        att = _dot_nt(qb, kr.astype(BF16)) * dmat
        o = _dot(att.astype(BF16), vb) + _dot(qb, st.astype(BF16)) * inter
        st_s[hh] = chunk_dec * st + _dot_tn((kr * state_dec).astype(BF16), vb)
        o_ref[:, ln] = _layer_norm_gate(o, gn_ref[:, ln], g_ref[:, ln]).astype(o_ref.dtype)

    @pl.when(c == pl.num_programs(2) - 1)
    def _():
        so_ref[...] = st_s[...]


def _ret_prompt(z, tables, gn, r, n_heads):
    bsz, t, width = z.shape
    dk = width // (4 * n_heads)
    C = RET_CHUNK
    hp = RET_HEADS_PER_STEP
    assert t % C == 0 and n_heads % hp == 0
    ng = n_heads // hp
    cos, sin = tables
    full = pl.BlockSpec((t, dk // 2), lambda b, h, c: (0, 0))

    def col(off):
        return pl.BlockSpec((None, C, hp * dk), lambda b, h, c, off=off: (b, c, off * ng + h))

    return pl.pallas_call(
        functools.partial(_ret_prompt_kernel, dk=dk),
        out_shape=(jax.ShapeDtypeStruct((bsz, t, n_heads * dk), BF16),
                   jax.ShapeDtypeStruct((bsz, n_heads, dk, dk), F32)),
        grid=(bsz, ng, t // C),
        in_specs=[col(0), col(1), col(2), col(3), full, full,
                  pl.BlockSpec((None, 1, hp * dk), lambda b, h, c: (r, 0, h))],
        out_specs=(pl.BlockSpec((None, C, hp * dk), lambda b, h, c: (b, c, h)),
                   pl.BlockSpec((None, hp, dk, dk), lambda b, h, c: (b, h, 0, 0))),
        scratch_shapes=[pltpu.VMEM((hp, dk, dk), F32)],
        compiler_params=_params("arbitrary", "arbitrary", "arbitrary"),
        name="retention_prompt",
    )(z, z, z, z, cos, sin, gn.reshape(gn.shape[0], 1, -1))


def _ret_sample_kernel(z_ref, s0_ref, c_ref, s_ref, gn_ref, o_ref, so_ref, *, n_heads, dk):
    h = pl.program_id(1)
    cos, sin = c_ref[...], s_ref[...]
    qr = _ret_rope(z_ref[pl.ds(h, 1), :], cos, sin)
    kr = _ret_rope(z_ref[pl.ds(n_heads + h, 1), :], cos, sin) * (dk ** -0.5)
    v = z_ref[pl.ds(2 * n_heads + h, 1), :]
    g = z_ref[pl.ds(3 * n_heads + h, 1), :]
    gamma = jnp.exp(_ret_log_gamma(h, (1, 1)))
    s0 = s0_ref[...]
    qs = jnp.sum(_row_to_col(qr, dk) * s0, axis=0, keepdims=True)
    o = jnp.sum(qr * kr, axis=1, keepdims=True) * v + qs * gamma
    so_ref[...] = gamma * s0 + _row_to_col(kr, dk) * v
    o_ref[...] = _layer_norm_gate(o, gn_ref[pl.ds(h, 1), :], g)


def _ret_sample(z4, s0, tables, gn, r, n_heads):
    db, groups, dk = z4.shape
    cos, sin = tables
    row = pl.BlockSpec((1, dk // 2), lambda b, h: (0, 0))
    return pl.pallas_call(
        functools.partial(_ret_sample_kernel, n_heads=n_heads, dk=dk),
        out_shape=(jax.ShapeDtypeStruct((db, n_heads, 1, dk), F32),
                   jax.ShapeDtypeStruct((db, n_heads, dk, dk), F32)),
        grid=(db, n_heads),
        in_specs=[pl.BlockSpec((None, groups, dk), lambda b, h: (b, 0, 0)),
                  pl.BlockSpec((None, None, None, dk, dk), lambda b, h: (r, b, h, 0, 0)), row, row,
                  pl.BlockSpec((None, n_heads, dk), lambda b, h: (r, 0, 0))],
        out_specs=(pl.BlockSpec((None, None, 1, dk), lambda b, h: (b, h, 0, 0)),
                   pl.BlockSpec((None, None, dk, dk), lambda b, h: (b, h, 0, 0))),
        compiler_params=_params("arbitrary", "arbitrary"),
        name="retention_sample",
    )(z4, s0, cos, sin, gn)


def kernel(x_prompt, x_sample, cache_k, cache_v, page_table, state_hgrn, state_ret, state_conv,
           norm_mix, norm_ffn, norm_final, w_in_even, w_out_even, gn_hgrn, lb_hgrn,
           w_in_odd, w_out_odd, gn_ret, w_ffn_in, conv_w, conv_b, w_ffn_out):
    bsz, t, d = x_prompt.shape
    db, ts, _ = x_sample.shape
    assert ts == 1
    depth = norm_mix.shape[0]
    n_a = gn_hgrn.shape[1] // HEAD_DIM
    n_b = (w_in_even.shape[2] - 4 * gn_hgrn.shape[1]) // (3 * HEAD_DIM)
    assert n_b == SUBLANES
    n_pages = page_table.shape[1]
    assert n_pages % PAGES_PER_BLOCK == 0
    dk_c = w_in_odd.shape[2] // (4 * H_C)
    pos_p = np.arange(t)
    pos_s = np.array([n_pages * PAGE_SIZE])
    moba_tab_p, moba_tab_s = _moba_rope_tables(pos_p), _moba_rope_tables(pos_s)
    ret_tab_p, ret_tab_s = _ret_rope_tables(pos_p, dk_c), _ret_rope_tables(pos_s, dk_c)
    gn_ret3 = gn_ret.reshape(gn_ret.shape[0], H_C, dk_c)
    conv_buf = jnp.swapaxes(state_conv, 1, 2)
    mp = bsz * t

    xp = x_prompt.reshape(mp, d)
    xs = x_sample.reshape(db, d)
    k_p, v_p, h_p, r_p, c_p = [], [], [], [], []
    k_s, v_s, h_s, r_s, c_s = [], [], [], [], []
    for l in range(depth):
        hp_, hs_ = _rmsnorm(xp, norm_mix, l, BF16), _rmsnorm(xs, norm_mix, l, BF16)
        if l % 2 == 0:
            e = l // 2
            zp, zs = _mm([hp_], [hs_], w_in_even, e)
            zp = zp.reshape(bsz, t, -1)
            oa, st = _hgrn_prompt(zp, lb_hgrn, gn_hgrn, e, n_a)
            ob, k_rot, v_rows = _moba_prompt(zp, moba_tab_p, n_b, 4 * n_a)
            h_p.append(st)
            k_p.append(k_rot.reshape(bsz, t, n_b, HEAD_DIM))
            v_p.append(v_rows.reshape(bsz, t, n_b, HEAD_DIM))

            z3 = zs.reshape(db, -1, HEAD_DIM)
            oa_s, st = _hgrn_sample(z3, state_hgrn, lb_hgrn, gn_hgrn, e, n_a)
            idx = _moba_sample_gate(z3, cache_k, page_table, e, moba_tab_s, n_b, 4 * n_a)
            logical = idx[..., None] * PAGES_PER_BLOCK + jnp.arange(PAGES_PER_BLOCK, dtype=jnp.int32)
            pages = jnp.take_along_axis(page_table[:, None, :], logical.reshape(db, n_b, -1), axis=2)
            ob_s, k_rot, v_rows = _moba_sample_attn(z3, cache_k, cache_v, pages, e, moba_tab_s, n_b, 4 * n_a)
            h_s.append(st)
            k_s.append(k_rot.reshape(db, ts, n_b, HEAD_DIM))
            v_s.append(v_rows.reshape(db, ts, n_b, HEAD_DIM))

            xp, xs = _mm([oa.reshape(mp, -1), ob.reshape(mp, -1)],
                         [oa_s.reshape(db, -1).astype(BF16), ob_s.reshape(db, -1).astype(BF16)],
                         w_out_even, e, res=xp, res_s=xs)
        else:
            r = l // 2
            zp, zs = _mm([hp_], [hs_], w_in_odd, r)
            o, st = _ret_prompt(zp.reshape(bsz, t, -1), ret_tab_p, gn_ret, r, H_C)
            r_p.append(st)
            o_s, st = _ret_sample(zs.reshape(db, 4 * H_C, dk_c), state_ret, ret_tab_s, gn_ret3, r, H_C)
            r_s.append(st)
            xp, xs = _mm([o.reshape(mp, -1)], [o_s.reshape(db, -1).astype(BF16)], w_out_odd, r, res=xp, res_s=xs)
        hp_, hs_ = _rmsnorm(xp, norm_ffn, l, BF16), _rmsnorm(xs, norm_ffn, l, BF16)
        act, buf, act_s, buf_s = _ffn_in(hp_, hs_, conv_buf, w_ffn_in, conv_w, conv_b, l, bsz, t)
        c_p.append(buf)
        c_s.append(jnp.swapaxes(buf_s, 0, 1))
        xp, xs = _mm([act], [act_s], w_ffn_out, l, res=xp, res_s=xs)
    y_p = _rmsnorm(xp, norm_final.reshape(1, d), 0, F32).reshape(bsz, t, d)
    y_s = _rmsnorm(xs, norm_final.reshape(1, d), 0, F32).reshape(db, ts, d)
    st = jnp.stack
    return (y_p, y_s, st(k_p), st(v_p), st(k_s), st(v_s), st(h_p), st(h_s), st(r_p), st(r_s), st(c_p), st(c_s))
```

```python
import functools

import numpy as np
import jax
import jax.numpy as jnp
from jax import lax
from jax.experimental import pallas as pl
from jax.experimental.pallas import tpu as pltpu

F32 = jnp.float32
BF16 = jnp.bfloat16

HEAD_DIM = 128
MOBA_BLOCK = 256
MOBA_TOPK = 3
PAGE_SIZE = 128
PAGES_PER_BLOCK = MOBA_BLOCK // PAGE_SIZE
ROT_DIM = HEAD_DIM // 4
ROPE_THETA = 500000.0
RET_THETA = 10000.0
H_C = 8
SCAN_CHUNK = 64
SUB_CHUNK = 16
SUBLANES = 8
CONV_W = 3
EPS = 1e-6
LANES = 128
VMEM_LIMIT = 56 * 1024 * 1024
NEG_INF = float("-inf")
HIGHEST = lax.Precision.HIGHEST


def _params(*sem):
    return pltpu.CompilerParams(dimension_semantics=sem, vmem_limit_bytes=VMEM_LIMIT)


def _dot(a, b, precision=None):
    return jnp.dot(a, b, preferred_element_type=F32, precision=precision)


def _dot_nt(a, b, precision=None):
    return lax.dot_general(a, b, (((1,), (1,)), ((), ())), preferred_element_type=F32, precision=precision)


def _dot_tn(a, b, precision=None):
    return lax.dot_general(a, b, (((0,), (0,)), ((), ())), preferred_element_type=F32, precision=precision)


def _sigmoid(x):
    return 1.0 / (1.0 + jnp.exp(-x))


def _silu(x):
    return x * _sigmoid(x)


def _log_sigmoid(x):
    return jnp.minimum(x, 0.0) - jnp.log1p(jnp.exp(-jnp.abs(x)))


def _logaddexp(a, b):
    return jnp.maximum(a, b) + jnp.log1p(jnp.exp(-jnp.abs(a - b)))


def _row_to_col(row, n):
    r = lax.broadcasted_iota(jnp.int32, (n, n), 0)
    c = lax.broadcasted_iota(jnp.int32, (n, n), 1)
    return jnp.sum(jnp.where(r == c, jnp.broadcast_to(row, (n, n)), 0.0), axis=1, keepdims=True)


def _rmsnorm_kernel(x_ref, g_ref, o_ref):
    x = x_ref[...]
    y = x * lax.rsqrt(jnp.mean(x * x, axis=-1, keepdims=True) + EPS)
    o_ref[...] = (y * g_ref[...]).astype(o_ref.dtype)


def _rmsnorm(x, g, l, out_dtype):
    m, d = x.shape
    tm = min(m, 512)
    return pl.pallas_call(
        _rmsnorm_kernel,
        out_shape=jax.ShapeDtypeStruct((m, d), out_dtype),
        grid=(m // tm,),
        in_specs=[pl.BlockSpec((tm, d), lambda i: (i, 0)), pl.BlockSpec((None, 1, d), lambda i: (l, 0, 0))],
        out_specs=pl.BlockSpec((tm, d), lambda i: (i, 0)),
        compiler_params=_params("arbitrary"),
        name="rmsnorm",
    )(x, g.reshape(-1, 1, d))


def _split_dot(a_refs, wb_ref):
    kp = wb_ref.shape[0] // len(a_refs)
    acc = None
    for p, a_ref in enumerate(a_refs):
        part = _dot(a_ref[...], wb_ref[p * kp:(p + 1) * kp, :])
        acc = part if acc is None else acc + part
    return acc


def _mm_kernel(*refs, n_parts, has_res):
    a_refs, refs = refs[:n_parts], refs[n_parts:]
    as_refs, refs = refs[:n_parts], refs[n_parts:]
    w_ref, refs = refs[0], refs[1:]
    if has_res:
        r_ref, rs_ref, o_ref, os_ref, wb_ref = refs
    else:
        o_ref, os_ref, wb_ref = refs

    @pl.when(pl.program_id(1) == 0)
    def _():
        wb_ref[...] = w_ref[...].astype(BF16)
        acc_s = _split_dot(as_refs, wb_ref)
        if has_res:
            acc_s = rs_ref[...] + acc_s
        os_ref[...] = acc_s

    acc = _split_dot(a_refs, wb_ref)
    if has_res:
        acc = r_ref[...] + acc
    o_ref[...] = acc


def _mm_tiles(m, k, n):
    big = k <= 2048
    return min(m, 1024 if big else 512), min(n, 1024 if big else 512)


def _mm(a_parts, as_parts, w, l, res=None, res_s=None):
    n_parts = len(a_parts)
    m, kp = a_parts[0].shape
    ms = as_parts[0].shape[0]
    k, n = w.shape[1], w.shape[2]
    assert kp * n_parts == k
    tm, tn = _mm_tiles(m, k, n)
    assert m % tm == 0 and n % tn == 0
    in_specs = ([pl.BlockSpec((tm, kp), lambda j, i: (i, 0))] * n_parts
                + [pl.BlockSpec((ms, kp), lambda j, i: (0, 0))] * n_parts
                + [pl.BlockSpec((None, k, tn), lambda j, i: (l, 0, j))])
    args = [*a_parts, *as_parts, w]
    if res is not None:
        in_specs += [pl.BlockSpec((tm, tn), lambda j, i: (i, j)), pl.BlockSpec((ms, tn), lambda j, i: (0, j))]
        args += [res, res_s]
    return pl.pallas_call(
        functools.partial(_mm_kernel, n_parts=n_parts, has_res=res is not None),
        out_shape=(jax.ShapeDtypeStruct((m, n), F32), jax.ShapeDtypeStruct((ms, n), F32)),
        grid=(n // tn, m // tm),
        in_specs=in_specs,
        out_specs=(pl.BlockSpec((tm, tn), lambda j, i: (i, j)), pl.BlockSpec((ms, tn), lambda j, i: (0, j))),
        scratch_shapes=[pltpu.VMEM((k, tn), BF16)],
        compiler_params=_params("arbitrary", "arbitrary"),
        name="matmul",
    )(*args)


def _conv_gate(a, g, p0, p1, cw_ref, cb_ref):
    row = lax.broadcasted_iota(jnp.int32, a.shape, 0)
    a1 = jnp.where(row == 0, p1, pltpu.roll(a, 1, 0))
    a2 = jnp.where(row == 0, p0, jnp.where(row == 1, p1, pltpu.roll(a, 2, 0)))
    conv = cb_ref[...] + cw_ref[0:1, :] * a2
    conv = conv + cw_ref[1:2, :] * a1
    conv = conv + cw_ref[2:3, :] * a
    return _silu(conv) * g


def _ffn_in_kernel(h_ref, hs_ref, bufs_ref, wa_ref, wb_ref, cw_ref, cb_ref, o_ref, nb_ref, os_ref, nbs_ref,
                   wa_s, wb_s, carry_s, *, tiles_per_seq):
    i = pl.program_id(1)

    @pl.when(i == 0)
    def _():
        wa_s[...] = wa_ref[...].astype(BF16)
        wb_s[...] = wb_ref[...].astype(BF16)
        hs = hs_ref[...]
        a_s = _dot(hs, wa_s[...])
        conv = cb_ref[...] + cw_ref[0:1, :] * bufs_ref[0]
        conv = conv + cw_ref[1:2, :] * bufs_ref[1]
        conv = conv + cw_ref[2:3, :] * a_s
        os_ref[...] = (_silu(conv) * _dot(hs, wb_s[...])).astype(os_ref.dtype)
        nbs_ref[0] = bufs_ref[1]
        nbs_ref[1] = a_s

    h = h_ref[...]
    a = _dot(h, wa_s[...])
    g = _dot(h, wb_s[...])
    tm = a.shape[0]
    seq_start = (i % tiles_per_seq) == 0
    p0 = jnp.where(seq_start, 0.0, carry_s[SUBLANES - 2:SUBLANES - 1, :])
    p1 = jnp.where(seq_start, 0.0, carry_s[SUBLANES - 1:SUBLANES, :])
    o_ref[...] = _conv_gate(a, g, p0, p1, cw_ref, cb_ref).astype(o_ref.dtype)
    carry_s[...] = a[tm - SUBLANES:tm, :]

    @pl.when((i % tiles_per_seq) == tiles_per_seq - 1)
    def _():
        nb_ref[...] = a[tm - (CONV_W - 1):tm, :]


def _ffn_in(h, h_s, buf_s, w, cw, cb, l, bsz, t):
    m, k = h.shape
    db = h_s.shape[0]
    f = w.shape[2] // 2
    tm = min(t, 1024)
    tn = 512
    assert t % tm == 0 and f % tn == 0 and tm >= SUBLANES
    nf = f // tn
    tiles_per_seq = t // tm
    return pl.pallas_call(
        functools.partial(_ffn_in_kernel, tiles_per_seq=tiles_per_seq),
        out_shape=(jax.ShapeDtypeStruct((m, f), BF16),
                   jax.ShapeDtypeStruct((bsz, CONV_W - 1, f), F32),
                   jax.ShapeDtypeStruct((db, f), BF16),
                   jax.ShapeDtypeStruct((CONV_W - 1, db, f), F32)),
        grid=(nf, m // tm),
        in_specs=[pl.BlockSpec((tm, k), lambda j, i: (i, 0)),
                  pl.BlockSpec((db, k), lambda j, i: (0, 0)),
                  pl.BlockSpec((None, CONV_W - 1, db, tn), lambda j, i: (l, 0, 0, j)),
                  pl.BlockSpec((None, k, tn), lambda j, i: (l, 0, j)),
                  pl.BlockSpec((None, k, tn), lambda j, i: (l, 0, nf + j)),
                  pl.BlockSpec((None, CONV_W, tn), lambda j, i: (l, 0, j)),
                  pl.BlockSpec((None, 1, tn), lambda j, i: (l, 0, j))],
        out_specs=(pl.BlockSpec((tm, tn), lambda j, i: (i, j)),
                   pl.BlockSpec((None, CONV_W - 1, tn), lambda j, i: (i // tiles_per_seq, 0, j)),
                   pl.BlockSpec((db, tn), lambda j, i: (0, j)),
                   pl.BlockSpec((CONV_W - 1, db, tn), lambda j, i: (0, 0, j))),
        scratch_shapes=[pltpu.VMEM((k, tn), BF16), pltpu.VMEM((k, tn), BF16), pltpu.VMEM((SUBLANES, tn), F32)],
        compiler_params=_params("arbitrary", "arbitrary"),
        name="ffn_in_conv",
    )(h, h_s, buf_s, w, w, cw, cb.reshape(-1, 1, f))


HGRN_HEADS_PER_STEP = 8


def _hgrn_lower_bound(lbp, e):
    mx = jnp.max(lbp, axis=0, keepdims=True)
    ex = jnp.exp(lbp - mx)
    sm = ex / jnp.sum(ex, axis=0, keepdims=True)
    lb = jnp.zeros_like(sm[0:1])
    for l in range(1, e + 1):
        lb = lb + sm[l:l + 1]
    return lb


def _hgrn_gates(zf, zi, lb):
    logf = _logaddexp(jnp.log(lb), jnp.log1p(-lb) + _log_sigmoid(zf))
    return logf, 1.0 - jnp.exp(logf), _silu(zi)


def _hgrn_diag(qi, ki, vi, bi):
    half = SUBLANES
    row = lax.broadcasted_iota(jnp.int32, (half, HEAD_DIM), 0)
    q_lo, q_hi = qi[:half], qi[half:]
    b_lo, b_hi = bi[:half], bi[half:]
    o_lo = jnp.zeros((half, HEAD_DIM), F32)
    o_hi = jnp.zeros((half, HEAD_DIM), F32)
    for s in range(SUB_CHUNK):
        ks, vs, bs = ki[s:s + 1], vi[s:s + 1], bi[s:s + 1]
        if s < half:
            dec = jnp.exp(jnp.where(row >= s, b_lo - bs, NEG_INF))
            o_lo = o_lo + jnp.sum(q_lo * ks * dec, axis=1, keepdims=True) * vs
            dec = jnp.exp(b_hi - bs)
        else:
            dec = jnp.exp(jnp.where(row >= s - half, b_hi - bs, NEG_INF))
        o_hi = o_hi + jnp.sum(q_hi * ks * dec, axis=1, keepdims=True) * vs
    return jnp.concatenate([o_lo, o_hi], axis=0)


def _rel(x, b, i):
    r0 = i * SUB_CHUNK
    return x if i == 0 else x - b[r0 - 1:r0]


def _hgrn_intra_mxu(zq, kk, vv, b):
    C, cs = SCAN_CHUNK, SUB_CHUNK
    q_parts, k_parts = [], []
    for i in range(C // cs):
        r0, r1 = i * cs, (i + 1) * cs
        qt = (zq[r0:r1] * jnp.exp(_rel(b[r0:r1], b, i))).astype(BF16)
        kt = (kk[:r1] * jnp.exp(-_rel(b[:r1], b, i))).astype(BF16)
        q_rows = [qt]
        if r0 > 0:
            q_rows.insert(0, jnp.zeros((r0, HEAD_DIM), BF16))
        if r1 < C:
            q_rows.append(jnp.zeros((C - r1, HEAD_DIM), BF16))
            kt = jnp.concatenate([kt, jnp.zeros((C - r1, HEAD_DIM), BF16)], axis=0)
        q_parts.append(jnp.concatenate(q_rows, axis=0))
        k_parts.append(kt)
    return _dot_nt(jnp.concatenate(q_parts, axis=1), jnp.concatenate(k_parts, axis=1))


def _hgrn_intra_mxu_apply(att, vv):
    C = SCAN_CHUNK
    t_i = lax.broadcasted_iota(jnp.int32, (C, C), 0)
    s_i = lax.broadcasted_iota(jnp.int32, (C, C), 1)
    return _dot(jnp.where(s_i <= t_i, att, 0.0).astype(BF16), vv.astype(BF16))


def _hgrn_intra_exact(zq, kk, vv, b):
    cs = SUB_CHUNK
    parts = []
    for i in range(SCAN_CHUNK // cs):
        r0, r1 = i * cs, (i + 1) * cs
        qi, bi = zq[r0:r1], b[r0:r1]
        oi = _hgrn_diag(qi, kk[r0:r1], vv[r0:r1], bi)
        if i > 0:
            qt = qi * jnp.exp(_rel(bi, b, i))
            kt = kk[:r0] * jnp.exp(-_rel(b[:r0], b, i))
            att = _dot_nt(qt.astype(BF16), kt.astype(BF16))
            oi = oi + _dot(att.astype(BF16), vv[:r0].astype(BF16))
        parts.append(oi)
    return jnp.concatenate(parts, axis=0)


HGRN_SAFE_SUB_DECAY = -80.0


def _hgrn_prompt_kernel(q_ref, f_ref, i_ref, g_ref, lbp_ref, gn_ref, o_ref, s_ref, st_ref, kk_s, vv_s, b_s,
                        *, e, n_chunks):
    c = pl.program_id(2)
    C, cs = SCAN_CHUNK, SUB_CHUNK
    hp = HGRN_HEADS_PER_STEP
    hd = HEAD_DIM

    @pl.when(c == 0)
    def _():
        st_ref[...] = jnp.zeros_like(st_ref)

    lb = _hgrn_lower_bound(lbp_ref[...], e)
    gn = gn_ref[...]
    r_i = lax.broadcasted_iota(jnp.int32, (C, C), 0)
    c_i = lax.broadcasted_iota(jnp.int32, (C, C), 1)
    tri = (c_i <= r_i).astype(F32)

    def rows_of(ci):
        return pl.ds(pl.multiple_of(ci * C, C), C)

    def gates(ci, sub_decay):
        rows = rows_of(ci)
        logf, kk, vv = _hgrn_gates(f_ref[rows, :], i_ref[rows, :], lb)
        b = _dot(tri, logf, HIGHEST)
        kk_s[rows, :] = kk
        vv_s[rows, :] = vv
        b_s[rows, :] = b
        for i in range(C // cs):
            sub_decay = jnp.minimum(sub_decay, _rel(b[(i + 1) * cs - 1:(i + 1) * cs], b, i))
        return sub_decay

    sub_decay = lax.fori_loop(0, n_chunks, gates, jnp.zeros((1, hp * hd), F32))
    safe = jnp.min(sub_decay) >= HGRN_SAFE_SUB_DECAY

    def scan(intra_first, intra_second):
        def chunk(ci, carry):
            rows = rows_of(ci)
            zq, zg = q_ref[rows, :], g_ref[rows, :]
            kk_all, vv_all, b_all = kk_s[rows, :], vv_s[rows, :], b_s[rows, :]
            firsts = []
            for hh in range(hp):
                ln = slice(hh * hd, (hh + 1) * hd)
                q, kk, vv, b = zq[:, ln], kk_all[:, ln], vv_all[:, ln], b_all[:, ln]
                st = st_ref[hh]
                o_inter = _dot_nt((q * jnp.exp(b)).astype(BF16), st.astype(BF16))
                firsts.append((o_inter, intra_first(q, kk, vv, b)))
                bl = b[C - 1:C]
                khat = kk * jnp.exp(bl - b)
                st_ref[hh] = st * jnp.exp(bl) + _dot_tn(vv.astype(BF16), khat.astype(BF16))
            outs = []
            for hh, (o_inter, first) in enumerate(firsts):
                ln = slice(hh * hd, (hh + 1) * hd)
                o = o_inter + intra_second(first, vv_all[:, ln])
                on = o * lax.rsqrt(jnp.mean(o * o, axis=1, keepdims=True) + EPS) * gn[:, ln]
                outs.append(on * _sigmoid(zg[:, ln]))
            o_ref[rows, :] = jnp.concatenate(outs, axis=1).astype(o_ref.dtype)
            return carry
        lax.fori_loop(0, n_chunks, chunk, 0)

    @pl.when(safe)
    def _():
        scan(_hgrn_intra_mxu, _hgrn_intra_mxu_apply)

    @pl.when(jnp.logical_not(safe))
    def _():
        scan(_hgrn_intra_exact, lambda intra, vv: intra)

    @pl.when(c == pl.num_programs(2) - 1)
    def _():
        for hh in range(hp):
            s_ref[hh] = st_ref[hh].T


def _hgrn_prompt(z, lb_hgrn, gn, e, n_heads):
    bsz, t, _ = z.shape
    ct = min(t, 512)
    hp = HGRN_HEADS_PER_STEP
    assert t % ct == 0 and ct % SCAN_CHUNK == 0 and n_heads % hp == 0
    n_even = lb_hgrn.shape[0]
    hd = HEAD_DIM
    ng = n_heads // hp

    def col(off):
        return pl.BlockSpec((None, ct, hp * hd), lambda b, h, c, off=off: (b, c, off * ng + h))

    return pl.pallas_call(
        functools.partial(_hgrn_prompt_kernel, e=e, n_chunks=ct // SCAN_CHUNK),
        out_shape=(jax.ShapeDtypeStruct((bsz, t, n_heads * hd), BF16),
                   jax.ShapeDtypeStruct((bsz, n_heads, hd, hd), F32)),
        grid=(bsz, ng, t // ct),
        in_specs=[col(0), col(1), col(2), col(3),
                  pl.BlockSpec((n_even, hp * hd), lambda b, h, c: (0, h)),
                  pl.BlockSpec((None, 1, hp * hd), lambda b, h, c: (e, 0, h))],
        out_specs=(pl.BlockSpec((None, ct, hp * hd), lambda b, h, c: (b, c, h)),
                   pl.BlockSpec((None, hp, hd, hd), lambda b, h, c: (b, h, 0, 0))),
        scratch_shapes=[pltpu.VMEM((hp, hd, hd), F32)] + [pltpu.VMEM((ct, hp * hd), F32)] * 3,
        compiler_params=_params("arbitrary", "arbitrary", "arbitrary"),
        name="hgrn_prompt",
    )(z, z, z, z, lb_hgrn, gn.reshape(n_even, 1, -1))


def _hgrn_sample_kernel(z_ref, s0_ref, lbp_ref, gn_ref, o_ref, s_ref, *, e, n_heads):
    hd = HEAD_DIM
    lb_all = _hgrn_lower_bound(lbp_ref[...], e)
    for h in range(n_heads):
        lanes = slice(h * hd, (h + 1) * hd)
        zq = z_ref[h:h + 1, :]
        zg = z_ref[3 * n_heads + h:3 * n_heads + h + 1, :]
        logf, kk, vv = _hgrn_gates(z_ref[n_heads + h:n_heads + h + 1, :],
                                   z_ref[2 * n_heads + h:2 * n_heads + h + 1, :], lb_all[:, lanes])
        s_new = _row_to_col(jnp.exp(logf), hd) * s0_ref[h] + _row_to_col(kk, hd) * vv
        s_ref[h] = s_new
        o = jnp.sum(_row_to_col(zq, hd) * s_new, axis=0, keepdims=True)
        on = o * lax.rsqrt(jnp.mean(o * o, axis=1, keepdims=True) + EPS) * gn_ref[:, lanes]
        o_ref[h:h + 1, :] = on * _sigmoid(zg)


def _hgrn_sample(z3, s0, lb_hgrn, gn, e, n_heads):
    db, groups, hd = z3.shape
    n_even = lb_hgrn.shape[0]
    w = n_heads * hd
    return pl.pallas_call(
        functools.partial(_hgrn_sample_kernel, e=e, n_heads=n_heads),
        out_shape=(jax.ShapeDtypeStruct((db, n_heads, hd), F32),
                   jax.ShapeDtypeStruct((db, n_heads, hd, hd), F32)),
        grid=(db,),
        in_specs=[pl.BlockSpec((None, groups, hd), lambda b: (b, 0, 0)),
                  pl.BlockSpec((None, None, n_heads, hd, hd), lambda b: (e, b, 0, 0, 0)),
                  pl.BlockSpec((n_even, w), lambda b: (0, 0)),
                  pl.BlockSpec((None, 1, w), lambda b: (e, 0, 0))],
        out_specs=(pl.BlockSpec((None, n_heads, hd), lambda b: (b, 0, 0)),
                   pl.BlockSpec((None, n_heads, hd, hd), lambda b: (b, 0, 0, 0))),
        compiler_params=_params("arbitrary"),
        name="hgrn_sample",
    )(z3, s0, lb_hgrn, gn.reshape(n_even, 1, -1))


def _moba_rope_tables(pos):
    half = ROT_DIM // 2
    inv = ROPE_THETA ** (-np.arange(half, dtype=np.float64) / half)
    ang = np.asarray(pos, np.float64)[:, None] * inv[None, :]
    cos, sin = np.cos(ang), np.sin(ang)
    t = ang.shape[0]
    c = np.ones((t, HEAD_DIM))
    s1 = np.zeros((t, HEAD_DIM))
    s2 = np.zeros((t, HEAD_DIM))
    c[:, :half] = cos
    c[:, half:2 * half] = cos
    s1[:, :half] = -sin
    s2[:, half:2 * half] = sin
    return tuple(jnp.asarray(a, F32) for a in (c, s1, s2))


def _ret_rope_tables(pos, dk):
    half = dk // 2
    inv = RET_THETA ** (-np.arange(half, dtype=np.float64) / half)
    ang = np.asarray(pos, np.float64)[:, None] * inv[None, :]
    return jnp.asarray(np.cos(ang), F32), jnp.asarray(np.sin(ang), F32)


def _moba_rope(x, c, s1, s2):
    n = x.shape[-1]
    half = ROT_DIM // 2
    return x * c + pltpu.roll(x, n - half, 1) * s1 + pltpu.roll(x, half, 1) * s2


def _ret_rope(x, c, s):
    half = x.shape[-1] // 2
    x1, x2 = x[:, :half], x[:, half:]
    return jnp.concatenate([x1 * c - x2 * s, x2 * c + x1 * s], axis=1)


MOBA_MASK = -(2.0 ** 100)


def _moba_unselected(qr, km, n_past, topk):
    nbp = -(-n_past // SUBLANES) * SUBLANES
    gate = _dot_nt(km, qr, HIGHEST)[0:nbp]
    row = lax.broadcasted_iota(jnp.int32, gate.shape, 0)
    rank = jnp.zeros(gate.shape, F32)
    for jp in range(n_past):
        gj = gate[jp:jp + 1, :]
        rank = rank + jnp.where(gj > gate, 1.0, jnp.where(gj == gate, jnp.where(row > jp, 1.0, 0.0), 0.0))
    uns = jnp.where(row < n_past, jnp.where(rank < topk, 0.0, -1.0), 0.0)
    uns = jnp.concatenate([uns, jnp.zeros((LANES - nbp, uns.shape[1]), F32)], axis=0)
    return uns.T


def _moba_prompt_kernel(q_ref, k_ref, v_ref, c_ref, s1_ref, s2_ref, o_ref, ko_ref, vo_ref,
                        kb_s, vb_s, km_s, *, nb, topk, scale):
    i = pl.program_id(2)
    blk = MOBA_BLOCK
    hd = HEAD_DIM

    @pl.when(i == 0)
    def _():
        km_s[...] = jnp.zeros_like(km_s)
        lane = lax.broadcasted_iota(jnp.int32, (blk, LANES), 1)
        for j in range(nb):
            rows = slice(j * blk, (j + 1) * blk)
            kr = _moba_rope(k_ref[rows, :], c_ref[rows, :], s1_ref[rows, :], s2_ref[rows, :])
            ko_ref[rows, :] = kr
            kb_s[rows, 0:hd] = kr.astype(BF16)
            kb_s[rows, hd:hd + LANES] = jnp.where(lane == j, -MOBA_MASK, 0.0).astype(BF16)
            km_s[j:j + 1, :] = jnp.mean(kr, axis=0, keepdims=True)
            v = v_ref[rows, :]
            vo_ref[rows, :] = v
            vb_s[rows, :] = v.astype(BF16)

    rows = pl.ds(pl.multiple_of(i * blk, blk), blk)
    qr = _moba_rope(q_ref[...], c_ref[rows, :], s1_ref[rows, :], s2_ref[rows, :])
    qb = (qr * scale).astype(BF16)
    r_i = lax.broadcasted_iota(jnp.int32, (blk, blk), 0)
    c_i = lax.broadcasted_iota(jnp.int32, (blk, blk), 1)

    for ii in range(nb):
        @pl.when(i == ii)
        def _(ii=ii):
            ncol = (ii + 1) * blk
            if ii > topk:
                uns = _moba_unselected(qr, km_s[...], ii, topk)
                s = _dot_nt(jnp.concatenate([qb, uns.astype(BF16)], axis=1), kb_s[0:ncol, :])
            else:
                s = _dot_nt(qb, kb_s[0:ncol, 0:hd])
            pieces = [s[:, :ii * blk]] if ii > 0 else []
            pieces.append(jnp.where(c_i <= r_i, s[:, ii * blk:], NEG_INF))
            s = jnp.concatenate(pieces, axis=1)
            m = jnp.max(s, axis=1, keepdims=True)
            p = jnp.exp(s - m)
            l = jnp.sum(p, axis=1, keepdims=True)
            o_ref[...] = (_dot(p.astype(BF16), vb_s[0:ncol, :]) / l).astype(o_ref.dtype)


def _moba_prompt(z, tables, n_heads, q_off):
    bsz, t, _ = z.shape
    hd = HEAD_DIM
    blk = MOBA_BLOCK
    assert t % blk == 0
    nb = t // blk
    assert nb <= LANES
    topk = max(1, min(MOBA_TOPK, nb - 1))
    c, s1, s2 = tables
    full = pl.BlockSpec((t, hd), lambda b, h, i: (0, 0))

    def seq(off):
        return pl.BlockSpec((None, t, hd), lambda b, h, i, off=off: (b, 0, off + h))

    return pl.pallas_call(
        functools.partial(_moba_prompt_kernel, nb=nb, topk=topk, scale=hd ** -0.5),
        out_shape=(jax.ShapeDtypeStruct((bsz, t, n_heads * hd), BF16),
                   jax.ShapeDtypeStruct((bsz, t, n_heads * hd), F32),
                   jax.ShapeDtypeStruct((bsz, t, n_heads * hd), F32)),
        grid=(bsz, n_heads, nb),
        in_specs=[pl.BlockSpec((None, blk, hd), lambda b, h, i: (b, i, q_off + h)),
                  seq(q_off + n_heads), seq(q_off + 2 * n_heads), full, full, full],
        out_specs=(pl.BlockSpec((None, blk, hd), lambda b, h, i: (b, i, h)), seq(0), seq(0)),
        scratch_shapes=[pltpu.VMEM((t, hd + LANES), BF16), pltpu.VMEM((t, hd), BF16), pltpu.VMEM((LANES, hd), F32)],
        compiler_params=_params("arbitrary", "arbitrary", "arbitrary"),
        name="moba_prompt",
    )(z, z, z, c, s1, s2)


GATE_PAGES_PER_STEP = 8


def _moba_gate_kernel(pt_ref, z_ref, c_ref, s1_ref, s2_ref, *rest, n_heads, q_off, topk):
    pages = rest[:GATE_PAGES_PER_STEP]
    gate_ref, idx_ref = rest[GATE_PAGES_PER_STEP:]
    s = pl.program_id(1)
    bps = GATE_PAGES_PER_STEP // PAGES_PER_BLOCK

    @pl.when(s == 0)
    def _():
        gate_ref[...] = jnp.full(gate_ref.shape, NEG_INF, F32)

    q = _moba_rope(z_ref[q_off:q_off + n_heads, :], c_ref[...], s1_ref[...], s2_ref[...])
    lane = lax.broadcasted_iota(jnp.int32, gate_ref.shape, 1)
    for blk in range(bps):
        ksum = jnp.zeros_like(q)
        for p in range(PAGES_PER_BLOCK):
            ksum = ksum + jnp.sum(pages[blk * PAGES_PER_BLOCK + p][...], axis=0)
        val = jnp.sum(q * (ksum * (1.0 / MOBA_BLOCK)), axis=1, keepdims=True)
        gate_ref[...] = jnp.where(lane == s * bps + blk, val, gate_ref[...])

    @pl.when(s == pl.num_programs(1) - 1)
    def _():
        g = gate_ref[...]
        lane_f = lane.astype(F32)
        out = jnp.zeros(g.shape, F32)
        for r in range(topk):
            mx = jnp.max(g, axis=1, keepdims=True)
            am = jnp.min(jnp.where(g == mx, lane_f, float(LANES)), axis=1, keepdims=True)
            out = jnp.where(lane_f == float(r), am, out)
            g = jnp.where(lane_f == am, NEG_INF, g)
        idx_ref[...] = out.astype(jnp.int32)


def _moba_sample_gate(z3, cache_k, page_table, e, tables, n_heads, q_off):
    db, groups, hd = z3.shape
    n_pages = page_table.shape[1]
    n_blocks = n_pages // PAGES_PER_BLOCK
    assert n_pages % GATE_PAGES_PER_STEP == 0 and n_blocks <= LANES
    topk = min(MOBA_TOPK, n_blocks)
    steps = n_pages // GATE_PAGES_PER_STEP
    row = pl.BlockSpec((1, hd), lambda b, s, pt: (0, 0))

    def page(p):
        return pl.BlockSpec((None, None, PAGE_SIZE, n_heads, hd),
                            lambda b, s, pt, p=p: (e, pt[b * n_pages + s * GATE_PAGES_PER_STEP + p], 0, 0, 0))

    gate, idx = pl.pallas_call(
        functools.partial(_moba_gate_kernel, n_heads=n_heads, q_off=q_off, topk=topk),
        out_shape=(jax.ShapeDtypeStruct((db, n_heads, LANES), F32),
                   jax.ShapeDtypeStruct((db, n_heads, LANES), jnp.int32)),
        grid_spec=pltpu.PrefetchScalarGridSpec(
            num_scalar_prefetch=1,
            grid=(db, steps),
            in_specs=[pl.BlockSpec((None, groups, hd), lambda b, s, pt: (b, 0, 0)), row, row, row]
                     + [page(p) for p in range(GATE_PAGES_PER_STEP)],
            out_specs=(pl.BlockSpec((None, n_heads, LANES), lambda b, s, pt: (b, 0, 0)),
                       pl.BlockSpec((None, n_heads, LANES), lambda b, s, pt: (b, 0, 0)))),
        compiler_params=_params("arbitrary", "arbitrary"),
        name="moba_sample_gate",
    )(page_table.reshape(-1), z3, *tables, *([cache_k] * GATE_PAGES_PER_STEP))
    return idx[:, :, :topk]


def _moba_sample_attn_kernel(pg_ref, z_ref, c_ref, s1_ref, s2_ref, *rest, n_heads, n_sel, q_off, scale):
    k_pages, v_pages = rest[:n_sel], rest[n_sel:2 * n_sel]
    o_ref, kn_ref, vn_ref = rest[2 * n_sel:]
    h = pl.program_id(1)
    hd = HEAD_DIM
    rows = PAGE_SIZE * n_heads
    q = _moba_rope(z_ref[pl.ds(q_off + h, 1), :], c_ref[...], s1_ref[...], s2_ref[...])
    kn = _moba_rope(z_ref[pl.ds(q_off + n_heads + h, 1), :], c_ref[...], s1_ref[...], s2_ref[...])
    vn = z_ref[pl.ds(q_off + 2 * n_heads + h, 1), :]
    kn_ref[...] = kn
    vn_ref[...] = vn
    s_own = jnp.sum(q * kn, axis=1, keepdims=True) * scale

    q8 = jnp.broadcast_to(q, (SUBLANES, hd)).astype(BF16)
    col = lax.broadcasted_iota(jnp.int32, (1, rows), 1)
    mine = (col % n_heads) == h
    scores = []
    for kp in k_pages:
        sc = _dot_nt(q8, kp[...].reshape(rows, hd).astype(BF16))[0:1] * scale
        scores.append(jnp.where(mine, sc, NEG_INF))
    m = s_own
    for sc in scores:
        m = jnp.maximum(m, jnp.max(sc, axis=1, keepdims=True))
    l = jnp.exp(s_own - m)
    acc = l * vn
    for sc, vp in zip(scores, v_pages):
        p = jnp.exp(sc - m)
        l = l + jnp.sum(p, axis=1, keepdims=True)
        p8 = jnp.broadcast_to(p, (SUBLANES, rows)).astype(BF16)
        acc = acc + _dot(p8, vp[...].reshape(rows, hd).astype(BF16))[0:1]
    o_ref[...] = acc / l


def _moba_sample_attn(z3, cache_k, cache_v, pages, e, tables, n_heads, q_off):
    db, groups, hd = z3.shape
    n_sel = pages.shape[2]
    row = pl.BlockSpec((1, hd), lambda b, h, pg: (0, 0))

    def page_spec(s):
        return pl.BlockSpec((None, None, PAGE_SIZE, n_heads, hd),
                            lambda b, h, pg, s=s: (e, pg[(b * n_heads + h) * n_sel + s], 0, 0, 0))

    out = pl.BlockSpec((None, None, 1, hd), lambda b, h, pg: (b, h, 0, 0))
    shape = jax.ShapeDtypeStruct((db, n_heads, 1, hd), F32)
    return pl.pallas_call(
        functools.partial(_moba_sample_attn_kernel, n_heads=n_heads, n_sel=n_sel, q_off=q_off, scale=hd ** -0.5),
        out_shape=(shape, shape, shape),
        grid_spec=pltpu.PrefetchScalarGridSpec(
            num_scalar_prefetch=1,
            grid=(db, n_heads),
            in_specs=[pl.BlockSpec((None, groups, hd), lambda b, h, pg: (b, 0, 0)), row, row, row]
                     + [page_spec(s) for s in range(n_sel)] * 2,
            out_specs=(out, out, out)),
        compiler_params=_params("arbitrary", "arbitrary"),
        name="moba_sample_attn",
    )(pages.reshape(-1), z3, *tables, *([cache_k] * n_sel), *([cache_v] * n_sel))


RET_CHUNK = 256


def _ret_log_gamma(h, shape):
    hf = jnp.full(shape, h, jnp.int32).astype(F32)
    return jnp.log(1.0 - jnp.exp2(-5.0 - hf))


def _layer_norm_gate(o, gn, g):
    oc = o - jnp.mean(o, axis=1, keepdims=True)
    on = oc * lax.rsqrt(jnp.mean(oc * oc, axis=1, keepdims=True) + EPS) * gn
    return on * _silu(g)


RET_HEADS_PER_STEP = 2


def _ret_prompt_kernel(q_ref, k_ref, v_ref, g_ref, c_ref, s_ref, gn_ref, o_ref, so_ref, st_s, *, dk):
    c = pl.program_id(2)
    C = RET_CHUNK
    hp = RET_HEADS_PER_STEP

    @pl.when(c == 0)
    def _():
        st_s[...] = jnp.zeros_like(st_s)

    rows = pl.ds(pl.multiple_of(c * C, C), C)
    cos, sin = c_ref[rows, :], s_ref[rows, :]
    t_sq = lax.broadcasted_iota(jnp.int32, (C, C), 0)
    s_sq = lax.broadcasted_iota(jnp.int32, (C, C), 1)
    t_w = lax.broadcasted_iota(jnp.int32, (C, dk), 0).astype(F32)

    for hh in range(hp):
        ln = slice(hh * dk, (hh + 1) * dk)
        qr = _ret_rope(q_ref[:, ln], cos, sin)
        kr = _ret_rope(k_ref[:, ln], cos, sin) * (dk ** -0.5)
        vb = v_ref[:, ln].astype(BF16)
        lg = _ret_log_gamma(pl.program_id(1) * hp + hh, (1, 1))
        dmat = jnp.exp(jnp.where(s_sq <= t_sq, (t_sq - s_sq).astype(F32) * lg, NEG_INF))
        inter = jnp.exp((t_w + 1.0) * lg)
        state_dec = jnp.exp((C - 1.0 - t_w) * lg)
        chunk_dec = jnp.exp(float(C) * lg)

        st = st_s[hh]
        qb = qr.astype(BF16)
        att = _dot_nt(qb, kr.astype(BF16)) * dmat
        o = _dot(att.astype(BF16), vb) + _dot(qb, st.astype(BF16)) * inter
        st_s[hh] = chunk_dec * st + _dot_tn((kr * state_dec).astype(BF16), vb)
        o_ref[:, ln] = _layer_norm_gate(o, gn_ref[:, ln], g_ref[:, ln]).astype(o_ref.dtype)

    @pl.when(c == pl.num_programs(2) - 1)
    def _():
        so_ref[...] = st_s[...]


def _ret_prompt(z, tables, gn, r, n_heads):
    bsz, t, width = z.shape
    dk = width // (4 * n_heads)
    C = RET_CHUNK
    hp = RET_HEADS_PER_STEP
    assert t % C == 0 and n_heads % hp == 0
    ng = n_heads // hp
    cos, sin = tables
    full = pl.BlockSpec((t, dk // 2), lambda b, h, c: (0, 0))

    def col(off):
        return pl.BlockSpec((None, C, hp * dk), lambda b, h, c, off=off: (b, c, off * ng + h))

    return pl.pallas_call(
        functools.partial(_ret_prompt_kernel, dk=dk),
        out_shape=(jax.ShapeDtypeStruct((bsz, t, n_heads * dk), BF16),
                   jax.ShapeDtypeStruct((bsz, n_heads, dk, dk), F32)),
        grid=(bsz, ng, t // C),
        in_specs=[col(0), col(1), col(2), col(3), full, full,
                  pl.BlockSpec((None, 1, hp * dk), lambda b, h, c: (r, 0, h))],
        out_specs=(pl.BlockSpec((None, C, hp * dk), lambda b, h, c: (b, c, h)),
                   pl.BlockSpec((None, hp, dk, dk), lambda b, h, c: (b, h, 0, 0))),
        scratch_shapes=[pltpu.VMEM((hp, dk, dk), F32)],
        compiler_params=_params("arbitrary", "arbitrary", "arbitrary"),
        name="retention_prompt",
    )(z, z, z, z, cos, sin, gn.reshape(gn.shape[0], 1, -1))


def _ret_sample_kernel(z_ref, s0_ref, c_ref, s_ref, gn_ref, o_ref, so_ref, *, n_heads, dk):
    h = pl.program_id(1)
    cos, sin = c_ref[...], s_ref[...]
    qr = _ret_rope(z_ref[pl.ds(h, 1), :], cos, sin)
    kr = _ret_rope(z_ref[pl.ds(n_heads + h, 1), :], cos, sin) * (dk ** -0.5)
    v = z_ref[pl.ds(2 * n_heads + h, 1), :]
    g = z_ref[pl.ds(3 * n_heads + h, 1), :]
    gamma = jnp.exp(_ret_log_gamma(h, (1, 1)))
    s0 = s0_ref[...]
    qs = jnp.sum(_row_to_col(qr, dk) * s0, axis=0, keepdims=True)
    o = jnp.sum(qr * kr, axis=1, keepdims=True) * v + qs * gamma
    so_ref[...] = gamma * s0 + _row_to_col(kr, dk) * v
    o_ref[...] = _layer_norm_gate(o, gn_ref[pl.ds(h, 1), :], g)


def _ret_sample(z4, s0, tables, gn, r, n_heads):
    db, groups, dk = z4.shape
    cos, sin = tables
    row = pl.BlockSpec((1, dk // 2), lambda b, h: (0, 0))
    return pl.pallas_call(
        functools.partial(_ret_sample_kernel, n_heads=n_heads, dk=dk),
        out_shape=(jax.ShapeDtypeStruct((db, n_heads, 1, dk), F32),
                   jax.ShapeDtypeStruct((db, n_heads, dk, dk), F32)),
        grid=(db, n_heads),
        in_specs=[pl.BlockSpec((None, groups, dk), lambda b, h: (b, 0, 0)),
                  pl.BlockSpec((None, None, None, dk, dk), lambda b, h: (r, b, h, 0, 0)), row, row,
                  pl.BlockSpec((None, n_heads, dk), lambda b, h: (r, 0, 0))],
        out_specs=(pl.BlockSpec((None, None, 1, dk), lambda b, h: (b, h, 0, 0)),
                   pl.BlockSpec((None, None, dk, dk), lambda b, h: (b, h, 0, 0))),
        compiler_params=_params("arbitrary", "arbitrary"),
        name="retention_sample",
    )(z4, s0, cos, sin, gn)


def kernel(x_prompt, x_sample, cache_k, cache_v, page_table, state_hgrn, state_ret, state_conv,
           norm_mix, norm_ffn, norm_final, w_in_even, w_out_even, gn_hgrn, lb_hgrn,
           w_in_odd, w_out_odd, gn_ret, w_ffn_in, conv_w, conv_b, w_ffn_out):
    bsz, t, d = x_prompt.shape
    db, ts, _ = x_sample.shape
    assert ts == 1
    depth = norm_mix.shape[0]
    n_a = gn_hgrn.shape[1] // HEAD_DIM
    n_b = (w_in_even.shape[2] - 4 * gn_hgrn.shape[1]) // (3 * HEAD_DIM)
    assert n_b == SUBLANES
    n_pages = page_table.shape[1]
    assert n_pages % PAGES_PER_BLOCK == 0
    dk_c = w_in_odd.shape[2] // (4 * H_C)
    pos_p = np.arange(t)
    pos_s = np.array([n_pages * PAGE_SIZE])
    moba_tab_p, moba_tab_s = _moba_rope_tables(pos_p), _moba_rope_tables(pos_s)
    ret_tab_p, ret_tab_s = _ret_rope_tables(pos_p, dk_c), _ret_rope_tables(pos_s, dk_c)
    gn_ret3 = gn_ret.reshape(gn_ret.shape[0], H_C, dk_c)
    conv_buf = jnp.swapaxes(state_conv, 1, 2)
    mp = bsz * t

    xp = x_prompt.reshape(mp, d)
    xs = x_sample.reshape(db, d)
    k_p, v_p, h_p, r_p, c_p = [], [], [], [], []
    k_s, v_s, h_s, r_s, c_s = [], [], [], [], []
    for l in range(depth):
        hp_, hs_ = _rmsnorm(xp, norm_mix, l, BF16), _rmsnorm(xs, norm_mix, l, BF16)
        if l % 2 == 0:
            e = l // 2
            zp, zs = _mm([hp_], [hs_], w_in_even, e)
            zp = zp.reshape(bsz, t, -1)
            oa, st = _hgrn_prompt(zp, lb_hgrn, gn_hgrn, e, n_a)
            ob, k_rot, v_rows = _moba_prompt(zp, moba_tab_p, n_b, 4 * n_a)
            h_p.append(st)
            k_p.append(k_rot.reshape(bsz, t, n_b, HEAD_DIM))
            v_p.append(v_rows.reshape(bsz, t, n_b, HEAD_DIM))

            z3 = zs.reshape(db, -1, HEAD_DIM)
            oa_s, st = _hgrn_sample(z3, state_hgrn, lb_hgrn, gn_hgrn, e, n_a)
            idx = _moba_sample_gate(z3, cache_k, page_table, e, moba_tab_s, n_b, 4 * n_a)
            logical = idx[..., None] * PAGES_PER_BLOCK + jnp.arange(PAGES_PER_BLOCK, dtype=jnp.int32)
            pages = jnp.take_along_axis(page_table[:, None, :], logical.reshape(db, n_b, -1), axis=2)
            ob_s, k_rot, v_rows = _moba_sample_attn(z3, cache_k, cache_v, pages, e, moba_tab_s, n_b, 4 * n_a)
            h_s.append(st)
            k_s.append(k_rot.reshape(db, ts, n_b, HEAD_DIM))
            v_s.append(v_rows.reshape(db, ts, n_b, HEAD_DIM))

            xp, xs = _mm([oa.reshape(mp, -1), ob.reshape(mp, -1)],
                         [oa_s.reshape(db, -1).astype(BF16), ob_s.reshape(db, -1).astype(BF16)],
                         w_out_even, e, res=xp, res_s=xs)
        else:
            r = l // 2
            zp, zs = _mm([hp_], [hs_], w_in_odd, r)
            o, st = _ret_prompt(zp.reshape(bsz, t, -1), ret_tab_p, gn_ret, r, H_C)
            r_p.append(st)
            o_s, st = _ret_sample(zs.reshape(db, 4 * H_C, dk_c), state_ret, ret_tab_s, gn_ret3, r, H_C)
            r_s.append(st)
            xp, xs = _mm([o.reshape(mp, -1)], [o_s.reshape(db, -1).astype(BF16)], w_out_odd, r, res=xp, res_s=xs)
        hp_, hs_ = _rmsnorm(xp, norm_ffn, l, BF16), _rmsnorm(xs, norm_ffn, l, BF16)
        act, buf, act_s, buf_s = _ffn_in(hp_, hs_, conv_buf, w_ffn_in, conv_w, conv_b, l, bsz, t)
        c_p.append(buf)
        c_s.append(jnp.swapaxes(buf_s, 0, 1))
        xp, xs = _mm([act], [act_s], w_ffn_out, l, res=xp, res_s=xs)
    y_p = _rmsnorm(xp, norm_final.reshape(1, d), 0, F32).reshape(bsz, t, d)
    y_s = _rmsnorm(xs, norm_final.reshape(1, d), 0, F32).reshape(db, ts, d)
    st = jnp.stack
    return (y_p, y_s, st(k_p), st(v_p), st(k_s), st(v_s), st(h_p), st(h_s), st(r_p), st(r_s), st(c_p), st(c_s))
```

```python
import functools

import numpy as np
import jax
import jax.numpy as jnp
from jax import lax
from jax.experimental import pallas as pl
from jax.experimental.pallas import tpu as pltpu

F32 = jnp.float32
BF16 = jnp.bfloat16

HEAD_DIM = 128
MOBA_BLOCK = 256
MOBA_TOPK = 3
PAGE_SIZE = 128
PAGES_PER_BLOCK = MOBA_BLOCK // PAGE_SIZE
ROT_DIM = HEAD_DIM // 4
ROPE_THETA = 500000.0
RET_THETA = 10000.0
H_C = 8
SCAN_CHUNK = 64
SUB_CHUNK = 16
SUBLANES = 8
CONV_W = 3
EPS = 1e-6
LANES = 128
VMEM_LIMIT = 56 * 1024 * 1024
NEG_INF = float("-inf")
HIGHEST = lax.Precision.HIGHEST


def _params(*sem):
    return pltpu.CompilerParams(dimension_semantics=sem, vmem_limit_bytes=VMEM_LIMIT)


def _dot(a, b, precision=None):
    return jnp.dot(a, b, preferred_element_type=F32, precision=precision)


def _dot_nt(a, b, precision=None):
    return lax.dot_general(a, b, (((1,), (1,)), ((), ())), preferred_element_type=F32, precision=precision)


def _dot_tn(a, b, precision=None):
    return lax.dot_general(a, b, (((0,), (0,)), ((), ())), preferred_element_type=F32, precision=precision)


def _sigmoid(x):
    return 1.0 / (1.0 + jnp.exp(-x))


def _silu(x):
    return x * _sigmoid(x)


def _log_sigmoid(x):
    return jnp.minimum(x, 0.0) - jnp.log1p(jnp.exp(-jnp.abs(x)))


def _logaddexp(a, b):
    return jnp.maximum(a, b) + jnp.log1p(jnp.exp(-jnp.abs(a - b)))


def _row_to_col(row, n):
    r = lax.broadcasted_iota(jnp.int32, (n, n), 0)
    c = lax.broadcasted_iota(jnp.int32, (n, n), 1)
    return jnp.sum(jnp.where(r == c, jnp.broadcast_to(row, (n, n)), 0.0), axis=1, keepdims=True)


def _rmsnorm_kernel(x_ref, g_ref, o_ref):
    x = x_ref[...]
    y = x * lax.rsqrt(jnp.mean(x * x, axis=-1, keepdims=True) + EPS)
    o_ref[...] = (y * g_ref[...]).astype(o_ref.dtype)


def _rmsnorm(x, g, l, out_dtype):
    m, d = x.shape
    tm = min(m, 512)
    return pl.pallas_call(
        _rmsnorm_kernel,
        out_shape=jax.ShapeDtypeStruct((m, d), out_dtype),
        grid=(m // tm,),
        in_specs=[pl.BlockSpec((tm, d), lambda i: (i, 0)), pl.BlockSpec((None, 1, d), lambda i: (l, 0, 0))],
        out_specs=pl.BlockSpec((tm, d), lambda i: (i, 0)),
        compiler_params=_params("arbitrary"),
        name="rmsnorm",
    )(x, g.reshape(-1, 1, d))


def _split_dot(a_refs, wb_ref):
    kp = wb_ref.shape[0] // len(a_refs)
    acc = None
    for p, a_ref in enumerate(a_refs):
        part = _dot(a_ref[...], wb_ref[p * kp:(p + 1) * kp, :])
        acc = part if acc is None else acc + part
    return acc


def _mm_kernel(*refs, n_parts, has_res):
    a_refs, refs = refs[:n_parts], refs[n_parts:]
    as_refs, refs = refs[:n_parts], refs[n_parts:]
    w_ref, refs = refs[0], refs[1:]
    if has_res:
        r_ref, rs_ref, o_ref, os_ref, wb_ref = refs
    else:
        o_ref, os_ref, wb_ref = refs

    @pl.when(pl.program_id(1) == 0)
    def _():
        wb_ref[...] = w_ref[...].astype(BF16)
        acc_s = _split_dot(as_refs, wb_ref)
        if has_res:
            acc_s = rs_ref[...] + acc_s
        os_ref[...] = acc_s

    acc = _split_dot(a_refs, wb_ref)
    if has_res:
        acc = r_ref[...] + acc
    o_ref[...] = acc


def _mm_tiles(m, k, n):
    big = k <= 2048
    return min(m, 1024 if big else 512), min(n, 1024 if big else 512)


def _mm(a_parts, as_parts, w, l, res=None, res_s=None):
    n_parts = len(a_parts)
    m, kp = a_parts[0].shape
    ms = as_parts[0].shape[0]
    k, n = w.shape[1], w.shape[2]
    assert kp * n_parts == k
    tm, tn = _mm_tiles(m, k, n)
    assert m % tm == 0 and n % tn == 0
    in_specs = ([pl.BlockSpec((tm, kp), lambda j, i: (i, 0))] * n_parts
                + [pl.BlockSpec((ms, kp), lambda j, i: (0, 0))] * n_parts
                + [pl.BlockSpec((None, k, tn), lambda j, i: (l, 0, j))])
    args = [*a_parts, *as_parts, w]
    if res is not None:
        in_specs += [pl.BlockSpec((tm, tn), lambda j, i: (i, j)), pl.BlockSpec((ms, tn), lambda j, i: (0, j))]
        args += [res, res_s]
    return pl.pallas_call(
        functools.partial(_mm_kernel, n_parts=n_parts, has_res=res is not None),
        out_shape=(jax.ShapeDtypeStruct((m, n), F32), jax.ShapeDtypeStruct((ms, n), F32)),
        grid=(n // tn, m // tm),
        in_specs=in_specs,
        out_specs=(pl.BlockSpec((tm, tn), lambda j, i: (i, j)), pl.BlockSpec((ms, tn), lambda j, i: (0, j))),
        scratch_shapes=[pltpu.VMEM((k, tn), BF16)],
        compiler_params=_params("arbitrary", "arbitrary"),
        name="matmul",
    )(*args)


def _conv_gate(a, g, p0, p1, cw_ref, cb_ref):
    row = lax.broadcasted_iota(jnp.int32, a.shape, 0)
    a1 = jnp.where(row == 0, p1, pltpu.roll(a, 1, 0))
    a2 = jnp.where(row == 0, p0, jnp.where(row == 1, p1, pltpu.roll(a, 2, 0)))
    conv = cb_ref[...] + cw_ref[0:1, :] * a2
    conv = conv + cw_ref[1:2, :] * a1
    conv = conv + cw_ref[2:3, :] * a
    return _silu(conv) * g


def _ffn_in_kernel(h_ref, hs_ref, bufs_ref, wa_ref, wb_ref, cw_ref, cb_ref, o_ref, nb_ref, os_ref, nbs_ref,
                   wa_s, wb_s, carry_s, *, tiles_per_seq):
    i = pl.program_id(1)

    @pl.when(i == 0)
    def _():
        wa_s[...] = wa_ref[...].astype(BF16)
        wb_s[...] = wb_ref[...].astype(BF16)
        hs = hs_ref[...]
        a_s = _dot(hs, wa_s[...])
        conv = cb_ref[...] + cw_ref[0:1, :] * bufs_ref[0]
        conv = conv + cw_ref[1:2, :] * bufs_ref[1]
        conv = conv + cw_ref[2:3, :] * a_s
        os_ref[...] = (_silu(conv) * _dot(hs, wb_s[...])).astype(os_ref.dtype)
        nbs_ref[0] = bufs_ref[1]
        nbs_ref[1] = a_s

    h = h_ref[...]
    a = _dot(h, wa_s[...])
    g = _dot(h, wb_s[...])
    tm = a.shape[0]
    seq_start = (i % tiles_per_seq) == 0
    p0 = jnp.where(seq_start, 0.0, carry_s[SUBLANES - 2:SUBLANES - 1, :])
    p1 = jnp.where(seq_start, 0.0, carry_s[SUBLANES - 1:SUBLANES, :])
    o_ref[...] = _conv_gate(a, g, p0, p1, cw_ref, cb_ref).astype(o_ref.dtype)
    carry_s[...] = a[tm - SUBLANES:tm, :]

    @pl.when((i % tiles_per_seq) == tiles_per_seq - 1)
    def _():
        nb_ref[...] = a[tm - (CONV_W - 1):tm, :]


def _ffn_in(h, h_s, buf_s, w, cw, cb, l, bsz, t):
    m, k = h.shape
    db = h_s.shape[0]
    f = w.shape[2] // 2
    tm = min(t, 1024)
    tn = 512
    assert t % tm == 0 and f % tn == 0 and tm >= SUBLANES
    nf = f // tn
    tiles_per_seq = t // tm
    return pl.pallas_call(
        functools.partial(_ffn_in_kernel, tiles_per_seq=tiles_per_seq),
        out_shape=(jax.ShapeDtypeStruct((m, f), BF16),
                   jax.ShapeDtypeStruct((bsz, CONV_W - 1, f), F32),
                   jax.ShapeDtypeStruct((db, f), BF16),
                   jax.ShapeDtypeStruct((CONV_W - 1, db, f), F32)),
        grid=(nf, m // tm),
        in_specs=[pl.BlockSpec((tm, k), lambda j, i: (i, 0)),
                  pl.BlockSpec((db, k), lambda j, i: (0, 0)),
                  pl.BlockSpec((None, CONV_W - 1, db, tn), lambda j, i: (l, 0, 0, j)),
                  pl.BlockSpec((None, k, tn), lambda j, i: (l, 0, j)),
                  pl.BlockSpec((None, k, tn), lambda j, i: (l, 0, nf + j)),
                  pl.BlockSpec((None, CONV_W, tn), lambda j, i: (l, 0, j)),
                  pl.BlockSpec((None, 1, tn), lambda j, i: (l, 0, j))],
        out_specs=(pl.BlockSpec((tm, tn), lambda j, i: (i, j)),
                   pl.BlockSpec((None, CONV_W - 1, tn), lambda j, i: (i // tiles_per_seq, 0, j)),
                   pl.BlockSpec((db, tn), lambda j, i: (0, j)),
                   pl.BlockSpec((CONV_W - 1, db, tn), lambda j, i: (0, 0, j))),
        scratch_shapes=[pltpu.VMEM((k, tn), BF16), pltpu.VMEM((k, tn), BF16), pltpu.VMEM((SUBLANES, tn), F32)],
        compiler_params=_params("arbitrary", "arbitrary"),
        name="ffn_in_conv",
    )(h, h_s, buf_s, w, w, cw, cb.reshape(-1, 1, f))


HGRN_HEADS_PER_STEP = 8


def _hgrn_lower_bound(lbp, e):
    mx = jnp.max(lbp, axis=0, keepdims=True)
    ex = jnp.exp(lbp - mx)
    sm = ex / jnp.sum(ex, axis=0, keepdims=True)
    lb = jnp.zeros_like(sm[0:1])
    for l in range(1, e + 1):
        lb = lb + sm[l:l + 1]
    return lb


def _hgrn_gates(zf, zi, lb):
    logf = _logaddexp(jnp.log(lb), jnp.log1p(-lb) + _log_sigmoid(zf))
    return logf, 1.0 - jnp.exp(logf), _silu(zi)


def _hgrn_diag(qi, ki, vi, bi):
    half = SUBLANES
    row = lax.broadcasted_iota(jnp.int32, (half, HEAD_DIM), 0)
    q_lo, q_hi = qi[:half], qi[half:]
    b_lo, b_hi = bi[:half], bi[half:]
    o_lo = jnp.zeros((half, HEAD_DIM), F32)
    o_hi = jnp.zeros((half, HEAD_DIM), F32)
    for s in range(SUB_CHUNK):
        ks, vs, bs = ki[s:s + 1], vi[s:s + 1], bi[s:s + 1]
        if s < half:
            dec = jnp.exp(jnp.where(row >= s, b_lo - bs, NEG_INF))
            o_lo = o_lo + jnp.sum(q_lo * ks * dec, axis=1, keepdims=True) * vs
            dec = jnp.exp(b_hi - bs)
        else:
            dec = jnp.exp(jnp.where(row >= s - half, b_hi - bs, NEG_INF))
        o_hi = o_hi + jnp.sum(q_hi * ks * dec, axis=1, keepdims=True) * vs
    return jnp.concatenate([o_lo, o_hi], axis=0)


def _rel(x, b, i):
    r0 = i * SUB_CHUNK
    return x if i == 0 else x - b[r0 - 1:r0]


def _hgrn_intra_mxu(zq, kk, vv, b):
    C, cs = SCAN_CHUNK, SUB_CHUNK
    q_parts, k_parts = [], []
    for i in range(C // cs):
        r0, r1 = i * cs, (i + 1) * cs
        qt = (zq[r0:r1] * jnp.exp(_rel(b[r0:r1], b, i))).astype(BF16)
        kt = (kk[:r1] * jnp.exp(-_rel(b[:r1], b, i))).astype(BF16)
        q_rows = [qt]
        if r0 > 0:
            q_rows.insert(0, jnp.zeros((r0, HEAD_DIM), BF16))
        if r1 < C:
            q_rows.append(jnp.zeros((C - r1, HEAD_DIM), BF16))
            kt = jnp.concatenate([kt, jnp.zeros((C - r1, HEAD_DIM), BF16)], axis=0)
        q_parts.append(jnp.concatenate(q_rows, axis=0))
        k_parts.append(kt)
    return _dot_nt(jnp.concatenate(q_parts, axis=1), jnp.concatenate(k_parts, axis=1))


def _hgrn_intra_mxu_apply(att, vv):
    C = SCAN_CHUNK
    t_i = lax.broadcasted_iota(jnp.int32, (C, C), 0)
    s_i = lax.broadcasted_iota(jnp.int32, (C, C), 1)
    return _dot(jnp.where(s_i <= t_i, att, 0.0).astype(BF16), vv.astype(BF16))


def _hgrn_intra_exact(zq, kk, vv, b):
    cs = SUB_CHUNK
    parts = []
    for i in range(SCAN_CHUNK // cs):
        r0, r1 = i * cs, (i + 1) * cs
        qi, bi = zq[r0:r1], b[r0:r1]
        oi = _hgrn_diag(qi, kk[r0:r1], vv[r0:r1], bi)
        if i > 0:
            qt = qi * jnp.exp(_rel(bi, b, i))
            kt = kk[:r0] * jnp.exp(-_rel(b[:r0], b, i))
            att = _dot_nt(qt.astype(BF16), kt.astype(BF16))
            oi = oi + _dot(att.astype(BF16), vv[:r0].astype(BF16))
        parts.append(oi)
    return jnp.concatenate(parts, axis=0)


HGRN_SAFE_SUB_DECAY = -80.0


def _hgrn_prompt_kernel(q_ref, f_ref, i_ref, g_ref, lbp_ref, gn_ref, o_ref, s_ref, st_ref, kk_s, vv_s, b_s,
                        *, e, n_chunks):
    c = pl.program_id(2)
    C, cs = SCAN_CHUNK, SUB_CHUNK
    hp = HGRN_HEADS_PER_STEP
    hd = HEAD_DIM

    @pl.when(c == 0)
    def _():
        st_ref[...] = jnp.zeros_like(st_ref)

    lb = _hgrn_lower_bound(lbp_ref[...], e)
    gn = gn_ref[...]
    r_i = lax.broadcasted_iota(jnp.int32, (C, C), 0)
    c_i = lax.broadcasted_iota(jnp.int32, (C, C), 1)
    tri = (c_i <= r_i).astype(F32)

    def rows_of(ci):
        return pl.ds(pl.multiple_of(ci * C, C), C)

    def gates(ci, sub_decay):
        rows = rows_of(ci)
        logf, kk, vv = _hgrn_gates(f_ref[rows, :], i_ref[rows, :], lb)
        b = _dot(tri, logf, HIGHEST)
        kk_s[rows, :] = kk
        vv_s[rows, :] = vv
        b_s[rows, :] = b
        for i in range(C // cs):
            sub_decay = jnp.minimum(sub_decay, _rel(b[(i + 1) * cs - 1:(i + 1) * cs], b, i))
        return sub_decay

    sub_decay = lax.fori_loop(0, n_chunks, gates, jnp.zeros((1, hp * hd), F32))
    safe = jnp.min(sub_decay) >= HGRN_SAFE_SUB_DECAY

    def scan(intra_first, intra_second):
        def chunk(ci, carry):
            rows = rows_of(ci)
            zq, zg = q_ref[rows, :], g_ref[rows, :]
            kk_all, vv_all, b_all = kk_s[rows, :], vv_s[rows, :], b_s[rows, :]
            firsts = []
            for hh in range(hp):
                ln = slice(hh * hd, (hh + 1) * hd)
                q, kk, vv, b = zq[:, ln], kk_all[:, ln], vv_all[:, ln], b_all[:, ln]
                st = st_ref[hh]
                o_inter = _dot_nt((q * jnp.exp(b)).astype(BF16), st.astype(BF16))
                firsts.append((o_inter, intra_first(q, kk, vv, b)))
                bl = b[C - 1:C]
                khat = kk * jnp.exp(bl - b)
                st_ref[hh] = st * jnp.exp(bl) + _dot_tn(vv.astype(BF16), khat.astype(BF16))
            outs = []
            for hh, (o_inter, first) in enumerate(firsts):
                ln = slice(hh * hd, (hh + 1) * hd)
                o = o_inter + intra_second(first, vv_all[:, ln])
                on = o * lax.rsqrt(jnp.mean(o * o, axis=1, keepdims=True) + EPS) * gn[:, ln]
                outs.append(on * _sigmoid(zg[:, ln]))
            o_ref[rows, :] = jnp.concatenate(outs, axis=1).astype(o_ref.dtype)
            return carry
        lax.fori_loop(0, n_chunks, chunk, 0)

    @pl.when(safe)
    def _():
        scan(_hgrn_intra_mxu, _hgrn_intra_mxu_apply)

    @pl.when(jnp.logical_not(safe))
    def _():
        scan(_hgrn_intra_exact, lambda intra, vv: intra)

    @pl.when(c == pl.num_programs(2) - 1)
    def _():
        for hh in range(hp):
            s_ref[hh] = st_ref[hh].T


def _hgrn_prompt(z, lb_hgrn, gn, e, n_heads):
    bsz, t, _ = z.shape
    ct = min(t, 512)
    hp = HGRN_HEADS_PER_STEP
    assert t % ct == 0 and ct % SCAN_CHUNK == 0 and n_heads % hp == 0
    n_even = lb_hgrn.shape[0]
    hd = HEAD_DIM
    ng = n_heads // hp

    def col(off):
        return pl.BlockSpec((None, ct, hp * hd), lambda b, h, c, off=off: (b, c, off * ng + h))

    return pl.pallas_call(
        functools.partial(_hgrn_prompt_kernel, e=e, n_chunks=ct // SCAN_CHUNK),
        out_shape=(jax.ShapeDtypeStruct((bsz, t, n_heads * hd), BF16),
                   jax.ShapeDtypeStruct((bsz, n_heads, hd, hd), F32)),
        grid=(bsz, ng, t // ct),
        in_specs=[col(0), col(1), col(2), col(3),
                  pl.BlockSpec((n_even, hp * hd), lambda b, h, c: (0, h)),
                  pl.BlockSpec((None, 1, hp * hd), lambda b, h, c: (e, 0, h))],
        out_specs=(pl.BlockSpec((None, ct, hp * hd), lambda b, h, c: (b, c, h)),
                   pl.BlockSpec((None, hp, hd, hd), lambda b, h, c: (b, h, 0, 0))),
        scratch_shapes=[pltpu.VMEM((hp, hd, hd), F32)] + [pltpu.VMEM((ct, hp * hd), F32)] * 3,
        compiler_params=_params("arbitrary", "arbitrary", "arbitrary"),
        name="hgrn_prompt",
    )(z, z, z, z, lb_hgrn, gn.reshape(n_even, 1, -1))


def _hgrn_sample_kernel(z_ref, s0_ref, lbp_ref, gn_ref, o_ref, s_ref, *, e, n_heads):
    hd = HEAD_DIM
    lb_all = _hgrn_lower_bound(lbp_ref[...], e)
    for h in range(n_heads):
        lanes = slice(h * hd, (h + 1) * hd)
        zq = z_ref[h:h + 1, :]
        zg = z_ref[3 * n_heads + h:3 * n_heads + h + 1, :]
        logf, kk, vv = _hgrn_gates(z_ref[n_heads + h:n_heads + h + 1, :],
                                   z_ref[2 * n_heads + h:2 * n_heads + h + 1, :], lb_all[:, lanes])
        s_new = _row_to_col(jnp.exp(logf), hd) * s0_ref[h] + _row_to_col(kk, hd) * vv
        s_ref[h] = s_new
        o = jnp.sum(_row_to_col(zq, hd) * s_new, axis=0, keepdims=True)
        on = o * lax.rsqrt(jnp.mean(o * o, axis=1, keepdims=True) + EPS) * gn_ref[:, lanes]
        o_ref[h:h + 1, :] = on * _sigmoid(zg)


def _hgrn_sample(z3, s0, lb_hgrn, gn, e, n_heads):
    db, groups, hd = z3.shape
    n_even = lb_hgrn.shape[0]
    w = n_heads * hd
    return pl.pallas_call(
        functools.partial(_hgrn_sample_kernel, e=e, n_heads=n_heads),
        out_shape=(jax.ShapeDtypeStruct((db, n_heads, hd), F32),
                   jax.ShapeDtypeStruct((db, n_heads, hd, hd), F32)),
        grid=(db,),
        in_specs=[pl.BlockSpec((None, groups, hd), lambda b: (b, 0, 0)),
                  pl.BlockSpec((None, None, n_heads, hd, hd), lambda b: (e, b, 0, 0, 0)),
                  pl.BlockSpec((n_even, w), lambda b: (0, 0)),
                  pl.BlockSpec((None, 1, w), lambda b: (e, 0, 0))],
        out_specs=(pl.BlockSpec((None, n_heads, hd), lambda b: (b, 0, 0)),
                   pl.BlockSpec((None, n_heads, hd, hd), lambda b: (b, 0, 0, 0))),
        compiler_params=_params("arbitrary"),
        name="hgrn_sample",
    )(z3, s0, lb_hgrn, gn.reshape(n_even, 1, -1))


def _moba_rope_tables(pos):
    half = ROT_DIM // 2
    inv = ROPE_THETA ** (-np.arange(half, dtype=np.float64) / half)
    ang = np.asarray(pos, np.float64)[:, None] * inv[None, :]
    cos, sin = np.cos(ang), np.sin(ang)
    t = ang.shape[0]
    c = np.ones((t, HEAD_DIM))
    s1 = np.zeros((t, HEAD_DIM))
    s2 = np.zeros((t, HEAD_DIM))
    c[:, :half] = cos
    c[:, half:2 * half] = cos
    s1[:, :half] = -sin
    s2[:, half:2 * half] = sin
    return tuple(jnp.asarray(a, F32) for a in (c, s1, s2))


def _ret_rope_tables(pos, dk):
    half = dk // 2
    inv = RET_THETA ** (-np.arange(half, dtype=np.float64) / half)
    ang = np.asarray(pos, np.float64)[:, None] * inv[None, :]
    return jnp.asarray(np.cos(ang), F32), jnp.asarray(np.sin(ang), F32)


def _moba_rope(x, c, s1, s2):
    n = x.shape[-1]
    half = ROT_DIM // 2
    return x * c + pltpu.roll(x, n - half, 1) * s1 + pltpu.roll(x, half, 1) * s2


def _ret_rope(x, c, s):
    half = x.shape[-1] // 2
    x1, x2 = x[:, :half], x[:, half:]
    return jnp.concatenate([x1 * c - x2 * s, x2 * c + x1 * s], axis=1)


MOBA_MASK = -(2.0 ** 100)


def _moba_unselected(qr, km, n_past, topk):
    nbp = -(-n_past // SUBLANES) * SUBLANES
    gate = _dot_nt(km, qr, HIGHEST)[0:nbp]
    row = lax.broadcasted_iota(jnp.int32, gate.shape, 0)
    rank = jnp.zeros(gate.shape, F32)
    for jp in range(n_past):
        gj = gate[jp:jp + 1, :]
        rank = rank + jnp.where(gj > gate, 1.0, jnp.where(gj == gate, jnp.where(row > jp, 1.0, 0.0), 0.0))
    uns = jnp.where(row < n_past, jnp.where(rank < topk, 0.0, -1.0), 0.0)
    uns = jnp.concatenate([uns, jnp.zeros((LANES - nbp, uns.shape[1]), F32)], axis=0)
    return uns.T


MOBA_HEADS_PER_STEP = 2


def _moba_prompt_kernel(q_ref, k_ref, v_ref, c_ref, s1_ref, s2_ref, o_ref, ko_ref, vo_ref,
                        kb_s, vb_s, km_s, *, nb, topk, scale):
    i = pl.program_id(2)
    blk = MOBA_BLOCK
    hd = HEAD_DIM
    hp = MOBA_HEADS_PER_STEP

    @pl.when(i == 0)
    def _():
        km_s[...] = jnp.zeros_like(km_s)
        lane = lax.broadcasted_iota(jnp.int32, (blk, LANES), 1)
        for j in range(nb):
            rows = slice(j * blk, (j + 1) * blk)
            for hh in range(hp):
                ln = slice(hh * hd, (hh + 1) * hd)
                kr = _moba_rope(k_ref[rows, ln], c_ref[rows, :], s1_ref[rows, :], s2_ref[rows, :])
                ko_ref[rows, ln] = kr
                kb_s[hh, rows, 0:hd] = kr.astype(BF16)
                kb_s[hh, rows, hd:hd + LANES] = jnp.where(lane == j, -MOBA_MASK, 0.0).astype(BF16)
                km_s[hh, j:j + 1, :] = jnp.mean(kr, axis=0, keepdims=True)
                v = v_ref[rows, ln]
                vo_ref[rows, ln] = v
                vb_s[hh, rows, :] = v.astype(BF16)

    rows = pl.ds(pl.multiple_of(i * blk, blk), blk)
    cq, s1q, s2q = c_ref[rows, :], s1_ref[rows, :], s2_ref[rows, :]
    r_i = lax.broadcasted_iota(jnp.int32, (blk, blk), 0)
    c_i = lax.broadcasted_iota(jnp.int32, (blk, blk), 1)

    for ii in range(nb):
        @pl.when(i == ii)
        def _(ii=ii):
            ncol = (ii + 1) * blk
            scores = []
            for hh in range(hp):
                qr = _moba_rope(q_ref[:, hh * hd:(hh + 1) * hd], cq, s1q, s2q)
                qb = (qr * scale).astype(BF16)
                if ii > topk:
                    uns = _moba_unselected(qr, km_s[hh], ii, topk)
                    s = _dot_nt(jnp.concatenate([qb, uns.astype(BF16)], axis=1), kb_s[hh, 0:ncol, :])
                else:
                    s = _dot_nt(qb, kb_s[hh, 0:ncol, 0:hd])
                scores.append(s)
            for hh, s in enumerate(scores):
                pieces = [s[:, :ii * blk]] if ii > 0 else []
                pieces.append(jnp.where(c_i <= r_i, s[:, ii * blk:], NEG_INF))
                s = jnp.concatenate(pieces, axis=1)
                m = jnp.max(s, axis=1, keepdims=True)
                p = jnp.exp(s - m)
                l = jnp.sum(p, axis=1, keepdims=True)
                o = _dot(p.astype(BF16), vb_s[hh, 0:ncol, :]) / l
                o_ref[:, hh * hd:(hh + 1) * hd] = o.astype(o_ref.dtype)


def _moba_prompt(z, tables, n_heads, q_off):
    bsz, t, _ = z.shape
    hd = HEAD_DIM
    blk = MOBA_BLOCK
    hp = MOBA_HEADS_PER_STEP
    assert t % blk == 0 and n_heads % hp == 0 and q_off % hp == 0
    nb = t // blk
    assert nb <= LANES
    ng = n_heads // hp
    topk = max(1, min(MOBA_TOPK, nb - 1))
    c, s1, s2 = tables
    full = pl.BlockSpec((t, hd), lambda b, h, i: (0, 0))

    def seq(off):
        return pl.BlockSpec((None, t, hp * hd), lambda b, h, i, off=off: (b, 0, off // hp + h))

    return pl.pallas_call(
        functools.partial(_moba_prompt_kernel, nb=nb, topk=topk, scale=hd ** -0.5),
        out_shape=(jax.ShapeDtypeStruct((bsz, t, n_heads * hd), BF16),
                   jax.ShapeDtypeStruct((bsz, t, n_heads * hd), F32),
                   jax.ShapeDtypeStruct((bsz, t, n_heads * hd), F32)),
        grid=(bsz, ng, nb),
        in_specs=[pl.BlockSpec((None, blk, hp * hd), lambda b, h, i: (b, i, q_off // hp + h)),
                  seq(q_off + n_heads), seq(q_off + 2 * n_heads), full, full, full],
        out_specs=(pl.BlockSpec((None, blk, hp * hd), lambda b, h, i: (b, i, h)), seq(0), seq(0)),
        scratch_shapes=[pltpu.VMEM((hp, t, hd + LANES), BF16), pltpu.VMEM((hp, t, hd), BF16),
                        pltpu.VMEM((hp, LANES, hd), F32)],
        compiler_params=_params("arbitrary", "arbitrary", "arbitrary"),
        name="moba_prompt",
    )(z, z, z, c, s1, s2)


GATE_PAGES_PER_STEP = 16


def _moba_gate_kernel(pt_ref, z_ref, c_ref, s1_ref, s2_ref, *rest, n_heads, q_off, topk):
    pages = rest[:GATE_PAGES_PER_STEP]
    gate_ref, idx_ref = rest[GATE_PAGES_PER_STEP:]
    s = pl.program_id(1)
    bps = GATE_PAGES_PER_STEP // PAGES_PER_BLOCK

    @pl.when(s == 0)
    def _():
        gate_ref[...] = jnp.full(gate_ref.shape, NEG_INF, F32)

    q = _moba_rope(z_ref[q_off:q_off + n_heads, :], c_ref[...], s1_ref[...], s2_ref[...])
    lane = lax.broadcasted_iota(jnp.int32, gate_ref.shape, 1)
    for blk in range(bps):
        ksum = jnp.zeros_like(q)
        for p in range(PAGES_PER_BLOCK):
            ksum = ksum + jnp.sum(pages[blk * PAGES_PER_BLOCK + p][...], axis=0)
        val = jnp.sum(q * (ksum * (1.0 / MOBA_BLOCK)), axis=1, keepdims=True)
        gate_ref[...] = jnp.where(lane == s * bps + blk, val, gate_ref[...])

    @pl.when(s == pl.num_programs(1) - 1)
    def _():
        g = gate_ref[...]
        lane_f = lane.astype(F32)
        out = jnp.zeros(g.shape, F32)
        for r in range(topk):
            mx = jnp.max(g, axis=1, keepdims=True)
            am = jnp.min(jnp.where(g == mx, lane_f, float(LANES)), axis=1, keepdims=True)
            out = jnp.where(lane_f == float(r), am, out)
            g = jnp.where(lane_f == am, NEG_INF, g)
        idx_ref[...] = out.astype(jnp.int32)


def _moba_sample_gate(z3, cache_k, page_table, e, tables, n_heads, q_off):
    db, groups, hd = z3.shape
    n_pages = page_table.shape[1]
    n_blocks = n_pages // PAGES_PER_BLOCK
    assert n_pages % GATE_PAGES_PER_STEP == 0 and n_blocks <= LANES
    topk = min(MOBA_TOPK, n_blocks)
    steps = n_pages // GATE_PAGES_PER_STEP
    row = pl.BlockSpec((1, hd), lambda b, s, pt: (0, 0))

    def page(p):
        return pl.BlockSpec((None, None, PAGE_SIZE, n_heads, hd),
                            lambda b, s, pt, p=p: (e, pt[b * n_pages + s * GATE_PAGES_PER_STEP + p], 0, 0, 0))

    gate, idx = pl.pallas_call(
        functools.partial(_moba_gate_kernel, n_heads=n_heads, q_off=q_off, topk=topk),
        out_shape=(jax.ShapeDtypeStruct((db, n_heads, LANES), F32),
                   jax.ShapeDtypeStruct((db, n_heads, LANES), jnp.int32)),
        grid_spec=pltpu.PrefetchScalarGridSpec(
            num_scalar_prefetch=1,
            grid=(db, steps),
            in_specs=[pl.BlockSpec((None, groups, hd), lambda b, s, pt: (b, 0, 0)), row, row, row]
                     + [page(p) for p in range(GATE_PAGES_PER_STEP)],
            out_specs=(pl.BlockSpec((None, n_heads, LANES), lambda b, s, pt: (b, 0, 0)),
                       pl.BlockSpec((None, n_heads, LANES), lambda b, s, pt: (b, 0, 0)))),
        compiler_params=_params("arbitrary", "arbitrary"),
        name="moba_sample_gate",
    )(page_table.reshape(-1), z3, *tables, *([cache_k] * GATE_PAGES_PER_STEP))
    return idx[:, :, :topk]


def _moba_sample_attn_kernel(pg_ref, z_ref, c_ref, s1_ref, s2_ref, *rest, n_heads, n_sel, q_off, scale):
    k_pages, v_pages = rest[:n_sel], rest[n_sel:2 * n_sel]
    o_ref, kn_ref, vn_ref = rest[2 * n_sel:]
    h = pl.program_id(1)
    hd = HEAD_DIM
    rows = PAGE_SIZE * n_heads
    q = _moba_rope(z_ref[pl.ds(q_off + h, 1), :], c_ref[...], s1_ref[...], s2_ref[...])
    kn = _moba_rope(z_ref[pl.ds(q_off + n_heads + h, 1), :], c_ref[...], s1_ref[...], s2_ref[...])
    vn = z_ref[pl.ds(q_off + 2 * n_heads + h, 1), :]
    kn_ref[...] = kn
    vn_ref[...] = vn
    s_own = jnp.sum(q * kn, axis=1, keepdims=True) * scale

    q8 = jnp.broadcast_to(q, (SUBLANES, hd)).astype(BF16)
    col = lax.broadcasted_iota(jnp.int32, (1, rows), 1)
    mine = (col % n_heads) == h
    scores = []
    for kp in k_pages:
        sc = _dot_nt(q8, kp[...].reshape(rows, hd).astype(BF16))[0:1] * scale
        scores.append(jnp.where(mine, sc, NEG_INF))
    m = s_own
    for sc in scores:
        m = jnp.maximum(m, jnp.max(sc, axis=1, keepdims=True))
    l = jnp.exp(s_own - m)
    acc = l * vn
    for sc, vp in zip(scores, v_pages):
        p = jnp.exp(sc - m)
        l = l + jnp.sum(p, axis=1, keepdims=True)
        p8 = jnp.broadcast_to(p, (SUBLANES, rows)).astype(BF16)
        acc = acc + _dot(p8, vp[...].reshape(rows, hd).astype(BF16))[0:1]
    o_ref[...] = acc / l


def _moba_sample_attn(z3, cache_k, cache_v, pages, e, tables, n_heads, q_off):
    db, groups, hd = z3.shape
    n_sel = pages.shape[2]
    row = pl.BlockSpec((1, hd), lambda b, h, pg: (0, 0))

    def page_spec(s):
        return pl.BlockSpec((None, None, PAGE_SIZE, n_heads, hd),
                            lambda b, h, pg, s=s: (e, pg[(b * n_heads + h) * n_sel + s], 0, 0, 0))

    out = pl.BlockSpec((None, None, 1, hd), lambda b, h, pg: (b, h, 0, 0))
    shape = jax.ShapeDtypeStruct((db, n_heads, 1, hd), F32)
    return pl.pallas_call(
        functools.partial(_moba_sample_attn_kernel, n_heads=n_heads, n_sel=n_sel, q_off=q_off, scale=hd ** -0.5),
        out_shape=(shape, shape, shape),
        grid_spec=pltpu.PrefetchScalarGridSpec(
            num_scalar_prefetch=1,
            grid=(db, n_heads),
            in_specs=[pl.BlockSpec((None, groups, hd), lambda b, h, pg: (b, 0, 0)), row, row, row]
                     + [page_spec(s) for s in range(n_sel)] * 2,
            out_specs=(out, out, out)),
        compiler_params=_params("arbitrary", "arbitrary"),
        name="moba_sample_attn",
    )(pages.reshape(-1), z3, *tables, *([cache_k] * n_sel), *([cache_v] * n_sel))


RET_CHUNK = 256


def _ret_log_gamma(h, shape):
    hf = jnp.full(shape, h, jnp.int32).astype(F32)
    return jnp.log(1.0 - jnp.exp2(-5.0 - hf))


def _layer_norm_gate(o, gn, g):
    oc = o - jnp.mean(o, axis=1, keepdims=True)
    on = oc * lax.rsqrt(jnp.mean(oc * oc, axis=1, keepdims=True) + EPS) * gn
    return on * _silu(g)


RET_HEADS_PER_STEP = 2


def _ret_prompt_kernel(q_ref, k_ref, v_ref, g_ref, c_ref, s_ref, gn_ref, o_ref, so_ref, st_s, *, dk):
    c = pl.program_id(2)
    C = RET_CHUNK
    hp = RET_HEADS_PER_STEP

    @pl.when(c == 0)
    def _():
        st_s[...] = jnp.zeros_like(st_s)

    rows = pl.ds(pl.multiple_of(c * C, C), C)
    cos, sin = c_ref[rows, :], s_ref[rows, :]
    t_sq = lax.broadcasted_iota(jnp.int32, (C, C), 0)
    s_sq = lax.broadcasted_iota(jnp.int32, (C, C), 1)
    t_w = lax.broadcasted_iota(jnp.int32, (C, dk), 0).astype(F32)

    for hh in range(hp):
        ln = slice(hh * dk, (hh + 1) * dk)
        qr = _ret_rope(q_ref[:, ln], cos, sin)
        kr = _ret_rope(k_ref[:, ln], cos, sin) * (dk ** -0.5)
        vb = v_ref[:, ln].astype(BF16)
        lg = _ret_log_gamma(pl.program_id(1) * hp + hh, (1, 1))
        dmat = jnp.exp(jnp.where(s_sq <= t_sq, (t_sq - s_sq).astype(F32) * lg, NEG_INF))
        inter = jnp.exp((t_w + 1.0) * lg)
        state_dec = jnp.exp((C - 1.0 - t_w) * lg)
        chunk_dec = jnp.exp(float(C) * lg)

        st = st_s[hh]
        qb = qr.astype(BF16)
        att = _dot_nt(qb, kr.astype(BF16)) * dmat
        o = _dot(att.astype(BF16), vb) + _dot(qb, st.astype(BF16)) * inter
        st_s[hh] = chunk_dec * st + _dot_tn((kr * state_dec).astype(BF16), vb)
        o_ref[:, ln] = _layer_norm_gate(o, gn_ref[:, ln], g_ref[:, ln]).astype(o_ref.dtype)

    @pl.when(c == pl.num_programs(2) - 1)
    def _():
        so_ref[...] = st_s[...]


def _ret_prompt(z, tables, gn, r, n_heads):
    bsz, t, width = z.shape
    dk = width // (4 * n_heads)
    C = RET_CHUNK
    hp = RET_HEADS_PER_STEP
    assert t % C == 0 and n_heads % hp == 0
    ng = n_heads // hp
    cos, sin = tables
    full = pl.BlockSpec((t, dk // 2), lambda b, h, c: (0, 0))

    def col(off):
        return pl.BlockSpec((None, C, hp * dk), lambda b, h, c, off=off: (b, c, off * ng + h))

    return pl.pallas_call(
        functools.partial(_ret_prompt_kernel, dk=dk),
        out_shape=(jax.ShapeDtypeStruct((bsz, t, n_heads * dk), BF16),
                   jax.ShapeDtypeStruct((bsz, n_heads, dk, dk), F32)),
        grid=(bsz, ng, t // C),
        in_specs=[col(0), col(1), col(2), col(3), full, full,
                  pl.BlockSpec((None, 1, hp * dk), lambda b, h, c: (r, 0, h))],
        out_specs=(pl.BlockSpec((None, C, hp * dk), lambda b, h, c: (b, c, h)),
                   pl.BlockSpec((None, hp, dk, dk), lambda b, h, c: (b, h, 0, 0))),
        scratch_shapes=[pltpu.VMEM((hp, dk, dk), F32)],
        compiler_params=_params("arbitrary", "arbitrary", "arbitrary"),
        name="retention_prompt",
    )(z, z, z, z, cos, sin, gn.reshape(gn.shape[0], 1, -1))


def _ret_sample_kernel(z_ref, s0_ref, c_ref, s_ref, gn_ref, o_ref, so_ref, *, n_heads, dk):
    h = pl.program_id(1)
    cos, sin = c_ref[...], s_ref[...]
    qr = _ret_rope(z_ref[pl.ds(h, 1), :], cos, sin)
    kr = _ret_rope(z_ref[pl.ds(n_heads + h, 1), :], cos, sin) * (dk ** -0.5)
    v = z_ref[pl.ds(2 * n_heads + h, 1), :]
    g = z_ref[pl.ds(3 * n_heads + h, 1), :]
    gamma = jnp.exp(_ret_log_gamma(h, (1, 1)))
    s0 = s0_ref[...]
    qs = jnp.sum(_row_to_col(qr, dk) * s0, axis=0, keepdims=True)
    o = jnp.sum(qr * kr, axis=1, keepdims=True) * v + qs * gamma
    so_ref[...] = gamma * s0 + _row_to_col(kr, dk) * v
    o_ref[...] = _layer_norm_gate(o, gn_ref[pl.ds(h, 1), :], g)


def _ret_sample(z4, s0, tables, gn, r, n_heads):
    db, groups, dk = z4.shape
    cos, sin = tables
    row = pl.BlockSpec((1, dk // 2), lambda b, h: (0, 0))
    return pl.pallas_call(
        functools.partial(_ret_sample_kernel, n_heads=n_heads, dk=dk),
        out_shape=(jax.ShapeDtypeStruct((db, n_heads, 1, dk), F32),
                   jax.ShapeDtypeStruct((db, n_heads, dk, dk), F32)),
        grid=(db, n_heads),
        in_specs=[pl.BlockSpec((None, groups, dk), lambda b, h: (b, 0, 0)),
                  pl.BlockSpec((None, None, None, dk, dk), lambda b, h: (r, b, h, 0, 0)), row, row,
                  pl.BlockSpec((None, n_heads, dk), lambda b, h: (r, 0, 0))],
        out_specs=(pl.BlockSpec((None, None, 1, dk), lambda b, h: (b, h, 0, 0)),
                   pl.BlockSpec((None, None, dk, dk), lambda b, h: (b, h, 0, 0))),
        compiler_params=_params("arbitrary", "arbitrary"),
        name="retention_sample",
    )(z4, s0, cos, sin, gn)


def kernel(x_prompt, x_sample, cache_k, cache_v, page_table, state_hgrn, state_ret, state_conv,
           norm_mix, norm_ffn, norm_final, w_in_even, w_out_even, gn_hgrn, lb_hgrn,
           w_in_odd, w_out_odd, gn_ret, w_ffn_in, conv_w, conv_b, w_ffn_out):
    bsz, t, d = x_prompt.shape
    db, ts, _ = x_sample.shape
    assert ts == 1
    depth = norm_mix.shape[0]
    n_a = gn_hgrn.shape[1] // HEAD_DIM
    n_b = (w_in_even.shape[2] - 4 * gn_hgrn.shape[1]) // (3 * HEAD_DIM)
    assert n_b == SUBLANES
    n_pages = page_table.shape[1]
    assert n_pages % PAGES_PER_BLOCK == 0
    dk_c = w_in_odd.shape[2] // (4 * H_C)
    pos_p = np.arange(t)
    pos_s = np.array([n_pages * PAGE_SIZE])
    moba_tab_p, moba_tab_s = _moba_rope_tables(pos_p), _moba_rope_tables(pos_s)
    ret_tab_p, ret_tab_s = _ret_rope_tables(pos_p, dk_c), _ret_rope_tables(pos_s, dk_c)
    gn_ret3 = gn_ret.reshape(gn_ret.shape[0], H_C, dk_c)
    conv_buf = jnp.swapaxes(state_conv, 1, 2)
    mp = bsz * t

    xp = x_prompt.reshape(mp, d)
    xs = x_sample.reshape(db, d)
    k_p, v_p, h_p, r_p, c_p = [], [], [], [], []
    k_s, v_s, h_s, r_s, c_s = [], [], [], [], []
    for l in range(depth):
        hp_, hs_ = _rmsnorm(xp, norm_mix, l, BF16), _rmsnorm(xs, norm_mix, l, BF16)
        if l % 2 == 0:
            e = l // 2
            zp, zs = _mm([hp_], [hs_], w_in_even, e)
            zp = zp.reshape(bsz, t, -1)
            oa, st = _hgrn_prompt(zp, lb_hgrn, gn_hgrn, e, n_a)
            ob, k_rot, v_rows = _moba_prompt(zp, moba_tab_p, n_b, 4 * n_a)
            h_p.append(st)
            k_p.append(k_rot.reshape(bsz, t, n_b, HEAD_DIM))
            v_p.append(v_rows.reshape(bsz, t, n_b, HEAD_DIM))

            z3 = zs.reshape(db, -1, HEAD_DIM)
            oa_s, st = _hgrn_sample(z3, state_hgrn, lb_hgrn, gn_hgrn, e, n_a)
            idx = _moba_sample_gate(z3, cache_k, page_table, e, moba_tab_s, n_b, 4 * n_a)
            logical = idx[..., None] * PAGES_PER_BLOCK + jnp.arange(PAGES_PER_BLOCK, dtype=jnp.int32)
            pages = jnp.take_along_axis(page_table[:, None, :], logical.reshape(db, n_b, -1), axis=2)
            ob_s, k_rot, v_rows = _moba_sample_attn(z3, cache_k, cache_v, pages, e, moba_tab_s, n_b, 4 * n_a)
            h_s.append(st)
            k_s.append(k_rot.reshape(db, ts, n_b, HEAD_DIM))
            v_s.append(v_rows.reshape(db, ts, n_b, HEAD_DIM))

            xp, xs = _mm([oa.reshape(mp, -1), ob.reshape(mp, -1)],
                         [oa_s.reshape(db, -1).astype(BF16), ob_s.reshape(db, -1).astype(BF16)],
                         w_out_even, e, res=xp, res_s=xs)
        else:
            r = l // 2
            zp, zs = _mm([hp_], [hs_], w_in_odd, r)
            o, st = _ret_prompt(zp.reshape(bsz, t, -1), ret_tab_p, gn_ret, r, H_C)
            r_p.append(st)
            o_s, st = _ret_sample(zs.reshape(db, 4 * H_C, dk_c), state_ret, ret_tab_s, gn_ret3, r, H_C)
            r_s.append(st)
            xp, xs = _mm([o.reshape(mp, -1)], [o_s.reshape(db, -1).astype(BF16)], w_out_odd, r, res=xp, res_s=xs)
        hp_, hs_ = _rmsnorm(xp, norm_ffn, l, BF16), _rmsnorm(xs, norm_ffn, l, BF16)
        act, buf, act_s, buf_s = _ffn_in(hp_, hs_, conv_buf, w_ffn_in, conv_w, conv_b, l, bsz, t)
        c_p.append(buf)
        c_s.append(jnp.swapaxes(buf_s, 0, 1))
        xp, xs = _mm([act], [act_s], w_ffn_out, l, res=xp, res_s=xs)
    y_p = _rmsnorm(xp, norm_final.reshape(1, d), 0, F32).reshape(bsz, t, d)
    y_s = _rmsnorm(xs, norm_final.reshape(1, d), 0, F32).reshape(db, ts, d)
    st = jnp.stack
    return (y_p, y_s, st(k_p), st(v_p), st(k_s), st(v_s), st(h_p), st(h_s), st(r_p), st(r_s), st(c_p), st(c_s))
```

```python
import functools

import numpy as np
import jax
import jax.numpy as jnp
from jax import lax
from jax.experimental import pallas as pl
from jax.experimental.pallas import tpu as pltpu

F32 = jnp.float32
BF16 = jnp.bfloat16

HEAD_DIM = 128
MOBA_BLOCK = 256
MOBA_TOPK = 3
PAGE_SIZE = 128
PAGES_PER_BLOCK = MOBA_BLOCK // PAGE_SIZE
ROT_DIM = HEAD_DIM // 4
ROPE_THETA = 500000.0
RET_THETA = 10000.0
H_C = 8
SCAN_CHUNK = 64
SUB_CHUNK = 16
SUBLANES = 8
CONV_W = 3
EPS = 1e-6
LANES = 128
VMEM_LIMIT = 56 * 1024 * 1024
NEG_INF = float("-inf")
HIGHEST = lax.Precision.HIGHEST


def _params(*sem):
    return pltpu.CompilerParams(dimension_semantics=sem, vmem_limit_bytes=VMEM_LIMIT)


def _dot(a, b, precision=None):
    return jnp.dot(a, b, preferred_element_type=F32, precision=precision)


def _dot_nt(a, b, precision=None):
    return lax.dot_general(a, b, (((1,), (1,)), ((), ())), preferred_element_type=F32, precision=precision)


def _dot_tn(a, b, precision=None):
    return lax.dot_general(a, b, (((0,), (0,)), ((), ())), preferred_element_type=F32, precision=precision)


def _sigmoid(x):
    return 1.0 / (1.0 + jnp.exp(-x))


def _silu(x):
    return x * _sigmoid(x)


def _log_sigmoid(x):
    return jnp.minimum(x, 0.0) - jnp.log1p(jnp.exp(-jnp.abs(x)))


def _logaddexp(a, b):
    return jnp.maximum(a, b) + jnp.log1p(jnp.exp(-jnp.abs(a - b)))


def _row_to_col(row, n):
    r = lax.broadcasted_iota(jnp.int32, (n, n), 0)
    c = lax.broadcasted_iota(jnp.int32, (n, n), 1)
    return jnp.sum(jnp.where(r == c, jnp.broadcast_to(row, (n, n)), 0.0), axis=1, keepdims=True)


def _rmsnorm_kernel(x_ref, g_ref, o_ref):
    x = x_ref[...]
    y = x * lax.rsqrt(jnp.mean(x * x, axis=-1, keepdims=True) + EPS)
    o_ref[...] = (y * g_ref[...]).astype(o_ref.dtype)


def _rmsnorm(x, g, l, out_dtype):
    m, d = x.shape
    tm = min(m, 512)
    return pl.pallas_call(
        _rmsnorm_kernel,
        out_shape=jax.ShapeDtypeStruct((m, d), out_dtype),
        grid=(m // tm,),
        in_specs=[pl.BlockSpec((tm, d), lambda i: (i, 0)), pl.BlockSpec((None, 1, d), lambda i: (l, 0, 0))],
        out_specs=pl.BlockSpec((tm, d), lambda i: (i, 0)),
        compiler_params=_params("arbitrary"),
        name="rmsnorm",
    )(x, g.reshape(-1, 1, d))


def _split_dot(a_refs, wb_ref):
    kp = wb_ref.shape[0] // len(a_refs)
    acc = None
    for p, a_ref in enumerate(a_refs):
        part = _dot(a_ref[...], wb_ref[p * kp:(p + 1) * kp, :])
        acc = part if acc is None else acc + part
    return acc


def _lane_group_sum(x2):
    acc = x2[:, 0:LANES]
    for k in range(1, x2.shape[1] // LANES):
        acc = acc + x2[:, k * LANES:(k + 1) * LANES]
    return acc


def _row_scale(ssq_ref, d):
    total = jnp.sum(jnp.sum(ssq_ref[...], axis=0), axis=1, keepdims=True)
    return lax.rsqrt(total / d + EPS)


def _emit_norm_parts(x, g_ref, hg_ref, ssq_ref):
    hg_ref[...] = (x * g_ref[...]).astype(hg_ref.dtype)
    ssq_ref[...] = _lane_group_sum(x * x)


def _prep_kernel(x_ref, g_ref, hg_ref, ssq_ref):
    _emit_norm_parts(x_ref[...], g_ref, hg_ref, ssq_ref)


def _norm_parts(x, g, l):
    m, d = x.shape
    tm = min(m, 512)
    return pl.pallas_call(
        _prep_kernel,
        out_shape=(jax.ShapeDtypeStruct((m, d), BF16), jax.ShapeDtypeStruct((1, m, LANES), F32)),
        grid=(m // tm,),
        in_specs=[pl.BlockSpec((tm, d), lambda i: (i, 0)), pl.BlockSpec((None, 1, d), lambda i: (l, 0, 0))],
        out_specs=(pl.BlockSpec((tm, d), lambda i: (i, 0)), pl.BlockSpec((None, tm, LANES), lambda i: (0, i, 0))),
        compiler_params=_params("arbitrary"),
        name="norm_parts",
    )(x, g.reshape(-1, 1, d))


def _mm_kernel(*refs, n_parts, has_res, scaled, emit_norm):
    a_refs, refs = refs[:n_parts], refs[n_parts:]
    as_refs, refs = refs[:n_parts], refs[n_parts:]
    w_ref, refs = refs[0], refs[1:]
    if has_res:
        r_ref, rs_ref, refs = refs[0], refs[1], refs[2:]
    if scaled:
        q_ref, qs_ref, refs = refs[0], refs[1], refs[2:]
    if emit_norm:
        g_ref, refs = refs[0], refs[1:]
        o_ref, os_ref, hg_ref, hgs_ref, ssq_ref, ssqs_ref, wb_ref = refs
    else:
        o_ref, os_ref, wb_ref = refs
    d_in = wb_ref.shape[0]

    @pl.when(pl.program_id(1) == 0)
    def _():
        wb_ref[...] = w_ref[...].astype(BF16)
        acc_s = _split_dot(as_refs, wb_ref)
        if scaled:
            acc_s = acc_s * _row_scale(qs_ref, d_in)
        if has_res:
            acc_s = rs_ref[...] + acc_s
        os_ref[...] = acc_s
        if emit_norm:
            _emit_norm_parts(acc_s, g_ref, hgs_ref, ssqs_ref)

    acc = _split_dot(a_refs, wb_ref)
    if scaled:
        acc = acc * _row_scale(q_ref, d_in)
    if has_res:
        acc = r_ref[...] + acc
    o_ref[...] = acc
    if emit_norm:
        _emit_norm_parts(acc, g_ref, hg_ref, ssq_ref)


def _mm_tiles(m, k, n):
    big = k <= 2048
    return min(m, 1024 if big else 512), min(n, 1024 if big else 512)


def _mm(a_parts, as_parts, w, l, res=None, res_s=None, ssq=None, ssq_s=None, gain=None, gain_l=0):
    n_parts = len(a_parts)
    m, kp = a_parts[0].shape
    ms = as_parts[0].shape[0]
    k, n = w.shape[1], w.shape[2]
    assert kp * n_parts == k
    tm, tn = _mm_tiles(m, k, n)
    assert m % tm == 0 and n % tn == 0
    nt = n // tn
    in_specs = ([pl.BlockSpec((tm, kp), lambda j, i: (i, 0))] * n_parts
                + [pl.BlockSpec((ms, kp), lambda j, i: (0, 0))] * n_parts
                + [pl.BlockSpec((None, k, tn), lambda j, i: (l, 0, j))])
    args = [*a_parts, *as_parts, w]
    if res is not None:
        in_specs += [pl.BlockSpec((tm, tn), lambda j, i: (i, j)), pl.BlockSpec((ms, tn), lambda j, i: (0, j))]
        args += [res, res_s]
    if ssq is not None:
        in_specs += [pl.BlockSpec((ssq.shape[0], tm, LANES), lambda j, i: (0, i, 0)),
                     pl.BlockSpec(ssq_s.shape, lambda j, i: (0, 0, 0))]
        args += [ssq, ssq_s]
    out_shape = [jax.ShapeDtypeStruct((m, n), F32), jax.ShapeDtypeStruct((ms, n), F32)]
    out_specs = [pl.BlockSpec((tm, tn), lambda j, i: (i, j)), pl.BlockSpec((ms, tn), lambda j, i: (0, j))]
    if gain is not None:
        in_specs.append(pl.BlockSpec((None, 1, tn), lambda j, i: (gain_l, 0, j)))
        args.append(gain.reshape(-1, 1, n))
        out_shape += [jax.ShapeDtypeStruct((m, n), BF16), jax.ShapeDtypeStruct((ms, n), BF16),
                      jax.ShapeDtypeStruct((nt, m, LANES), F32), jax.ShapeDtypeStruct((nt, ms, LANES), F32)]
        out_specs += [pl.BlockSpec((tm, tn), lambda j, i: (i, j)), pl.BlockSpec((ms, tn), lambda j, i: (0, j)),
                      pl.BlockSpec((None, tm, LANES), lambda j, i: (j, i, 0)),
                      pl.BlockSpec((None, ms, LANES), lambda j, i: (j, 0, 0))]
    return pl.pallas_call(
        functools.partial(_mm_kernel, n_parts=n_parts, has_res=res is not None, scaled=ssq is not None,
                          emit_norm=gain is not None),
        out_shape=tuple(out_shape),
        grid=(nt, m // tm),
        in_specs=in_specs,
        out_specs=tuple(out_specs),
        scratch_shapes=[pltpu.VMEM((k, tn), BF16)],
        compiler_params=_params("arbitrary", "arbitrary"),
        name="matmul",
    )(*args)


def _conv_gate(a, g, p0, p1, cw_ref, cb_ref):
    row = lax.broadcasted_iota(jnp.int32, a.shape, 0)
    a1 = jnp.where(row == 0, p1, pltpu.roll(a, 1, 0))
    a2 = jnp.where(row == 0, p0, jnp.where(row == 1, p1, pltpu.roll(a, 2, 0)))
    conv = cb_ref[...] + cw_ref[0:1, :] * a2
    conv = conv + cw_ref[1:2, :] * a1
    conv = conv + cw_ref[2:3, :] * a
    return _silu(conv) * g


def _ffn_in_kernel(h_ref, hs_ref, q_ref, qs_ref, bufs_ref, wa_ref, wb_ref, cw_ref, cb_ref,
                   o_ref, nb_ref, os_ref, nbs_ref, wa_s, wb_s, carry_s, *, tiles_per_seq):
    i = pl.program_id(1)
    d_in = wa_s.shape[0]

    @pl.when(i == 0)
    def _():
        wa_s[...] = wa_ref[...].astype(BF16)
        wb_s[...] = wb_ref[...].astype(BF16)
        hs = hs_ref[...]
        rs_s = _row_scale(qs_ref, d_in)
        a_s = _dot(hs, wa_s[...]) * rs_s
        conv = cb_ref[...] + cw_ref[0:1, :] * bufs_ref[0]
        conv = conv + cw_ref[1:2, :] * bufs_ref[1]
        conv = conv + cw_ref[2:3, :] * a_s
        os_ref[...] = (_silu(conv) * (_dot(hs, wb_s[...]) * rs_s)).astype(os_ref.dtype)
        nbs_ref[0] = bufs_ref[1]
        nbs_ref[1] = a_s

    h = h_ref[...]
    rs = _row_scale(q_ref, d_in)
    a = _dot(h, wa_s[...]) * rs
    g = _dot(h, wb_s[...]) * rs
    tm = a.shape[0]
    seq_start = (i % tiles_per_seq) == 0
    p0 = jnp.where(seq_start, 0.0, carry_s[SUBLANES - 2:SUBLANES - 1, :])
    p1 = jnp.where(seq_start, 0.0, carry_s[SUBLANES - 1:SUBLANES, :])
    o_ref[...] = _conv_gate(a, g, p0, p1, cw_ref, cb_ref).astype(o_ref.dtype)
    carry_s[...] = a[tm - SUBLANES:tm, :]

    @pl.when((i % tiles_per_seq) == tiles_per_seq - 1)
    def _():
        nb_ref[...] = a[tm - (CONV_W - 1):tm, :]


def _ffn_in(h, h_s, ssq, ssq_s, buf_s, w, cw, cb, l, bsz, t):
    m, k = h.shape
    db = h_s.shape[0]
    f = w.shape[2] // 2
    tm = min(t, 1024)
    tn = 512
    assert t % tm == 0 and f % tn == 0 and tm >= SUBLANES
    nf = f // tn
    tiles_per_seq = t // tm
    return pl.pallas_call(
        functools.partial(_ffn_in_kernel, tiles_per_seq=tiles_per_seq),
        out_shape=(jax.ShapeDtypeStruct((m, f), BF16),
                   jax.ShapeDtypeStruct((bsz, CONV_W - 1, f), F32),
                   jax.ShapeDtypeStruct((db, f), BF16),
                   jax.ShapeDtypeStruct((CONV_W - 1, db, f), F32)),
        grid=(nf, m // tm),
        in_specs=[pl.BlockSpec((tm, k), lambda j, i: (i, 0)),
                  pl.BlockSpec((db, k), lambda j, i: (0, 0)),
                  pl.BlockSpec((ssq.shape[0], tm, LANES), lambda j, i: (0, i, 0)),
                  pl.BlockSpec(ssq_s.shape, lambda j, i: (0, 0, 0)),
                  pl.BlockSpec((None, CONV_W - 1, db, tn), lambda j, i: (l, 0, 0, j)),
                  pl.BlockSpec((None, k, tn), lambda j, i: (l, 0, j)),
                  pl.BlockSpec((None, k, tn), lambda j, i: (l, 0, nf + j)),
                  pl.BlockSpec((None, CONV_W, tn), lambda j, i: (l, 0, j)),
                  pl.BlockSpec((None, 1, tn), lambda j, i: (l, 0, j))],
        out_specs=(pl.BlockSpec((tm, tn), lambda j, i: (i, j)),
                   pl.BlockSpec((None, CONV_W - 1, tn), lambda j, i: (i // tiles_per_seq, 0, j)),
                   pl.BlockSpec((db, tn), lambda j, i: (0, j)),
                   pl.BlockSpec((CONV_W - 1, db, tn), lambda j, i: (0, 0, j))),
        scratch_shapes=[pltpu.VMEM((k, tn), BF16), pltpu.VMEM((k, tn), BF16), pltpu.VMEM((SUBLANES, tn), F32)],
        compiler_params=_params("arbitrary", "arbitrary"),
        name="ffn_in_conv",
    )(h, h_s, ssq, ssq_s, buf_s, w, w, cw, cb.reshape(-1, 1, f))


HGRN_HEADS_PER_STEP = 8


def _hgrn_lower_bound(lbp, e):
    mx = jnp.max(lbp, axis=0, keepdims=True)
    ex = jnp.exp(lbp - mx)
    sm = ex / jnp.sum(ex, axis=0, keepdims=True)
    lb = jnp.zeros_like(sm[0:1])
    for l in range(1, e + 1):
        lb = lb + sm[l:l + 1]
    return lb


def _hgrn_gates(zf, zi, lb):
    logf = _logaddexp(jnp.log(lb), jnp.log1p(-lb) + _log_sigmoid(zf))
    return logf, 1.0 - jnp.exp(logf), _silu(zi)


def _hgrn_diag(qi, ki, vi, bi):
    half = SUBLANES
    row = lax.broadcasted_iota(jnp.int32, (half, HEAD_DIM), 0)
    q_lo, q_hi = qi[:half], qi[half:]
    b_lo, b_hi = bi[:half], bi[half:]
    o_lo = jnp.zeros((half, HEAD_DIM), F32)
    o_hi = jnp.zeros((half, HEAD_DIM), F32)
    for s in range(SUB_CHUNK):
        ks, vs, bs = ki[s:s + 1], vi[s:s + 1], bi[s:s + 1]
        if s < half:
            dec = jnp.exp(jnp.where(row >= s, b_lo - bs, NEG_INF))
            o_lo = o_lo + jnp.sum(q_lo * ks * dec, axis=1, keepdims=True) * vs
            dec = jnp.exp(b_hi - bs)
        else:
            dec = jnp.exp(jnp.where(row >= s - half, b_hi - bs, NEG_INF))
        o_hi = o_hi + jnp.sum(q_hi * ks * dec, axis=1, keepdims=True) * vs
    return jnp.concatenate([o_lo, o_hi], axis=0)


def _rel(x, b, i):
    r0 = i * SUB_CHUNK
    return x if i == 0 else x - b[r0 - 1:r0]


def _hgrn_intra_mxu(zq, kk, vv, b):
    C, cs = SCAN_CHUNK, SUB_CHUNK
    q_parts, k_parts = [], []
    for i in range(C // cs):
        r0, r1 = i * cs, (i + 1) * cs
        qt = (zq[r0:r1] * jnp.exp(_rel(b[r0:r1], b, i))).astype(BF16)
        kt = (kk[:r1] * jnp.exp(-_rel(b[:r1], b, i))).astype(BF16)
        q_rows = [qt]
        if r0 > 0:
            q_rows.insert(0, jnp.zeros((r0, HEAD_DIM), BF16))
        if r1 < C:
            q_rows.append(jnp.zeros((C - r1, HEAD_DIM), BF16))
            kt = jnp.concatenate([kt, jnp.zeros((C - r1, HEAD_DIM), BF16)], axis=0)
        q_parts.append(jnp.concatenate(q_rows, axis=0))
        k_parts.append(kt)
    return _dot_nt(jnp.concatenate(q_parts, axis=1), jnp.concatenate(k_parts, axis=1))


def _hgrn_intra_mxu_apply(att, vv):
    C = SCAN_CHUNK
    t_i = lax.broadcasted_iota(jnp.int32, (C, C), 0)
    s_i = lax.broadcasted_iota(jnp.int32, (C, C), 1)
    return _dot(jnp.where(s_i <= t_i, att, 0.0).astype(BF16), vv.astype(BF16))


def _hgrn_intra_exact(zq, kk, vv, b):
    cs = SUB_CHUNK
    parts = []
    for i in range(SCAN_CHUNK // cs):
        r0, r1 = i * cs, (i + 1) * cs
        qi, bi = zq[r0:r1], b[r0:r1]
        oi = _hgrn_diag(qi, kk[r0:r1], vv[r0:r1], bi)
        if i > 0:
            qt = qi * jnp.exp(_rel(bi, b, i))
            kt = kk[:r0] * jnp.exp(-_rel(b[:r0], b, i))
            att = _dot_nt(qt.astype(BF16), kt.astype(BF16))
            oi = oi + _dot(att.astype(BF16), vv[:r0].astype(BF16))
        parts.append(oi)
    return jnp.concatenate(parts, axis=0)


HGRN_SAFE_SUB_DECAY = -80.0


def _hgrn_prompt_kernel(q_ref, f_ref, i_ref, g_ref, lbp_ref, gn_ref, o_ref, s_ref, st_ref, kk_s, vv_s, b_s,
                        *, e, n_chunks):
    c = pl.program_id(2)
    C, cs = SCAN_CHUNK, SUB_CHUNK
    hp = HGRN_HEADS_PER_STEP
    hd = HEAD_DIM

    @pl.when(c == 0)
    def _():
        st_ref[...] = jnp.zeros_like(st_ref)

    lb = _hgrn_lower_bound(lbp_ref[...], e)
    gn = gn_ref[...]
    r_i = lax.broadcasted_iota(jnp.int32, (C, C), 0)
    c_i = lax.broadcasted_iota(jnp.int32, (C, C), 1)
    tri = (c_i <= r_i).astype(F32)

    def rows_of(ci):
        return pl.ds(pl.multiple_of(ci * C, C), C)

    def gates(ci, sub_decay):
        rows = rows_of(ci)
        logf, kk, vv = _hgrn_gates(f_ref[rows, :], i_ref[rows, :], lb)
        b = _dot(tri, logf, HIGHEST)
        kk_s[rows, :] = kk
        vv_s[rows, :] = vv
        b_s[rows, :] = b
        for i in range(C // cs):
            sub_decay = jnp.minimum(sub_decay, _rel(b[(i + 1) * cs - 1:(i + 1) * cs], b, i))
        return sub_decay

    sub_decay = lax.fori_loop(0, n_chunks, gates, jnp.zeros((1, hp * hd), F32))
    safe = jnp.min(sub_decay) >= HGRN_SAFE_SUB_DECAY

    def scan(intra_first, intra_second):
        def chunk(ci, carry):
            rows = rows_of(ci)
            zq, zg = q_ref[rows, :], g_ref[rows, :]
            kk_all, vv_all, b_all = kk_s[rows, :], vv_s[rows, :], b_s[rows, :]
            firsts = []
            for hh in range(hp):
                ln = slice(hh * hd, (hh + 1) * hd)
                q, kk, vv, b = zq[:, ln], kk_all[:, ln], vv_all[:, ln], b_all[:, ln]
                st = st_ref[hh]
                o_inter = _dot_nt((q * jnp.exp(b)).astype(BF16), st.astype(BF16))
                firsts.append((o_inter, intra_first(q, kk, vv, b)))
                bl = b[C - 1:C]
                khat = kk * jnp.exp(bl - b)
                st_ref[hh] = st * jnp.exp(bl) + _dot_tn(vv.astype(BF16), khat.astype(BF16))
            outs = []
            for hh, (o_inter, first) in enumerate(firsts):
                ln = slice(hh * hd, (hh + 1) * hd)
                o = o_inter + intra_second(first, vv_all[:, ln])
                on = o * lax.rsqrt(jnp.mean(o * o, axis=1, keepdims=True) + EPS) * gn[:, ln]
                outs.append(on * _sigmoid(zg[:, ln]))
            o_ref[rows, :] = jnp.concatenate(outs, axis=1).astype(o_ref.dtype)
            return carry
        lax.fori_loop(0, n_chunks, chunk, 0)

    @pl.when(safe)
    def _():
        scan(_hgrn_intra_mxu, _hgrn_intra_mxu_apply)

    @pl.when(jnp.logical_not(safe))
    def _():
        scan(_hgrn_intra_exact, lambda intra, vv: intra)

    @pl.when(c == pl.num_programs(2) - 1)
    def _():
        for hh in range(hp):
            s_ref[hh] = st_ref[hh].T


def _hgrn_prompt(z, lb_hgrn, gn, e, n_heads):
    bsz, t, _ = z.shape
    ct = min(t, 512)
    hp = HGRN_HEADS_PER_STEP
    assert t % ct == 0 and ct % SCAN_CHUNK == 0 and n_heads % hp == 0
    n_even = lb_hgrn.shape[0]
    hd = HEAD_DIM
    ng = n_heads // hp

    def col(off):
        return pl.BlockSpec((None, ct, hp * hd), lambda b, h, c, off=off: (b, c, off * ng + h))

    return pl.pallas_call(
        functools.partial(_hgrn_prompt_kernel, e=e, n_chunks=ct // SCAN_CHUNK),
        out_shape=(jax.ShapeDtypeStruct((bsz, t, n_heads * hd), BF16),
                   jax.ShapeDtypeStruct((bsz, n_heads, hd, hd), F32)),
        grid=(bsz, ng, t // ct),
        in_specs=[col(0), col(1), col(2), col(3),
                  pl.BlockSpec((n_even, hp * hd), lambda b, h, c: (0, h)),
                  pl.BlockSpec((None, 1, hp * hd), lambda b, h, c: (e, 0, h))],
        out_specs=(pl.BlockSpec((None, ct, hp * hd), lambda b, h, c: (b, c, h)),
                   pl.BlockSpec((None, hp, hd, hd), lambda b, h, c: (b, h, 0, 0))),
        scratch_shapes=[pltpu.VMEM((hp, hd, hd), F32)] + [pltpu.VMEM((ct, hp * hd), F32)] * 3,
        compiler_params=_params("arbitrary", "arbitrary", "arbitrary"),
        name="hgrn_prompt",
    )(z, z, z, z, lb_hgrn, gn.reshape(n_even, 1, -1))


def _hgrn_sample_kernel(z_ref, s0_ref, lbp_ref, gn_ref, o_ref, s_ref, *, e, n_heads):
    hd = HEAD_DIM
    lb_all = _hgrn_lower_bound(lbp_ref[...], e)
    for h in range(n_heads):
        lanes = slice(h * hd, (h + 1) * hd)
        zq = z_ref[h:h + 1, :]
        zg = z_ref[3 * n_heads + h:3 * n_heads + h + 1, :]
        logf, kk, vv = _hgrn_gates(z_ref[n_heads + h:n_heads + h + 1, :],
                                   z_ref[2 * n_heads + h:2 * n_heads + h + 1, :], lb_all[:, lanes])
        s_new = _row_to_col(jnp.exp(logf), hd) * s0_ref[h] + _row_to_col(kk, hd) * vv
        s_ref[h] = s_new
        o = jnp.sum(_row_to_col(zq, hd) * s_new, axis=0, keepdims=True)
        on = o * lax.rsqrt(jnp.mean(o * o, axis=1, keepdims=True) + EPS) * gn_ref[:, lanes]
        o_ref[h:h + 1, :] = on * _sigmoid(zg)


def _hgrn_sample(z3, s0, lb_hgrn, gn, e, n_heads):
    db, groups, hd = z3.shape
    n_even = lb_hgrn.shape[0]
    w = n_heads * hd
    return pl.pallas_call(
        functools.partial(_hgrn_sample_kernel, e=e, n_heads=n_heads),
        out_shape=(jax.ShapeDtypeStruct((db, n_heads, hd), F32),
                   jax.ShapeDtypeStruct((db, n_heads, hd, hd), F32)),
        grid=(db,),
        in_specs=[pl.BlockSpec((None, groups, hd), lambda b: (b, 0, 0)),
                  pl.BlockSpec((None, None, n_heads, hd, hd), lambda b: (e, b, 0, 0, 0)),
                  pl.BlockSpec((n_even, w), lambda b: (0, 0)),
                  pl.BlockSpec((None, 1, w), lambda b: (e, 0, 0))],
        out_specs=(pl.BlockSpec((None, n_heads, hd), lambda b: (b, 0, 0)),
                   pl.BlockSpec((None, n_heads, hd, hd), lambda b: (b, 0, 0, 0))),
        compiler_params=_params("arbitrary"),
        name="hgrn_sample",
    )(z3, s0, lb_hgrn, gn.reshape(n_even, 1, -1))


def _moba_rope_tables(pos):
    half = ROT_DIM // 2
    inv = ROPE_THETA ** (-np.arange(half, dtype=np.float64) / half)
    ang = np.asarray(pos, np.float64)[:, None] * inv[None, :]
    cos, sin = np.cos(ang), np.sin(ang)
    t = ang.shape[0]
    c = np.ones((t, HEAD_DIM))
    s1 = np.zeros((t, HEAD_DIM))
    s2 = np.zeros((t, HEAD_DIM))
    c[:, :half] = cos
    c[:, half:2 * half] = cos
    s1[:, :half] = -sin
    s2[:, half:2 * half] = sin
    return tuple(jnp.asarray(a, F32) for a in (c, s1, s2))


def _ret_rope_tables(pos, dk):
    half = dk // 2
    inv = RET_THETA ** (-np.arange(half, dtype=np.float64) / half)
    ang = np.asarray(pos, np.float64)[:, None] * inv[None, :]
    return jnp.asarray(np.cos(ang), F32), jnp.asarray(np.sin(ang), F32)


def _moba_rope(x, c, s1, s2):
    n = x.shape[-1]
    half = ROT_DIM // 2
    return x * c + pltpu.roll(x, n - half, 1) * s1 + pltpu.roll(x, half, 1) * s2


def _ret_rope(x, c, s):
    half = x.shape[-1] // 2
    x1, x2 = x[:, :half], x[:, half:]
    return jnp.concatenate([x1 * c - x2 * s, x2 * c + x1 * s], axis=1)


MOBA_MASK = -(2.0 ** 100)


def _moba_unselected(qr, km, n_past, topk):
    nbp = -(-n_past // SUBLANES) * SUBLANES
    gate = _dot_nt(km, qr, HIGHEST)[0:nbp]
    row = lax.broadcasted_iota(jnp.int32, gate.shape, 0)
    rank = jnp.zeros(gate.shape, F32)
    for jp in range(n_past):
        gj = gate[jp:jp + 1, :]
        rank = rank + jnp.where(gj > gate, 1.0, jnp.where(gj == gate, jnp.where(row > jp, 1.0, 0.0), 0.0))
    uns = jnp.where(row < n_past, jnp.where(rank < topk, 0.0, -1.0), 0.0)
    uns = jnp.concatenate([uns, jnp.zeros((LANES - nbp, uns.shape[1]), F32)], axis=0)
    return uns.T


MOBA_HEADS_PER_STEP = 2


def _moba_prompt_kernel(q_ref, k_ref, v_ref, c_ref, s1_ref, s2_ref, o_ref, ko_ref, vo_ref,
                        kb_s, vb_s, km_s, *, nb, topk, scale):
    i = pl.program_id(2)
    blk = MOBA_BLOCK
    hd = HEAD_DIM
    hp = MOBA_HEADS_PER_STEP

    @pl.when(i == 0)
    def _():
        km_s[...] = jnp.zeros_like(km_s)
        lane = lax.broadcasted_iota(jnp.int32, (blk, LANES), 1)
        for j in range(nb):
            rows = slice(j * blk, (j + 1) * blk)
            for hh in range(hp):
                ln = slice(hh * hd, (hh + 1) * hd)
                kr = _moba_rope(k_ref[rows, ln], c_ref[rows, :], s1_ref[rows, :], s2_ref[rows, :])
                ko_ref[rows, ln] = kr
                kb_s[hh, rows, 0:hd] = kr.astype(BF16)
                kb_s[hh, rows, hd:hd + LANES] = jnp.where(lane == j, -MOBA_MASK, 0.0).astype(BF16)
                km_s[hh, j:j + 1, :] = jnp.mean(kr, axis=0, keepdims=True)
                v = v_ref[rows, ln]
                vo_ref[rows, ln] = v
                vb_s[hh, rows, :] = v.astype(BF16)

    rows = pl.ds(pl.multiple_of(i * blk, blk), blk)
    cq, s1q, s2q = c_ref[rows, :], s1_ref[rows, :], s2_ref[rows, :]
    r_i = lax.broadcasted_iota(jnp.int32, (blk, blk), 0)
    c_i = lax.broadcasted_iota(jnp.int32, (blk, blk), 1)

    for ii in range(nb):
        @pl.when(i == ii)
        def _(ii=ii):
            ncol = (ii + 1) * blk
            scores = []
            for hh in range(hp):
                qr = _moba_rope(q_ref[:, hh * hd:(hh + 1) * hd], cq, s1q, s2q)
                qb = (qr * scale).astype(BF16)
                if ii > topk:
                    uns = _moba_unselected(qr, km_s[hh], ii, topk)
                    s = _dot_nt(jnp.concatenate([qb, uns.astype(BF16)], axis=1), kb_s[hh, 0:ncol, :])
                else:
                    s = _dot_nt(qb, kb_s[hh, 0:ncol, 0:hd])
                scores.append(s)
            for hh, s in enumerate(scores):
                pieces = [s[:, :ii * blk]] if ii > 0 else []
                pieces.append(jnp.where(c_i <= r_i, s[:, ii * blk:], NEG_INF))
                s = jnp.concatenate(pieces, axis=1)
                m = jnp.max(s, axis=1, keepdims=True)
                p = jnp.exp(s - m)
                l = jnp.sum(p, axis=1, keepdims=True)
                o = _dot(p.astype(BF16), vb_s[hh, 0:ncol, :]) / l
                o_ref[:, hh * hd:(hh + 1) * hd] = o.astype(o_ref.dtype)


def _moba_prompt(z, tables, n_heads, q_off):
    bsz, t, _ = z.shape
    hd = HEAD_DIM
    blk = MOBA_BLOCK
    hp = MOBA_HEADS_PER_STEP
    assert t % blk == 0 and n_heads % hp == 0 and q_off % hp == 0
    nb = t // blk
    assert nb <= LANES
    ng = n_heads // hp
    topk = max(1, min(MOBA_TOPK, nb - 1))
    c, s1, s2 = tables
    full = pl.BlockSpec((t, hd), lambda b, h, i: (0, 0))

    def seq(off):
        return pl.BlockSpec((None, t, hp * hd), lambda b, h, i, off=off: (b, 0, off // hp + h))

    return pl.pallas_call(
        functools.partial(_moba_prompt_kernel, nb=nb, topk=topk, scale=hd ** -0.5),
        out_shape=(jax.ShapeDtypeStruct((bsz, t, n_heads * hd), BF16),
                   jax.ShapeDtypeStruct((bsz, t, n_heads * hd), F32),
                   jax.ShapeDtypeStruct((bsz, t, n_heads * hd), F32)),
        grid=(bsz, ng, nb),
        in_specs=[pl.BlockSpec((None, blk, hp * hd), lambda b, h, i: (b, i, q_off // hp + h)),
                  seq(q_off + n_heads), seq(q_off + 2 * n_heads), full, full, full],
        out_specs=(pl.BlockSpec((None, blk, hp * hd), lambda b, h, i: (b, i, h)), seq(0), seq(0)),
        scratch_shapes=[pltpu.VMEM((hp, t, hd + LANES), BF16), pltpu.VMEM((hp, t, hd), BF16),
                        pltpu.VMEM((hp, LANES, hd), F32)],
        compiler_params=_params("arbitrary", "arbitrary", "arbitrary"),
        name="moba_prompt",
    )(z, z, z, c, s1, s2)


GATE_PAGES_PER_STEP = 16


def _moba_gate_kernel(pt_ref, z_ref, c_ref, s1_ref, s2_ref, *rest, n_heads, q_off, topk):
    pages = rest[:GATE_PAGES_PER_STEP]
    gate_ref, idx_ref = rest[GATE_PAGES_PER_STEP:]
    s = pl.program_id(1)
    bps = GATE_PAGES_PER_STEP // PAGES_PER_BLOCK

    @pl.when(s == 0)
    def _():
        gate_ref[...] = jnp.full(gate_ref.shape, NEG_INF, F32)

    q = _moba_rope(z_ref[q_off:q_off + n_heads, :], c_ref[...], s1_ref[...], s2_ref[...])
    lane = lax.broadcasted_iota(jnp.int32, gate_ref.shape, 1)
    for blk in range(bps):
        ksum = jnp.zeros_like(q)
        for p in range(PAGES_PER_BLOCK):
            ksum = ksum + jnp.sum(pages[blk * PAGES_PER_BLOCK + p][...], axis=0)
        val = jnp.sum(q * (ksum * (1.0 / MOBA_BLOCK)), axis=1, keepdims=True)
        gate_ref[...] = jnp.where(lane == s * bps + blk, val, gate_ref[...])

    @pl.when(s == pl.num_programs(1) - 1)
    def _():
        g = gate_ref[...]
        lane_f = lane.astype(F32)
        out = jnp.zeros(g.shape, F32)
        for r in range(topk):
            mx = jnp.max(g, axis=1, keepdims=True)
            am = jnp.min(jnp.where(g == mx, lane_f, float(LANES)), axis=1, keepdims=True)
            out = jnp.where(lane_f == float(r), am, out)
            g = jnp.where(lane_f == am, NEG_INF, g)
        idx_ref[...] = out.astype(jnp.int32)


def _moba_sample_gate(z3, cache_k, page_table, e, tables, n_heads, q_off):
    db, groups, hd = z3.shape
    n_pages = page_table.shape[1]
    n_blocks = n_pages // PAGES_PER_BLOCK
    assert n_pages % GATE_PAGES_PER_STEP == 0 and n_blocks <= LANES
    topk = min(MOBA_TOPK, n_blocks)
    steps = n_pages // GATE_PAGES_PER_STEP
    row = pl.BlockSpec((1, hd), lambda b, s, pt: (0, 0))

    def page(p):
        return pl.BlockSpec((None, None, PAGE_SIZE, n_heads, hd),
                            lambda b, s, pt, p=p: (e, pt[b * n_pages + s * GATE_PAGES_PER_STEP + p], 0, 0, 0))

    gate, idx = pl.pallas_call(
        functools.partial(_moba_gate_kernel, n_heads=n_heads, q_off=q_off, topk=topk),
        out_shape=(jax.ShapeDtypeStruct((db, n_heads, LANES), F32),
                   jax.ShapeDtypeStruct((db, n_heads, LANES), jnp.int32)),
        grid_spec=pltpu.PrefetchScalarGridSpec(
            num_scalar_prefetch=1,
            grid=(db, steps),
            in_specs=[pl.BlockSpec((None, groups, hd), lambda b, s, pt: (b, 0, 0)), row, row, row]
                     + [page(p) for p in range(GATE_PAGES_PER_STEP)],
            out_specs=(pl.BlockSpec((None, n_heads, LANES), lambda b, s, pt: (b, 0, 0)),
                       pl.BlockSpec((None, n_heads, LANES), lambda b, s, pt: (b, 0, 0)))),
        compiler_params=_params("arbitrary", "arbitrary"),
        name="moba_sample_gate",
    )(page_table.reshape(-1), z3, *tables, *([cache_k] * GATE_PAGES_PER_STEP))
    return idx[:, :, :topk]


def _moba_sample_attn_kernel(pg_ref, z_ref, c_ref, s1_ref, s2_ref, *rest, n_heads, n_sel, q_off, scale):
    k_pages, v_pages = rest[:n_sel], rest[n_sel:2 * n_sel]
    o_ref, kn_ref, vn_ref = rest[2 * n_sel:]
    h = pl.program_id(1)
    hd = HEAD_DIM
    rows = PAGE_SIZE * n_heads
    q = _moba_rope(z_ref[pl.ds(q_off + h, 1), :], c_ref[...], s1_ref[...], s2_ref[...])
    kn = _moba_rope(z_ref[pl.ds(q_off + n_heads + h, 1), :], c_ref[...], s1_ref[...], s2_ref[...])
    vn = z_ref[pl.ds(q_off + 2 * n_heads + h, 1), :]
    kn_ref[...] = kn
    vn_ref[...] = vn
    s_own = jnp.sum(q * kn, axis=1, keepdims=True) * scale

    q8 = jnp.broadcast_to(q, (SUBLANES, hd)).astype(BF16)
    col = lax.broadcasted_iota(jnp.int32, (1, rows), 1)
    mine = (col % n_heads) == h
    scores = []
    for kp in k_pages:
        sc = _dot_nt(q8, kp[...].reshape(rows, hd).astype(BF16))[0:1] * scale
        scores.append(jnp.where(mine, sc, NEG_INF))
    m = s_own
    for sc in scores:
        m = jnp.maximum(m, jnp.max(sc, axis=1, keepdims=True))
    l = jnp.exp(s_own - m)
    acc = l * vn
    for sc, vp in zip(scores, v_pages):
        p = jnp.exp(sc - m)
        l = l + jnp.sum(p, axis=1, keepdims=True)
        p8 = jnp.broadcast_to(p, (SUBLANES, rows)).astype(BF16)
        acc = acc + _dot(p8, vp[...].reshape(rows, hd).astype(BF16))[0:1]
    o_ref[...] = acc / l


def _moba_sample_attn(z3, cache_k, cache_v, pages, e, tables, n_heads, q_off):
    db, groups, hd = z3.shape
    n_sel = pages.shape[2]
    row = pl.BlockSpec((1, hd), lambda b, h, pg: (0, 0))

    def page_spec(s):
        return pl.BlockSpec((None, None, PAGE_SIZE, n_heads, hd),
                            lambda b, h, pg, s=s: (e, pg[(b * n_heads + h) * n_sel + s], 0, 0, 0))

    out = pl.BlockSpec((None, None, 1, hd), lambda b, h, pg: (b, h, 0, 0))
    shape = jax.ShapeDtypeStruct((db, n_heads, 1, hd), F32)
    return pl.pallas_call(
        functools.partial(_moba_sample_attn_kernel, n_heads=n_heads, n_sel=n_sel, q_off=q_off, scale=hd ** -0.5),
        out_shape=(shape, shape, shape),
        grid_spec=pltpu.PrefetchScalarGridSpec(
            num_scalar_prefetch=1,
            grid=(db, n_heads),
            in_specs=[pl.BlockSpec((None, groups, hd), lambda b, h, pg: (b, 0, 0)), row, row, row]
                     + [page_spec(s) for s in range(n_sel)] * 2,
            out_specs=(out, out, out)),
        compiler_params=_params("arbitrary", "arbitrary"),
        name="moba_sample_attn",
    )(pages.reshape(-1), z3, *tables, *([cache_k] * n_sel), *([cache_v] * n_sel))


RET_CHUNK = 256


def _ret_log_gamma(h, shape):
    hf = jnp.full(shape, h, jnp.int32).astype(F32)
    return jnp.log(1.0 - jnp.exp2(-5.0 - hf))


def _layer_norm_gate(o, gn, g):
    oc = o - jnp.mean(o, axis=1, keepdims=True)
    on = oc * lax.rsqrt(jnp.mean(oc * oc, axis=1, keepdims=True) + EPS) * gn
    return on * _silu(g)


RET_HEADS_PER_STEP = 2


def _ret_prompt_kernel(q_ref, k_ref, v_ref, g_ref, c_ref, s_ref, gn_ref, o_ref, so_ref, st_s, *, dk):
    c = pl.program_id(2)
    C = RET_CHUNK
    hp = RET_HEADS_PER_STEP

    @pl.when(c == 0)
    def _():
        st_s[...] = jnp.zeros_like(st_s)

    rows = pl.ds(pl.multiple_of(c * C, C), C)
    cos, sin = c_ref[rows, :], s_ref[rows, :]
    t_sq = lax.broadcasted_iota(jnp.int32, (C, C), 0)
    s_sq = lax.broadcasted_iota(jnp.int32, (C, C), 1)
    t_w = lax.broadcasted_iota(jnp.int32, (C, dk), 0).astype(F32)

    for hh in range(hp):
        ln = slice(hh * dk, (hh + 1) * dk)
        qr = _ret_rope(q_ref[:, ln], cos, sin)
        kr = _ret_rope(k_ref[:, ln], cos, sin) * (dk ** -0.5)
        vb = v_ref[:, ln].astype(BF16)
        lg = _ret_log_gamma(pl.program_id(1) * hp + hh, (1, 1))
        dmat = jnp.exp(jnp.where(s_sq <= t_sq, (t_sq - s_sq).astype(F32) * lg, NEG_INF))
        inter = jnp.exp((t_w + 1.0) * lg)
        state_dec = jnp.exp((C - 1.0 - t_w) * lg)
        chunk_dec = jnp.exp(float(C) * lg)

        st = st_s[hh]
        qb = qr.astype(BF16)
        att = _dot_nt(qb, kr.astype(BF16)) * dmat
        o = _dot(att.astype(BF16), vb) + _dot(qb, st.astype(BF16)) * inter
        st_s[hh] = chunk_dec * st + _dot_tn((kr * state_dec).astype(BF16), vb)
        o_ref[:, ln] = _layer_norm_gate(o, gn_ref[:, ln], g_ref[:, ln]).astype(o_ref.dtype)

    @pl.when(c == pl.num_programs(2) - 1)
    def _():
        so_ref[...] = st_s[...]


def _ret_prompt(z, tables, gn, r, n_heads):
    bsz, t, width = z.shape
    dk = width // (4 * n_heads)
    C = RET_CHUNK
    hp = RET_HEADS_PER_STEP
    assert t % C == 0 and n_heads % hp == 0
    ng = n_heads // hp
    cos, sin = tables
    full = pl.BlockSpec((t, dk // 2), lambda b, h, c: (0, 0))

    def col(off):
        return pl.BlockSpec((None, C, hp * dk), lambda b, h, c, off=off: (b, c, off * ng + h))

    return pl.pallas_call(
        functools.partial(_ret_prompt_kernel, dk=dk),
        out_shape=(jax.ShapeDtypeStruct((bsz, t, n_heads * dk), BF16),
                   jax.ShapeDtypeStruct((bsz, n_heads, dk, dk), F32)),
        grid=(bsz, ng, t // C),
        in_specs=[col(0), col(1), col(2), col(3), full, full,
                  pl.BlockSpec((None, 1, hp * dk), lambda b, h, c: (r, 0, h))],
        out_specs=(pl.BlockSpec((None, C, hp * dk), lambda b, h, c: (b, c, h)),
                   pl.BlockSpec((None, hp, dk, dk), lambda b, h, c: (b, h, 0, 0))),
        scratch_shapes=[pltpu.VMEM((hp, dk, dk), F32)],
        compiler_params=_params("arbitrary", "arbitrary", "arbitrary"),
        name="retention_prompt",
    )(z, z, z, z, cos, sin, gn.reshape(gn.shape[0], 1, -1))


def _ret_sample_kernel(z_ref, s0_ref, c_ref, s_ref, gn_ref, o_ref, so_ref, *, n_heads, dk):
    h = pl.program_id(1)
    cos, sin = c_ref[...], s_ref[...]
    qr = _ret_rope(z_ref[pl.ds(h, 1), :], cos, sin)
    kr = _ret_rope(z_ref[pl.ds(n_heads + h, 1), :], cos, sin) * (dk ** -0.5)
    v = z_ref[pl.ds(2 * n_heads + h, 1), :]
    g = z_ref[pl.ds(3 * n_heads + h, 1), :]
    gamma = jnp.exp(_ret_log_gamma(h, (1, 1)))
    s0 = s0_ref[...]
    qs = jnp.sum(_row_to_col(qr, dk) * s0, axis=0, keepdims=True)
    o = jnp.sum(qr * kr, axis=1, keepdims=True) * v + qs * gamma
    so_ref[...] = gamma * s0 + _row_to_col(kr, dk) * v
    o_ref[...] = _layer_norm_gate(o, gn_ref[pl.ds(h, 1), :], g)


def _ret_sample(z4, s0, tables, gn, r, n_heads):
    db, groups, dk = z4.shape
    cos, sin = tables
    row = pl.BlockSpec((1, dk // 2), lambda b, h: (0, 0))
    return pl.pallas_call(
        functools.partial(_ret_sample_kernel, n_heads=n_heads, dk=dk),
        out_shape=(jax.ShapeDtypeStruct((db, n_heads, 1, dk), F32),
                   jax.ShapeDtypeStruct((db, n_heads, dk, dk), F32)),
        grid=(db, n_heads),
        in_specs=[pl.BlockSpec((None, groups, dk), lambda b, h: (b, 0, 0)),
                  pl.BlockSpec((None, None, None, dk, dk), lambda b, h: (r, b, h, 0, 0)), row, row,
                  pl.BlockSpec((None, n_heads, dk), lambda b, h: (r, 0, 0))],
        out_specs=(pl.BlockSpec((None, None, 1, dk), lambda b, h: (b, h, 0, 0)),
                   pl.BlockSpec((None, None, dk, dk), lambda b, h: (b, h, 0, 0))),
        compiler_params=_params("arbitrary", "arbitrary"),
        name="retention_sample",
    )(z4, s0, cos, sin, gn)


def kernel(x_prompt, x_sample, cache_k, cache_v, page_table, state_hgrn, state_ret, state_conv,
           norm_mix, norm_ffn, norm_final, w_in_even, w_out_even, gn_hgrn, lb_hgrn,
           w_in_odd, w_out_odd, gn_ret, w_ffn_in, conv_w, conv_b, w_ffn_out):
    bsz, t, d = x_prompt.shape
    db, ts, _ = x_sample.shape
    assert ts == 1
    depth = norm_mix.shape[0]
    n_a = gn_hgrn.shape[1] // HEAD_DIM
    n_b = (w_in_even.shape[2] - 4 * gn_hgrn.shape[1]) // (3 * HEAD_DIM)
    assert n_b == SUBLANES
    n_pages = page_table.shape[1]
    assert n_pages % PAGES_PER_BLOCK == 0
    dk_c = w_in_odd.shape[2] // (4 * H_C)
    pos_p = np.arange(t)
    pos_s = np.array([n_pages * PAGE_SIZE])
    moba_tab_p, moba_tab_s = _moba_rope_tables(pos_p), _moba_rope_tables(pos_s)
    ret_tab_p, ret_tab_s = _ret_rope_tables(pos_p, dk_c), _ret_rope_tables(pos_s, dk_c)
    gn_ret3 = gn_ret.reshape(gn_ret.shape[0], H_C, dk_c)
    conv_buf = jnp.swapaxes(state_conv, 1, 2)
    mp = bsz * t

    xp = x_prompt.reshape(mp, d)
    xs = x_sample.reshape(db, d)
    k_p, v_p, h_p, r_p, c_p = [], [], [], [], []
    k_s, v_s, h_s, r_s, c_s = [], [], [], [], []
    (hp_, qp), (hs_, qs) = _norm_parts(xp, norm_mix, 0), _norm_parts(xs, norm_mix, 0)
    for l in range(depth):
        if l % 2 == 0:
            e = l // 2
            zp, zs = _mm([hp_], [hs_], w_in_even, e, ssq=qp, ssq_s=qs)
            zp = zp.reshape(bsz, t, -1)
            oa, st = _hgrn_prompt(zp, lb_hgrn, gn_hgrn, e, n_a)
            ob, k_rot, v_rows = _moba_prompt(zp, moba_tab_p, n_b, 4 * n_a)
            h_p.append(st)
            k_p.append(k_rot.reshape(bsz, t, n_b, HEAD_DIM))
            v_p.append(v_rows.reshape(bsz, t, n_b, HEAD_DIM))

            z3 = zs.reshape(db, -1, HEAD_DIM)
            oa_s, st = _hgrn_sample(z3, state_hgrn, lb_hgrn, gn_hgrn, e, n_a)
            idx = _moba_sample_gate(z3, cache_k, page_table, e, moba_tab_s, n_b, 4 * n_a)
            logical = idx[..., None] * PAGES_PER_BLOCK + jnp.arange(PAGES_PER_BLOCK, dtype=jnp.int32)
            pages = jnp.take_along_axis(page_table[:, None, :], logical.reshape(db, n_b, -1), axis=2)
            ob_s, k_rot, v_rows = _moba_sample_attn(z3, cache_k, cache_v, pages, e, moba_tab_s, n_b, 4 * n_a)
            h_s.append(st)
            k_s.append(k_rot.reshape(db, ts, n_b, HEAD_DIM))
            v_s.append(v_rows.reshape(db, ts, n_b, HEAD_DIM))

            xp, xs, hp_, hs_, qp, qs = _mm(
                [oa.reshape(mp, -1), ob.reshape(mp, -1)],
                [oa_s.reshape(db, -1).astype(BF16), ob_s.reshape(db, -1).astype(BF16)],
                w_out_even, e, res=xp, res_s=xs, gain=norm_ffn, gain_l=l)
        else:
            r = l // 2
            zp, zs = _mm([hp_], [hs_], w_in_odd, r, ssq=qp, ssq_s=qs)
            o, st = _ret_prompt(zp.reshape(bsz, t, -1), ret_tab_p, gn_ret, r, H_C)
            r_p.append(st)
            o_s, st = _ret_sample(zs.reshape(db, 4 * H_C, dk_c), state_ret, ret_tab_s, gn_ret3, r, H_C)
            r_s.append(st)
            xp, xs, hp_, hs_, qp, qs = _mm([o.reshape(mp, -1)], [o_s.reshape(db, -1).astype(BF16)], w_out_odd, r,
                                           res=xp, res_s=xs, gain=norm_ffn, gain_l=l)
        act, buf, act_s, buf_s = _ffn_in(hp_, hs_, qp, qs, conv_buf, w_ffn_in, conv_w, conv_b, l, bsz, t)
        c_p.append(buf)
        c_s.append(jnp.swapaxes(buf_s, 0, 1))
        if l + 1 < depth:
            xp, xs, hp_, hs_, qp, qs = _mm([act], [act_s], w_ffn_out, l, res=xp, res_s=xs,
                                           gain=norm_mix, gain_l=l + 1)
        else:
            xp, xs = _mm([act], [act_s], w_ffn_out, l, res=xp, res_s=xs)
    y_p = _rmsnorm(xp, norm_final.reshape(1, d), 0, F32).reshape(bsz, t, d)
    y_s = _rmsnorm(xs, norm_final.reshape(1, d), 0, F32).reshape(db, ts, d)
    st = jnp.stack
    return (y_p, y_s, st(k_p), st(v_p), st(k_s), st(v_s), st(h_p), st(h_s), st(r_p), st(r_s), st(c_p), st(c_s))
```

```python
import functools

import numpy as np
import jax
import jax.numpy as jnp
from jax import lax
from jax.experimental import pallas as pl
from jax.experimental.pallas import tpu as pltpu

F32 = jnp.float32
BF16 = jnp.bfloat16

HEAD_DIM = 128
MOBA_BLOCK = 256
MOBA_TOPK = 3
PAGE_SIZE = 128
PAGES_PER_BLOCK = MOBA_BLOCK // PAGE_SIZE
ROT_DIM = HEAD_DIM // 4
ROPE_THETA = 500000.0
RET_THETA = 10000.0
H_C = 8
SCAN_CHUNK = 64
SUB_CHUNK = 16
SUBLANES = 8
CONV_W = 3
EPS = 1e-6
LANES = 128
VMEM_LIMIT = 56 * 1024 * 1024
NEG_INF = float("-inf")
HIGHEST = lax.Precision.HIGHEST


def _params(*sem):
    return pltpu.CompilerParams(dimension_semantics=sem, vmem_limit_bytes=VMEM_LIMIT)


def _dot(a, b, precision=None):
    return jnp.dot(a, b, preferred_element_type=F32, precision=precision)


def _dot_nt(a, b, precision=None):
    return lax.dot_general(a, b, (((1,), (1,)), ((), ())), preferred_element_type=F32, precision=precision)


def _dot_tn(a, b, precision=None):
    return lax.dot_general(a, b, (((0,), (0,)), ((), ())), preferred_element_type=F32, precision=precision)


def _sigmoid(x):
    return 1.0 / (1.0 + jnp.exp(-x))


def _silu(x):
    return x * _sigmoid(x)


def _log_sigmoid(x):
    return jnp.minimum(x, 0.0) - jnp.log1p(jnp.exp(-jnp.abs(x)))


def _logaddexp(a, b):
    return jnp.maximum(a, b) + jnp.log1p(jnp.exp(-jnp.abs(a - b)))


def _row_to_col(row, n):
    r = lax.broadcasted_iota(jnp.int32, (n, n), 0)
    c = lax.broadcasted_iota(jnp.int32, (n, n), 1)
    return jnp.sum(jnp.where(r == c, jnp.broadcast_to(row, (n, n)), 0.0), axis=1, keepdims=True)


def _rmsnorm_kernel(x_ref, g_ref, o_ref):
    x = x_ref[...]
    y = x * lax.rsqrt(jnp.mean(x * x, axis=-1, keepdims=True) + EPS)
    o_ref[...] = (y * g_ref[...]).astype(o_ref.dtype)


def _rmsnorm(x, g, l, out_dtype):
    m, d = x.shape
    tm = min(m, 512)
    return pl.pallas_call(
        _rmsnorm_kernel,
        out_shape=jax.ShapeDtypeStruct((m, d), out_dtype),
        grid=(m // tm,),
        in_specs=[pl.BlockSpec((tm, d), lambda i: (i, 0)), pl.BlockSpec((None, 1, d), lambda i: (l, 0, 0))],
        out_specs=pl.BlockSpec((tm, d), lambda i: (i, 0)),
        compiler_params=_params("arbitrary"),
        name="rmsnorm",
    )(x, g.reshape(-1, 1, d))


def _split_dot(a_refs, wb_ref):
    kp = wb_ref.shape[0] // len(a_refs)
    acc = None
    for p, a_ref in enumerate(a_refs):
        part = _dot(a_ref[...], wb_ref[p * kp:(p + 1) * kp, :])
        acc = part if acc is None else acc + part
    return acc


def _lane_group_sum(x2):
    acc = x2[:, 0:LANES]
    for k in range(1, x2.shape[1] // LANES):
        acc = acc + x2[:, k * LANES:(k + 1) * LANES]
    return acc


def _row_scale(ssq_ref, d):
    total = jnp.sum(jnp.sum(ssq_ref[...], axis=0), axis=1, keepdims=True)
    return lax.rsqrt(total / d + EPS)


def _emit_norm_parts(x, g_ref, hg_ref, ssq_ref):
    hg_ref[...] = (x * g_ref[...]).astype(hg_ref.dtype)
    ssq_ref[...] = _lane_group_sum(x * x)


def _prep_kernel(x_ref, g_ref, hg_ref, ssq_ref):
    _emit_norm_parts(x_ref[...], g_ref, hg_ref, ssq_ref)


def _norm_parts(x, g, l):
    m, d = x.shape
    tm = min(m, 512)
    return pl.pallas_call(
        _prep_kernel,
        out_shape=(jax.ShapeDtypeStruct((m, d), BF16), jax.ShapeDtypeStruct((1, m, LANES), F32)),
        grid=(m // tm,),
        in_specs=[pl.BlockSpec((tm, d), lambda i: (i, 0)), pl.BlockSpec((None, 1, d), lambda i: (l, 0, 0))],
        out_specs=(pl.BlockSpec((tm, d), lambda i: (i, 0)), pl.BlockSpec((None, tm, LANES), lambda i: (0, i, 0))),
        compiler_params=_params("arbitrary"),
        name="norm_parts",
    )(x, g.reshape(-1, 1, d))


def _mm_kernel(*refs, n_parts, has_res, scaled, emit_norm):
    a_refs, refs = refs[:n_parts], refs[n_parts:]
    as_refs, refs = refs[:n_parts], refs[n_parts:]
    w_ref, refs = refs[0], refs[1:]
    if has_res:
        r_ref, rs_ref, refs = refs[0], refs[1], refs[2:]
    if scaled:
        q_ref, qs_ref, refs = refs[0], refs[1], refs[2:]
    if emit_norm:
        g_ref, refs = refs[0], refs[1:]
        o_ref, os_ref, hg_ref, hgs_ref, ssq_ref, ssqs_ref, wb_ref = refs
    else:
        o_ref, os_ref, wb_ref = refs
    d_in = wb_ref.shape[0]

    @pl.when(pl.program_id(1) == 0)
    def _():
        wb_ref[...] = w_ref[...].astype(BF16)
        acc_s = _split_dot(as_refs, wb_ref)
        if scaled:
            acc_s = acc_s * _row_scale(qs_ref, d_in)
        if has_res:
            acc_s = rs_ref[...] + acc_s
        os_ref[...] = acc_s
        if emit_norm:
            _emit_norm_parts(acc_s, g_ref, hgs_ref, ssqs_ref)

    acc = _split_dot(a_refs, wb_ref)
    if scaled:
        acc = acc * _row_scale(q_ref, d_in)
    if has_res:
        acc = r_ref[...] + acc
    o_ref[...] = acc
    if emit_norm:
        _emit_norm_parts(acc, g_ref, hg_ref, ssq_ref)


def _mm_tiles(m, k, n):
    big = k <= 2048
    return min(m, 1024 if big else 512), min(n, 1024 if big else 512)


def _mm(a_parts, as_parts, w, l, res=None, res_s=None, ssq=None, ssq_s=None, gain=None, gain_l=0):
    n_parts = len(a_parts)
    m, kp = a_parts[0].shape
    ms = as_parts[0].shape[0]
    k, n = w.shape[1], w.shape[2]
    assert kp * n_parts == k
    tm, tn = _mm_tiles(m, k, n)
    assert m % tm == 0 and n % tn == 0
    nt = n // tn
    in_specs = ([pl.BlockSpec((tm, kp), lambda j, i: (i, 0))] * n_parts
                + [pl.BlockSpec((ms, kp), lambda j, i: (0, 0))] * n_parts
                + [pl.BlockSpec((None, k, tn), lambda j, i: (l, 0, j))])
    args = [*a_parts, *as_parts, w]
    if res is not None:
        in_specs += [pl.BlockSpec((tm, tn), lambda j, i: (i, j)), pl.BlockSpec((ms, tn), lambda j, i: (0, j))]
        args += [res, res_s]
    if ssq is not None:
        in_specs += [pl.BlockSpec((ssq.shape[0], tm, LANES), lambda j, i: (0, i, 0)),
                     pl.BlockSpec(ssq_s.shape, lambda j, i: (0, 0, 0))]
        args += [ssq, ssq_s]
    out_shape = [jax.ShapeDtypeStruct((m, n), F32), jax.ShapeDtypeStruct((ms, n), F32)]
    out_specs = [pl.BlockSpec((tm, tn), lambda j, i: (i, j)), pl.BlockSpec((ms, tn), lambda j, i: (0, j))]
    if gain is not None:
        in_specs.append(pl.BlockSpec((None, 1, tn), lambda j, i: (gain_l, 0, j)))
        args.append(gain.reshape(-1, 1, n))
        out_shape += [jax.ShapeDtypeStruct((m, n), BF16), jax.ShapeDtypeStruct((ms, n), BF16),
                      jax.ShapeDtypeStruct((nt, m, LANES), F32), jax.ShapeDtypeStruct((nt, ms, LANES), F32)]
        out_specs += [pl.BlockSpec((tm, tn), lambda j, i: (i, j)), pl.BlockSpec((ms, tn), lambda j, i: (0, j)),
                      pl.BlockSpec((None, tm, LANES), lambda j, i: (j, i, 0)),
                      pl.BlockSpec((None, ms, LANES), lambda j, i: (j, 0, 0))]
    return pl.pallas_call(
        functools.partial(_mm_kernel, n_parts=n_parts, has_res=res is not None, scaled=ssq is not None,
                          emit_norm=gain is not None),
        out_shape=tuple(out_shape),
        grid=(nt, m // tm),
        in_specs=in_specs,
        out_specs=tuple(out_specs),
        scratch_shapes=[pltpu.VMEM((k, tn), BF16)],
        compiler_params=_params("arbitrary", "arbitrary"),
        name="matmul",
    )(*args)


def _conv_gate(a, g, p0, p1, cw_ref, cb_ref):
    row = lax.broadcasted_iota(jnp.int32, a.shape, 0)
    a1 = jnp.where(row == 0, p1, pltpu.roll(a, 1, 0))
    a2 = jnp.where(row == 0, p0, jnp.where(row == 1, p1, pltpu.roll(a, 2, 0)))
    conv = cb_ref[...] + cw_ref[0:1, :] * a2
    conv = conv + cw_ref[1:2, :] * a1
    conv = conv + cw_ref[2:3, :] * a
    return _silu(conv) * g


def _ffn_in_kernel(h_ref, hs_ref, q_ref, qs_ref, bufs_ref, wa_ref, wb_ref, cw_ref, cb_ref,
                   o_ref, nb_ref, os_ref, nbs_ref, wa_s, wb_s, carry_s, *, tiles_per_seq):
    i = pl.program_id(1)
    d_in = wa_s.shape[0]

    @pl.when(i == 0)
    def _():
        wa_s[...] = wa_ref[...].astype(BF16)
        wb_s[...] = wb_ref[...].astype(BF16)
        hs = hs_ref[...]
        rs_s = _row_scale(qs_ref, d_in)
        a_s = _dot(hs, wa_s[...]) * rs_s
        conv = cb_ref[...] + cw_ref[0:1, :] * bufs_ref[0]
        conv = conv + cw_ref[1:2, :] * bufs_ref[1]
        conv = conv + cw_ref[2:3, :] * a_s
        os_ref[...] = (_silu(conv) * (_dot(hs, wb_s[...]) * rs_s)).astype(os_ref.dtype)
        nbs_ref[0] = bufs_ref[1]
        nbs_ref[1] = a_s

    h = h_ref[...]
    rs = _row_scale(q_ref, d_in)
    a = _dot(h, wa_s[...]) * rs
    g = _dot(h, wb_s[...]) * rs
    tm = a.shape[0]
    seq_start = (i % tiles_per_seq) == 0
    p0 = jnp.where(seq_start, 0.0, carry_s[SUBLANES - 2:SUBLANES - 1, :])
    p1 = jnp.where(seq_start, 0.0, carry_s[SUBLANES - 1:SUBLANES, :])
    o_ref[...] = _conv_gate(a, g, p0, p1, cw_ref, cb_ref).astype(o_ref.dtype)
    carry_s[...] = a[tm - SUBLANES:tm, :]

    @pl.when((i % tiles_per_seq) == tiles_per_seq - 1)
    def _():
        nb_ref[...] = a[tm - (CONV_W - 1):tm, :]


FFN_IN_ROWS = 1024


def _ffn_in(h, h_s, ssq, ssq_s, buf_s, w, cw, cb, l, bsz, t):
    m, k = h.shape
    db = h_s.shape[0]
    f = w.shape[2] // 2
    tm = min(t, FFN_IN_ROWS)
    tn = 512
    assert t % tm == 0 and f % tn == 0 and tm >= SUBLANES
    nf = f // tn
    tiles_per_seq = t // tm
    return pl.pallas_call(
        functools.partial(_ffn_in_kernel, tiles_per_seq=tiles_per_seq),
        out_shape=(jax.ShapeDtypeStruct((m, f), BF16),
                   jax.ShapeDtypeStruct((bsz, CONV_W - 1, f), F32),
                   jax.ShapeDtypeStruct((db, f), BF16),
                   jax.ShapeDtypeStruct((CONV_W - 1, db, f), F32)),
        grid=(nf, m // tm),
        in_specs=[pl.BlockSpec((tm, k), lambda j, i: (i, 0)),
                  pl.BlockSpec((db, k), lambda j, i: (0, 0)),
                  pl.BlockSpec((ssq.shape[0], tm, LANES), lambda j, i: (0, i, 0)),
                  pl.BlockSpec(ssq_s.shape, lambda j, i: (0, 0, 0)),
                  pl.BlockSpec((None, CONV_W - 1, db, tn), lambda j, i: (l, 0, 0, j)),
                  pl.BlockSpec((None, k, tn), lambda j, i: (l, 0, j)),
                  pl.BlockSpec((None, k, tn), lambda j, i: (l, 0, nf + j)),
                  pl.BlockSpec((None, CONV_W, tn), lambda j, i: (l, 0, j)),
                  pl.BlockSpec((None, 1, tn), lambda j, i: (l, 0, j))],
        out_specs=(pl.BlockSpec((tm, tn), lambda j, i: (i, j)),
                   pl.BlockSpec((None, CONV_W - 1, tn), lambda j, i: (i // tiles_per_seq, 0, j)),
                   pl.BlockSpec((db, tn), lambda j, i: (0, j)),
                   pl.BlockSpec((CONV_W - 1, db, tn), lambda j, i: (0, 0, j))),
        scratch_shapes=[pltpu.VMEM((k, tn), BF16), pltpu.VMEM((k, tn), BF16), pltpu.VMEM((SUBLANES, tn), F32)],
        compiler_params=_params("arbitrary", "arbitrary"),
        name="ffn_in_conv",
    )(h, h_s, ssq, ssq_s, buf_s, w, w, cw, cb.reshape(-1, 1, f))


HGRN_HEADS_PER_STEP = 8


def _hgrn_lower_bound(lbp, e):
    mx = jnp.max(lbp, axis=0, keepdims=True)
    ex = jnp.exp(lbp - mx)
    sm = ex / jnp.sum(ex, axis=0, keepdims=True)
    lb = jnp.zeros_like(sm[0:1])
    for l in range(1, e + 1):
        lb = lb + sm[l:l + 1]
    return lb


def _hgrn_gates(zf, zi, lb):
    logf = _logaddexp(jnp.log(lb), jnp.log1p(-lb) + _log_sigmoid(zf))
    return logf, 1.0 - jnp.exp(logf), _silu(zi)


def _hgrn_diag(qi, ki, vi, bi):
    half = SUBLANES
    row = lax.broadcasted_iota(jnp.int32, (half, HEAD_DIM), 0)
    q_lo, q_hi = qi[:half], qi[half:]
    b_lo, b_hi = bi[:half], bi[half:]
    o_lo = jnp.zeros((half, HEAD_DIM), F32)
    o_hi = jnp.zeros((half, HEAD_DIM), F32)
    for s in range(SUB_CHUNK):
        ks, vs, bs = ki[s:s + 1], vi[s:s + 1], bi[s:s + 1]
        if s < half:
            dec = jnp.exp(jnp.where(row >= s, b_lo - bs, NEG_INF))
            o_lo = o_lo + jnp.sum(q_lo * ks * dec, axis=1, keepdims=True) * vs
            dec = jnp.exp(b_hi - bs)
        else:
            dec = jnp.exp(jnp.where(row >= s - half, b_hi - bs, NEG_INF))
        o_hi = o_hi + jnp.sum(q_hi * ks * dec, axis=1, keepdims=True) * vs
    return jnp.concatenate([o_lo, o_hi], axis=0)


def _rel(x, b, i):
    r0 = i * SUB_CHUNK
    return x if i == 0 else x - b[r0 - 1:r0]


def _hgrn_intra_mxu(zq, kk, vv, b):
    C, cs = SCAN_CHUNK, SUB_CHUNK
    q_parts, k_parts = [], []
    for i in range(C // cs):
        r0, r1 = i * cs, (i + 1) * cs
        qt = (zq[r0:r1] * jnp.exp(_rel(b[r0:r1], b, i))).astype(BF16)
        kt = (kk[:r1] * jnp.exp(-_rel(b[:r1], b, i))).astype(BF16)
        q_rows = [qt]
        if r0 > 0:
            q_rows.insert(0, jnp.zeros((r0, HEAD_DIM), BF16))
        if r1 < C:
            q_rows.append(jnp.zeros((C - r1, HEAD_DIM), BF16))
            kt = jnp.concatenate([kt, jnp.zeros((C - r1, HEAD_DIM), BF16)], axis=0)
        q_parts.append(jnp.concatenate(q_rows, axis=0))
        k_parts.append(kt)
    return _dot_nt(jnp.concatenate(q_parts, axis=1), jnp.concatenate(k_parts, axis=1))


def _hgrn_intra_mxu_apply(att, vv):
    C = SCAN_CHUNK
    t_i = lax.broadcasted_iota(jnp.int32, (C, C), 0)
    s_i = lax.broadcasted_iota(jnp.int32, (C, C), 1)
    return _dot(jnp.where(s_i <= t_i, att, 0.0).astype(BF16), vv.astype(BF16))


def _hgrn_intra_exact(zq, kk, vv, b):
    cs = SUB_CHUNK
    parts = []
    for i in range(SCAN_CHUNK // cs):
        r0, r1 = i * cs, (i + 1) * cs
        qi, bi = zq[r0:r1], b[r0:r1]
        oi = _hgrn_diag(qi, kk[r0:r1], vv[r0:r1], bi)
        if i > 0:
            qt = qi * jnp.exp(_rel(bi, b, i))
            kt = kk[:r0] * jnp.exp(-_rel(b[:r0], b, i))
            att = _dot_nt(qt.astype(BF16), kt.astype(BF16))
            oi = oi + _dot(att.astype(BF16), vv[:r0].astype(BF16))
        parts.append(oi)
    return jnp.concatenate(parts, axis=0)


HGRN_SAFE_SUB_DECAY = -80.0


def _hgrn_prompt_kernel(q_ref, f_ref, i_ref, g_ref, lbp_ref, gn_ref, o_ref, s_ref, st_ref, kk_s, vv_s, b_s,
                        *, e, n_chunks):
    c = pl.program_id(2)
    C, cs = SCAN_CHUNK, SUB_CHUNK
    hp = HGRN_HEADS_PER_STEP
    hd = HEAD_DIM

    @pl.when(c == 0)
    def _():
        st_ref[...] = jnp.zeros_like(st_ref)

    lb = _hgrn_lower_bound(lbp_ref[...], e)
    gn = gn_ref[...]
    r_i = lax.broadcasted_iota(jnp.int32, (C, C), 0)
    c_i = lax.broadcasted_iota(jnp.int32, (C, C), 1)
    tri = (c_i <= r_i).astype(F32)

    def rows_of(ci):
        return pl.ds(pl.multiple_of(ci * C, C), C)

    def gates(ci, sub_decay):
        rows = rows_of(ci)
        logf, kk, vv = _hgrn_gates(f_ref[rows, :], i_ref[rows, :], lb)
        b = _dot(tri, logf, HIGHEST)
        kk_s[rows, :] = kk
        vv_s[rows, :] = vv
        b_s[rows, :] = b
        for i in range(C // cs):
            sub_decay = jnp.minimum(sub_decay, _rel(b[(i + 1) * cs - 1:(i + 1) * cs], b, i))
        return sub_decay

    sub_decay = lax.fori_loop(0, n_chunks, gates, jnp.zeros((1, hp * hd), F32))
    safe = jnp.min(sub_decay) >= HGRN_SAFE_SUB_DECAY

    def scan(intra_first, intra_second):
        def chunk(ci, carry):
            rows = rows_of(ci)
            zq, zg = q_ref[rows, :], g_ref[rows, :]
            kk_all, vv_all, b_all = kk_s[rows, :], vv_s[rows, :], b_s[rows, :]
            firsts = []
            for hh in range(hp):
                ln = slice(hh * hd, (hh + 1) * hd)
                q, kk, vv, b = zq[:, ln], kk_all[:, ln], vv_all[:, ln], b_all[:, ln]
                st = st_ref[hh]
                o_inter = _dot_nt((q * jnp.exp(b)).astype(BF16), st.astype(BF16))
                firsts.append((o_inter, intra_first(q, kk, vv, b)))
                bl = b[C - 1:C]
                khat = kk * jnp.exp(bl - b)
                st_ref[hh] = st * jnp.exp(bl) + _dot_tn(vv.astype(BF16), khat.astype(BF16))
            outs = []
            for hh, (o_inter, first) in enumerate(firsts):
                ln = slice(hh * hd, (hh + 1) * hd)
                o = o_inter + intra_second(first, vv_all[:, ln])
                on = o * lax.rsqrt(jnp.mean(o * o, axis=1, keepdims=True) + EPS) * gn[:, ln]
                outs.append(on * _sigmoid(zg[:, ln]))
            o_ref[rows, :] = jnp.concatenate(outs, axis=1).astype(o_ref.dtype)
            return carry
        lax.fori_loop(0, n_chunks, chunk, 0)

    @pl.when(safe)
    def _():
        scan(_hgrn_intra_mxu, _hgrn_intra_mxu_apply)

    @pl.when(jnp.logical_not(safe))
    def _():
        scan(_hgrn_intra_exact, lambda intra, vv: intra)

    @pl.when(c == pl.num_programs(2) - 1)
    def _():
        for hh in range(hp):
            s_ref[hh] = st_ref[hh].T


def _hgrn_prompt(z, lb_hgrn, gn, e, n_heads):
    bsz, t, _ = z.shape
    ct = min(t, 512)
    hp = HGRN_HEADS_PER_STEP
    assert t % ct == 0 and ct % SCAN_CHUNK == 0 and n_heads % hp == 0
    n_even = lb_hgrn.shape[0]
    hd = HEAD_DIM
    ng = n_heads // hp

    def col(off):
        return pl.BlockSpec((None, ct, hp * hd), lambda b, h, c, off=off: (b, c, off * ng + h))

    return pl.pallas_call(
        functools.partial(_hgrn_prompt_kernel, e=e, n_chunks=ct // SCAN_CHUNK),
        out_shape=(jax.ShapeDtypeStruct((bsz, t, n_heads * hd), BF16),
                   jax.ShapeDtypeStruct((bsz, n_heads, hd, hd), F32)),
        grid=(bsz, ng, t // ct),
        in_specs=[col(0), col(1), col(2), col(3),
                  pl.BlockSpec((n_even, hp * hd), lambda b, h, c: (0, h)),
                  pl.BlockSpec((None, 1, hp * hd), lambda b, h, c: (e, 0, h))],
        out_specs=(pl.BlockSpec((None, ct, hp * hd), lambda b, h, c: (b, c, h)),
                   pl.BlockSpec((None, hp, hd, hd), lambda b, h, c: (b, h, 0, 0))),
        scratch_shapes=[pltpu.VMEM((hp, hd, hd), F32)] + [pltpu.VMEM((ct, hp * hd), F32)] * 3,
        compiler_params=_params("arbitrary", "arbitrary", "arbitrary"),
        name="hgrn_prompt",
    )(z, z, z, z, lb_hgrn, gn.reshape(n_even, 1, -1))


def _hgrn_sample_kernel(z_ref, s0_ref, lbp_ref, gn_ref, o_ref, s_ref, *, e, n_heads):
    hd = HEAD_DIM
    lb_all = _hgrn_lower_bound(lbp_ref[...], e)
    for h in range(n_heads):
        lanes = slice(h * hd, (h + 1) * hd)
        zq = z_ref[h:h + 1, :]
        zg = z_ref[3 * n_heads + h:3 * n_heads + h + 1, :]
        logf, kk, vv = _hgrn_gates(z_ref[n_heads + h:n_heads + h + 1, :],
                                   z_ref[2 * n_heads + h:2 * n_heads + h + 1, :], lb_all[:, lanes])
        s_new = _row_to_col(jnp.exp(logf), hd) * s0_ref[h] + _row_to_col(kk, hd) * vv
        s_ref[h] = s_new
        o = jnp.sum(_row_to_col(zq, hd) * s_new, axis=0, keepdims=True)
        on = o * lax.rsqrt(jnp.mean(o * o, axis=1, keepdims=True) + EPS) * gn_ref[:, lanes]
        o_ref[h:h + 1, :] = on * _sigmoid(zg)


def _hgrn_sample(z3, s0, lb_hgrn, gn, e, n_heads):
    db, groups, hd = z3.shape
    n_even = lb_hgrn.shape[0]
    w = n_heads * hd
    return pl.pallas_call(
        functools.partial(_hgrn_sample_kernel, e=e, n_heads=n_heads),
        out_shape=(jax.ShapeDtypeStruct((db, n_heads, hd), F32),
                   jax.ShapeDtypeStruct((db, n_heads, hd, hd), F32)),
        grid=(db,),
        in_specs=[pl.BlockSpec((None, groups, hd), lambda b: (b, 0, 0)),
                  pl.BlockSpec((None, None, n_heads, hd, hd), lambda b: (e, b, 0, 0, 0)),
                  pl.BlockSpec((n_even, w), lambda b: (0, 0)),
                  pl.BlockSpec((None, 1, w), lambda b: (e, 0, 0))],
        out_specs=(pl.BlockSpec((None, n_heads, hd), lambda b: (b, 0, 0)),
                   pl.BlockSpec((None, n_heads, hd, hd), lambda b: (b, 0, 0, 0))),
        compiler_params=_params("arbitrary"),
        name="hgrn_sample",
    )(z3, s0, lb_hgrn, gn.reshape(n_even, 1, -1))


def _moba_rope_tables(pos):
    half = ROT_DIM // 2
    inv = ROPE_THETA ** (-np.arange(half, dtype=np.float64) / half)
    ang = np.asarray(pos, np.float64)[:, None] * inv[None, :]
    cos, sin = np.cos(ang), np.sin(ang)
    t = ang.shape[0]
    c = np.ones((t, HEAD_DIM))
    s1 = np.zeros((t, HEAD_DIM))
    s2 = np.zeros((t, HEAD_DIM))
    c[:, :half] = cos
    c[:, half:2 * half] = cos
    s1[:, :half] = -sin
    s2[:, half:2 * half] = sin
    return tuple(jnp.asarray(a, F32) for a in (c, s1, s2))


def _ret_rope_tables(pos, dk):
    half = dk // 2
    inv = RET_THETA ** (-np.arange(half, dtype=np.float64) / half)
    ang = np.asarray(pos, np.float64)[:, None] * inv[None, :]
    return jnp.asarray(np.cos(ang), F32), jnp.asarray(np.sin(ang), F32)


def _moba_rope(x, c, s1, s2):
    n = x.shape[-1]
    half = ROT_DIM // 2
    return x * c + pltpu.roll(x, n - half, 1) * s1 + pltpu.roll(x, half, 1) * s2


def _ret_rope(x, c, s):
    half = x.shape[-1] // 2
    x1, x2 = x[:, :half], x[:, half:]
    return jnp.concatenate([x1 * c - x2 * s, x2 * c + x1 * s], axis=1)


MOBA_MASK = -(2.0 ** 100)


def _moba_unselected(qr, km, n_past, topk):
    nbp = -(-n_past // SUBLANES) * SUBLANES
    km_hi, qr_hi = km.astype(BF16), qr.astype(BF16)
    km_lo, qr_lo = (km - km_hi.astype(F32)).astype(BF16), (qr - qr_hi.astype(F32)).astype(BF16)
    gate = (_dot_nt(km_hi, qr_hi) + _dot_nt(km_hi, qr_lo) + _dot_nt(km_lo, qr_hi))[0:nbp]
    row = lax.broadcasted_iota(jnp.int32, gate.shape, 0)
    rank = jnp.zeros(gate.shape, F32)
    for jp in range(n_past):
        gj = gate[jp:jp + 1, :]
        rank = rank + jnp.where(gj > gate, 1.0, jnp.where(gj == gate, jnp.where(row > jp, 1.0, 0.0), 0.0))
    uns = jnp.where(row < n_past, jnp.where(rank < topk, 0.0, -1.0), 0.0)
    uns = jnp.concatenate([uns, jnp.zeros((LANES - nbp, uns.shape[1]), F32)], axis=0)
    return uns.T


MOBA_HEADS_PER_STEP = 2


def _moba_prompt_kernel(q_ref, k_ref, v_ref, c_ref, s1_ref, s2_ref, o_ref, ko_ref, vo_ref,
                        kb_s, vb_s, km_s, *, nb, topk, scale):
    i = pl.program_id(2)
    blk = MOBA_BLOCK
    hd = HEAD_DIM
    hp = MOBA_HEADS_PER_STEP

    @pl.when(i == 0)
    def _():
        km_s[...] = jnp.zeros_like(km_s)
        lane = lax.broadcasted_iota(jnp.int32, (blk, LANES), 1)
        for j in range(nb):
            rows = slice(j * blk, (j + 1) * blk)
            for hh in range(hp):
                ln = slice(hh * hd, (hh + 1) * hd)
                kr = _moba_rope(k_ref[rows, ln], c_ref[rows, :], s1_ref[rows, :], s2_ref[rows, :])
                ko_ref[rows, ln] = kr
                kb_s[hh, rows, 0:hd] = kr.astype(BF16)
                kb_s[hh, rows, hd:hd + LANES] = jnp.where(lane == j, -MOBA_MASK, 0.0).astype(BF16)
                km_s[hh, j:j + 1, :] = jnp.mean(kr, axis=0, keepdims=True)
                v = v_ref[rows, ln]
                vo_ref[rows, ln] = v
                vb_s[hh, rows, :] = v.astype(BF16)

    rows = pl.ds(pl.multiple_of(i * blk, blk), blk)
    cq, s1q, s2q = c_ref[rows, :], s1_ref[rows, :], s2_ref[rows, :]
    r_i = lax.broadcasted_iota(jnp.int32, (blk, blk), 0)
    c_i = lax.broadcasted_iota(jnp.int32, (blk, blk), 1)

    for ii in range(nb):
        @pl.when(i == ii)
        def _(ii=ii):
            ncol = (ii + 1) * blk
            scores = []
            for hh in range(hp):
                qr = _moba_rope(q_ref[:, hh * hd:(hh + 1) * hd], cq, s1q, s2q)
                qb = (qr * scale).astype(BF16)
                if ii > topk:
                    uns = _moba_unselected(qr, km_s[hh], ii, topk)
                    s = _dot_nt(jnp.concatenate([qb, uns.astype(BF16)], axis=1), kb_s[hh, 0:ncol, :])
                else:
                    s = _dot_nt(qb, kb_s[hh, 0:ncol, 0:hd])
                scores.append(s)
            for hh, s in enumerate(scores):
                pieces = [s[:, :ii * blk]] if ii > 0 else []
                pieces.append(jnp.where(c_i <= r_i, s[:, ii * blk:], NEG_INF))
                s = jnp.concatenate(pieces, axis=1)
                m = jnp.max(s, axis=1, keepdims=True)
                p = jnp.exp(s - m)
                l = jnp.sum(p, axis=1, keepdims=True)
                o = _dot(p.astype(BF16), vb_s[hh, 0:ncol, :]) / l
                o_ref[:, hh * hd:(hh + 1) * hd] = o.astype(o_ref.dtype)


def _moba_prompt(z, tables, n_heads, q_off):
    bsz, t, _ = z.shape
    hd = HEAD_DIM
    blk = MOBA_BLOCK
    hp = MOBA_HEADS_PER_STEP
    assert t % blk == 0 and n_heads % hp == 0 and q_off % hp == 0
    nb = t // blk
    assert nb <= LANES
    ng = n_heads // hp
    topk = max(1, min(MOBA_TOPK, nb - 1))
    c, s1, s2 = tables
    full = pl.BlockSpec((t, hd), lambda b, h, i: (0, 0))

    def seq(off):
        return pl.BlockSpec((None, t, hp * hd), lambda b, h, i, off=off: (b, 0, off // hp + h))

    return pl.pallas_call(
        functools.partial(_moba_prompt_kernel, nb=nb, topk=topk, scale=hd ** -0.5),
        out_shape=(jax.ShapeDtypeStruct((bsz, t, n_heads * hd), BF16),
                   jax.ShapeDtypeStruct((bsz, t, n_heads * hd), F32),
                   jax.ShapeDtypeStruct((bsz, t, n_heads * hd), F32)),
        grid=(bsz, ng, nb),
        in_specs=[pl.BlockSpec((None, blk, hp * hd), lambda b, h, i: (b, i, q_off // hp + h)),
                  seq(q_off + n_heads), seq(q_off + 2 * n_heads), full, full, full],
        out_specs=(pl.BlockSpec((None, blk, hp * hd), lambda b, h, i: (b, i, h)), seq(0), seq(0)),
        scratch_shapes=[pltpu.VMEM((hp, t, hd + LANES), BF16), pltpu.VMEM((hp, t, hd), BF16),
                        pltpu.VMEM((hp, LANES, hd), F32)],
        compiler_params=_params("arbitrary", "arbitrary", "arbitrary"),
        name="moba_prompt",
    )(z, z, z, c, s1, s2)


GATE_PAGES_PER_STEP = 16


def _moba_gate_kernel(pt_ref, z_ref, c_ref, s1_ref, s2_ref, *rest, n_heads, q_off, topk):
    pages = rest[:GATE_PAGES_PER_STEP]
    gate_ref, idx_ref = rest[GATE_PAGES_PER_STEP:]
    s = pl.program_id(1)
    bps = GATE_PAGES_PER_STEP // PAGES_PER_BLOCK

    @pl.when(s == 0)
    def _():
        gate_ref[...] = jnp.full(gate_ref.shape, NEG_INF, F32)

    q = _moba_rope(z_ref[q_off:q_off + n_heads, :], c_ref[...], s1_ref[...], s2_ref[...])
    lane = lax.broadcasted_iota(jnp.int32, gate_ref.shape, 1)
    for blk in range(bps):
        ksum = jnp.zeros_like(q)
        for p in range(PAGES_PER_BLOCK):
            ksum = ksum + jnp.sum(pages[blk * PAGES_PER_BLOCK + p][...], axis=0)
        val = jnp.sum(q * (ksum * (1.0 / MOBA_BLOCK)), axis=1, keepdims=True)
        gate_ref[...] = jnp.where(lane == s * bps + blk, val, gate_ref[...])

    @pl.when(s == pl.num_programs(1) - 1)
    def _():
        g = gate_ref[...]
        lane_f = lane.astype(F32)
        out = jnp.zeros(g.shape, F32)
        for r in range(topk):
            mx = jnp.max(g, axis=1, keepdims=True)
            am = jnp.min(jnp.where(g == mx, lane_f, float(LANES)), axis=1, keepdims=True)
            out = jnp.where(lane_f == float(r), am, out)
            g = jnp.where(lane_f == am, NEG_INF, g)
        idx_ref[...] = out.astype(jnp.int32)


def _moba_sample_gate(z3, cache_k, page_table, e, tables, n_heads, q_off):
    db, groups, hd = z3.shape
    n_pages = page_table.shape[1]
    n_blocks = n_pages // PAGES_PER_BLOCK
    assert n_pages % GATE_PAGES_PER_STEP == 0 and n_blocks <= LANES
    topk = min(MOBA_TOPK, n_blocks)
    steps = n_pages // GATE_PAGES_PER_STEP
    row = pl.BlockSpec((1, hd), lambda b, s, pt: (0, 0))

    def page(p):
        return pl.BlockSpec((None, None, PAGE_SIZE, n_heads, hd),
                            lambda b, s, pt, p=p: (e, pt[b * n_pages + s * GATE_PAGES_PER_STEP + p], 0, 0, 0))

    gate, idx = pl.pallas_call(
        functools.partial(_moba_gate_kernel, n_heads=n_heads, q_off=q_off, topk=topk),
        out_shape=(jax.ShapeDtypeStruct((db, n_heads, LANES), F32),
                   jax.ShapeDtypeStruct((db, n_heads, LANES), jnp.int32)),
        grid_spec=pltpu.PrefetchScalarGridSpec(
            num_scalar_prefetch=1,
            grid=(db, steps),
            in_specs=[pl.BlockSpec((None, groups, hd), lambda b, s, pt: (b, 0, 0)), row, row, row]
                     + [page(p) for p in range(GATE_PAGES_PER_STEP)],
            out_specs=(pl.BlockSpec((None, n_heads, LANES), lambda b, s, pt: (b, 0, 0)),
                       pl.BlockSpec((None, n_heads, LANES), lambda b, s, pt: (b, 0, 0)))),
        compiler_params=_params("arbitrary", "arbitrary"),
        name="moba_sample_gate",
    )(page_table.reshape(-1), z3, *tables, *([cache_k] * GATE_PAGES_PER_STEP))
    return idx[:, :, :topk]


def _moba_sample_attn_kernel(pg_ref, z_ref, c_ref, s1_ref, s2_ref, *rest, n_heads, n_sel, q_off, scale):
    k_pages, v_pages = rest[:n_sel], rest[n_sel:2 * n_sel]
    o_ref, kn_ref, vn_ref = rest[2 * n_sel:]
    h = pl.program_id(1)
    hd = HEAD_DIM
    rows = PAGE_SIZE * n_heads
    q = _moba_rope(z_ref[pl.ds(q_off + h, 1), :], c_ref[...], s1_ref[...], s2_ref[...])
    kn = _moba_rope(z_ref[pl.ds(q_off + n_heads + h, 1), :], c_ref[...], s1_ref[...], s2_ref[...])
    vn = z_ref[pl.ds(q_off + 2 * n_heads + h, 1), :]
    kn_ref[...] = kn
    vn_ref[...] = vn
    s_own = jnp.sum(q * kn, axis=1, keepdims=True) * scale

    q8 = jnp.broadcast_to(q, (SUBLANES, hd)).astype(BF16)
    col = lax.broadcasted_iota(jnp.int32, (1, rows), 1)
    mine = (col % n_heads) == h
    scores = []
    for kp in k_pages:
        sc = _dot_nt(q8, kp[...].reshape(rows, hd).astype(BF16))[0:1] * scale
        scores.append(jnp.where(mine, sc, NEG_INF))
    m = s_own
    for sc in scores:
        m = jnp.maximum(m, jnp.max(sc, axis=1, keepdims=True))
    l = jnp.exp(s_own - m)
    acc = l * vn
    for sc, vp in zip(scores, v_pages):
        p = jnp.exp(sc - m)
        l = l + jnp.sum(p, axis=1, keepdims=True)
        p8 = jnp.broadcast_to(p, (SUBLANES, rows)).astype(BF16)
        acc = acc + _dot(p8, vp[...].reshape(rows, hd).astype(BF16))[0:1]
    o_ref[...] = acc / l


def _moba_sample_attn(z3, cache_k, cache_v, pages, e, tables, n_heads, q_off):
    db, groups, hd = z3.shape
    n_sel = pages.shape[2]
    row = pl.BlockSpec((1, hd), lambda b, h, pg: (0, 0))

    def page_spec(s):
        return pl.BlockSpec((None, None, PAGE_SIZE, n_heads, hd),
                            lambda b, h, pg, s=s: (e, pg[(b * n_heads + h) * n_sel + s], 0, 0, 0))

    out = pl.BlockSpec((None, None, 1, hd), lambda b, h, pg: (b, h, 0, 0))
    shape = jax.ShapeDtypeStruct((db, n_heads, 1, hd), F32)
    return pl.pallas_call(
        functools.partial(_moba_sample_attn_kernel, n_heads=n_heads, n_sel=n_sel, q_off=q_off, scale=hd ** -0.5),
        out_shape=(shape, shape, shape),
        grid_spec=pltpu.PrefetchScalarGridSpec(
            num_scalar_prefetch=1,
            grid=(db, n_heads),
            in_specs=[pl.BlockSpec((None, groups, hd), lambda b, h, pg: (b, 0, 0)), row, row, row]
                     + [page_spec(s) for s in range(n_sel)] * 2,
            out_specs=(out, out, out)),
        compiler_params=_params("arbitrary", "arbitrary"),
        name="moba_sample_attn",
    )(pages.reshape(-1), z3, *tables, *([cache_k] * n_sel), *([cache_v] * n_sel))


RET_CHUNK = 256


def _ret_log_gamma(h, shape):
    hf = jnp.full(shape, h, jnp.int32).astype(F32)
    return jnp.log(1.0 - jnp.exp2(-5.0 - hf))


def _layer_norm_gate(o, gn, g):
    oc = o - jnp.mean(o, axis=1, keepdims=True)
    on = oc * lax.rsqrt(jnp.mean(oc * oc, axis=1, keepdims=True) + EPS) * gn
    return on * _silu(g)


RET_HEADS_PER_STEP = 8


def _ret_prompt_kernel(q_ref, k_ref, v_ref, g_ref, c_ref, s_ref, gn_ref, o_ref, so_ref, st_s, *, dk):
    c = pl.program_id(2)
    C = RET_CHUNK
    hp = RET_HEADS_PER_STEP

    @pl.when(c == 0)
    def _():
        st_s[...] = jnp.zeros_like(st_s)

    rows = pl.ds(pl.multiple_of(c * C, C), C)
    cos, sin = c_ref[rows, :], s_ref[rows, :]
    t_sq = lax.broadcasted_iota(jnp.int32, (C, C), 0)
    s_sq = lax.broadcasted_iota(jnp.int32, (C, C), 1)
    t_w = lax.broadcasted_iota(jnp.int32, (C, dk), 0).astype(F32)

    for hh in range(hp):
        ln = slice(hh * dk, (hh + 1) * dk)
        qr = _ret_rope(q_ref[:, ln], cos, sin)
        kr = _ret_rope(k_ref[:, ln], cos, sin) * (dk ** -0.5)
        vb = v_ref[:, ln].astype(BF16)
        lg = _ret_log_gamma(pl.program_id(1) * hp + hh, (1, 1))
        dmat = jnp.exp(jnp.where(s_sq <= t_sq, (t_sq - s_sq).astype(F32) * lg, NEG_INF))
        inter = jnp.exp((t_w + 1.0) * lg)
        state_dec = jnp.exp((C - 1.0 - t_w) * lg)
        chunk_dec = jnp.exp(float(C) * lg)

        st = st_s[hh]
        qb = qr.astype(BF16)
        att = _dot_nt(qb, kr.astype(BF16)) * dmat
        o = _dot(att.astype(BF16), vb) + _dot(qb, st.astype(BF16)) * inter
        st_s[hh] = chunk_dec * st + _dot_tn((kr * state_dec).astype(BF16), vb)
        o_ref[:, ln] = _layer_norm_gate(o, gn_ref[:, ln], g_ref[:, ln]).astype(o_ref.dtype)

    @pl.when(c == pl.num_programs(2) - 1)
    def _():
        so_ref[...] = st_s[...]


def _ret_prompt(z, tables, gn, r, n_heads):
    bsz, t, width = z.shape
    dk = width // (4 * n_heads)
    C = RET_CHUNK
    hp = RET_HEADS_PER_STEP
    assert t % C == 0 and n_heads % hp == 0
    ng = n_heads // hp
    cos, sin = tables
    full = pl.BlockSpec((t, dk // 2), lambda b, h, c: (0, 0))

    def col(off):
        return pl.BlockSpec((None, C, hp * dk), lambda b, h, c, off=off: (b, c, off * ng + h))

    return pl.pallas_call(
        functools.partial(_ret_prompt_kernel, dk=dk),
        out_shape=(jax.ShapeDtypeStruct((bsz, t, n_heads * dk), BF16),
                   jax.ShapeDtypeStruct((bsz, n_heads, dk, dk), F32)),
        grid=(bsz, ng, t // C),
        in_specs=[col(0), col(1), col(2), col(3), full, full,
                  pl.BlockSpec((None, 1, hp * dk), lambda b, h, c: (r, 0, h))],
        out_specs=(pl.BlockSpec((None, C, hp * dk), lambda b, h, c: (b, c, h)),
                   pl.BlockSpec((None, hp, dk, dk), lambda b, h, c: (b, h, 0, 0))),
        scratch_shapes=[pltpu.VMEM((hp, dk, dk), F32)],
        compiler_params=_params("arbitrary", "arbitrary", "arbitrary"),
        name="retention_prompt",
    )(z, z, z, z, cos, sin, gn.reshape(gn.shape[0], 1, -1))


def _ret_sample_kernel(z_ref, s0_ref, c_ref, s_ref, gn_ref, o_ref, so_ref, *, n_heads, dk):
    cos, sin = c_ref[...], s_ref[...]
    for h in range(n_heads):
        qr = _ret_rope(z_ref[h:h + 1, :], cos, sin)
        kr = _ret_rope(z_ref[n_heads + h:n_heads + h + 1, :], cos, sin) * (dk ** -0.5)
        v = z_ref[2 * n_heads + h:2 * n_heads + h + 1, :]
        g = z_ref[3 * n_heads + h:3 * n_heads + h + 1, :]
        gamma = jnp.exp(_ret_log_gamma(h, (1, 1)))
        s0 = s0_ref[h]
        qs = jnp.sum(_row_to_col(qr, dk) * s0, axis=0, keepdims=True)
        o = jnp.sum(qr * kr, axis=1, keepdims=True) * v + qs * gamma
        so_ref[h] = gamma * s0 + _row_to_col(kr, dk) * v
        o_ref[h:h + 1, :] = _layer_norm_gate(o, gn_ref[h:h + 1, :], g)


def _ret_sample(z4, s0, tables, gn, r, n_heads):
    db, groups, dk = z4.shape
    cos, sin = tables
    row = pl.BlockSpec((1, dk // 2), lambda b: (0, 0))
    return pl.pallas_call(
        functools.partial(_ret_sample_kernel, n_heads=n_heads, dk=dk),
        out_shape=(jax.ShapeDtypeStruct((db, n_heads, dk), F32),
                   jax.ShapeDtypeStruct((db, n_heads, dk, dk), F32)),
        grid=(db,),
        in_specs=[pl.BlockSpec((None, groups, dk), lambda b: (b, 0, 0)),
                  pl.BlockSpec((None, None, n_heads, dk, dk), lambda b: (r, b, 0, 0, 0)), row, row,
                  pl.BlockSpec((None, n_heads, dk), lambda b: (r, 0, 0))],
        out_specs=(pl.BlockSpec((None, n_heads, dk), lambda b: (b, 0, 0)),
                   pl.BlockSpec((None, n_heads, dk, dk), lambda b: (b, 0, 0, 0))),
        compiler_params=_params("arbitrary"),
        name="retention_sample",
    )(z4, s0, cos, sin, gn)


def kernel(x_prompt, x_sample, cache_k, cache_v, page_table, state_hgrn, state_ret, state_conv,
           norm_mix, norm_ffn, norm_final, w_in_even, w_out_even, gn_hgrn, lb_hgrn,
           w_in_odd, w_out_odd, gn_ret, w_ffn_in, conv_w, conv_b, w_ffn_out):
    bsz, t, d = x_prompt.shape
    db, ts, _ = x_sample.shape
    assert ts == 1
    depth = norm_mix.shape[0]
    n_a = gn_hgrn.shape[1] // HEAD_DIM
    n_b = (w_in_even.shape[2] - 4 * gn_hgrn.shape[1]) // (3 * HEAD_DIM)
    assert n_b == SUBLANES
    n_pages = page_table.shape[1]
    assert n_pages % PAGES_PER_BLOCK == 0
    dk_c = w_in_odd.shape[2] // (4 * H_C)
    pos_p = np.arange(t)
    pos_s = np.array([n_pages * PAGE_SIZE])
    moba_tab_p, moba_tab_s = _moba_rope_tables(pos_p), _moba_rope_tables(pos_s)
    ret_tab_p, ret_tab_s = _ret_rope_tables(pos_p, dk_c), _ret_rope_tables(pos_s, dk_c)
    gn_ret3 = gn_ret.reshape(gn_ret.shape[0], H_C, dk_c)
    conv_buf = jnp.swapaxes(state_conv, 1, 2)
    mp = bsz * t

    xp = x_prompt.reshape(mp, d)
    xs = x_sample.reshape(db, d)
    k_p, v_p, h_p, r_p, c_p = [], [], [], [], []
    k_s, v_s, h_s, r_s, c_s = [], [], [], [], []
    (hp_, qp), (hs_, qs) = _norm_parts(xp, norm_mix, 0), _norm_parts(xs, norm_mix, 0)
    for l in range(depth):
        if l % 2 == 0:
            e = l // 2
            zp, zs = _mm([hp_], [hs_], w_in_even, e, ssq=qp, ssq_s=qs)
            zp = zp.reshape(bsz, t, -1)
            oa, st = _hgrn_prompt(zp, lb_hgrn, gn_hgrn, e, n_a)
            ob, k_rot, v_rows = _moba_prompt(zp, moba_tab_p, n_b, 4 * n_a)
            h_p.append(st)
            k_p.append(k_rot.reshape(bsz, t, n_b, HEAD_DIM))
            v_p.append(v_rows.reshape(bsz, t, n_b, HEAD_DIM))

            z3 = zs.reshape(db, -1, HEAD_DIM)
            oa_s, st = _hgrn_sample(z3, state_hgrn, lb_hgrn, gn_hgrn, e, n_a)
            idx = _moba_sample_gate(z3, cache_k, page_table, e, moba_tab_s, n_b, 4 * n_a)
            logical = idx[..., None] * PAGES_PER_BLOCK + jnp.arange(PAGES_PER_BLOCK, dtype=jnp.int32)
            pages = jnp.take_along_axis(page_table[:, None, :], logical.reshape(db, n_b, -1), axis=2)
            ob_s, k_rot, v_rows = _moba_sample_attn(z3, cache_k, cache_v, pages, e, moba_tab_s, n_b, 4 * n_a)
            h_s.append(st)
            k_s.append(k_rot.reshape(db, ts, n_b, HEAD_DIM))
            v_s.append(v_rows.reshape(db, ts, n_b, HEAD_DIM))

            xp, xs, hp_, hs_, qp, qs = _mm(
                [oa.reshape(mp, -1), ob.reshape(mp, -1)],
                [oa_s.reshape(db, -1).astype(BF16), ob_s.reshape(db, -1).astype(BF16)],
                w_out_even, e, res=xp, res_s=xs, gain=norm_ffn, gain_l=l)
        else:
            r = l // 2
            zp, zs = _mm([hp_], [hs_], w_in_odd, r, ssq=qp, ssq_s=qs)
            o, st = _ret_prompt(zp.reshape(bsz, t, -1), ret_tab_p, gn_ret, r, H_C)
            r_p.append(st)
            o_s, st = _ret_sample(zs.reshape(db, 4 * H_C, dk_c), state_ret, ret_tab_s, gn_ret3, r, H_C)
            r_s.append(st)
            xp, xs, hp_, hs_, qp, qs = _mm([o.reshape(mp, -1)], [o_s.reshape(db, -1).astype(BF16)], w_out_odd, r,
                                           res=xp, res_s=xs, gain=norm_ffn, gain_l=l)
        act, buf, act_s, buf_s = _ffn_in(hp_, hs_, qp, qs, conv_buf, w_ffn_in, conv_w, conv_b, l, bsz, t)
        c_p.append(buf)
        c_s.append(jnp.swapaxes(buf_s, 0, 1))
        if l + 1 < depth:
            xp, xs, hp_, hs_, qp, qs = _mm([act], [act_s], w_ffn_out, l, res=xp, res_s=xs,
                                           gain=norm_mix, gain_l=l + 1)
        else:
            xp, xs = _mm([act], [act_s], w_ffn_out, l, res=xp, res_s=xs)
    y_p = _rmsnorm(xp, norm_final.reshape(1, d), 0, F32).reshape(bsz, t, d)
    y_s = _rmsnorm(xs, norm_final.reshape(1, d), 0, F32).reshape(db, ts, d)
    st = jnp.stack
    return (y_p, y_s, st(k_p), st(v_p), st(k_s), st(v_s), st(h_p), st(h_s), st(r_p), st(r_s), st(c_p), st(c_s))
```

```python
import functools

import numpy as np
import jax
import jax.numpy as jnp
from jax import lax
from jax.experimental import pallas as pl
from jax.experimental.pallas import tpu as pltpu

F32 = jnp.float32
BF16 = jnp.bfloat16

HEAD_DIM = 128
MOBA_BLOCK = 256
MOBA_TOPK = 3
PAGE_SIZE = 128
PAGES_PER_BLOCK = MOBA_BLOCK // PAGE_SIZE
ROT_DIM = HEAD_DIM // 4
ROPE_THETA = 500000.0
RET_THETA = 10000.0
H_C = 8
SCAN_CHUNK = 64
SUB_CHUNK = 16
SUBLANES = 8
CONV_W = 3
EPS = 1e-6
LANES = 128
VMEM_LIMIT = 56 * 1024 * 1024
NEG_INF = float("-inf")
HIGHEST = lax.Precision.HIGHEST


def _params(*sem):
    return pltpu.CompilerParams(dimension_semantics=sem, vmem_limit_bytes=VMEM_LIMIT)


def _dot(a, b, precision=None):
    return jnp.dot(a, b, preferred_element_type=F32, precision=precision)


def _dot_nt(a, b, precision=None):
    return lax.dot_general(a, b, (((1,), (1,)), ((), ())), preferred_element_type=F32, precision=precision)


def _dot_tn(a, b, precision=None):
    return lax.dot_general(a, b, (((0,), (0,)), ((), ())), preferred_element_type=F32, precision=precision)


def _sigmoid(x):
    return 1.0 / (1.0 + jnp.exp(-x))


def _silu(x):
    return x * _sigmoid(x)


def _log_sigmoid(x):
    return jnp.minimum(x, 0.0) - jnp.log1p(jnp.exp(-jnp.abs(x)))


def _logaddexp(a, b):
    return jnp.maximum(a, b) + jnp.log1p(jnp.exp(-jnp.abs(a - b)))


def _row_to_col(row, n):
    r = lax.broadcasted_iota(jnp.int32, (n, n), 0)
    c = lax.broadcasted_iota(jnp.int32, (n, n), 1)
    return jnp.sum(jnp.where(r == c, jnp.broadcast_to(row, (n, n)), 0.0), axis=1, keepdims=True)


def _rmsnorm_kernel(x_ref, g_ref, o_ref):
    x = x_ref[...]
    y = x * lax.rsqrt(jnp.mean(x * x, axis=-1, keepdims=True) + EPS)
    o_ref[...] = (y * g_ref[...]).astype(o_ref.dtype)


def _rmsnorm(x, g, l, out_dtype):
    m, d = x.shape
    tm = min(m, 512)
    return pl.pallas_call(
        _rmsnorm_kernel,
        out_shape=jax.ShapeDtypeStruct((m, d), out_dtype),
        grid=(m // tm,),
        in_specs=[pl.BlockSpec((tm, d), lambda i: (i, 0)), pl.BlockSpec((None, 1, d), lambda i: (l, 0, 0))],
        out_specs=pl.BlockSpec((tm, d), lambda i: (i, 0)),
        compiler_params=_params("arbitrary"),
        name="rmsnorm",
    )(x, g.reshape(-1, 1, d))


def _split_dot(a_refs, wb_ref):
    kp = wb_ref.shape[0] // len(a_refs)
    acc = None
    for p, a_ref in enumerate(a_refs):
        part = _dot(a_ref[...], wb_ref[p * kp:(p + 1) * kp, :])
        acc = part if acc is None else acc + part
    return acc


def _lane_group_sum(x2):
    acc = x2[:, 0:LANES]
    for k in range(1, x2.shape[1] // LANES):
        acc = acc + x2[:, k * LANES:(k + 1) * LANES]
    return acc


def _row_scale(ssq_ref, d):
    total = jnp.sum(jnp.sum(ssq_ref[...], axis=0), axis=1, keepdims=True)
    return lax.rsqrt(total / d + EPS)


def _emit_norm_parts(x, g_ref, hg_ref, ssq_ref):
    hg_ref[...] = (x * g_ref[...]).astype(hg_ref.dtype)
    ssq_ref[...] = _lane_group_sum(x * x)


def _prep_kernel(x_ref, g_ref, hg_ref, ssq_ref):
    _emit_norm_parts(x_ref[...], g_ref, hg_ref, ssq_ref)


def _norm_parts(x, g, l):
    m, d = x.shape
    tm = min(m, 512)
    return pl.pallas_call(
        _prep_kernel,
        out_shape=(jax.ShapeDtypeStruct((m, d), BF16), jax.ShapeDtypeStruct((1, m, LANES), F32)),
        grid=(m // tm,),
        in_specs=[pl.BlockSpec((tm, d), lambda i: (i, 0)), pl.BlockSpec((None, 1, d), lambda i: (l, 0, 0))],
        out_specs=(pl.BlockSpec((tm, d), lambda i: (i, 0)), pl.BlockSpec((None, tm, LANES), lambda i: (0, i, 0))),
        compiler_params=_params("arbitrary"),
        name="norm_parts",
    )(x, g.reshape(-1, 1, d))


def _mm_kernel(*refs, n_parts, has_res, scaled, emit_norm):
    a_refs, refs = refs[:n_parts], refs[n_parts:]
    as_refs, refs = refs[:n_parts], refs[n_parts:]
    w_ref, refs = refs[0], refs[1:]
    if has_res:
        r_ref, rs_ref, refs = refs[0], refs[1], refs[2:]
    if scaled:
        q_ref, qs_ref, scale_s, refs = refs[0], refs[1], refs[-1], refs[2:-1]
    if emit_norm:
        g_ref, refs = refs[0], refs[1:]
        o_ref, os_ref, hg_ref, hgs_ref, ssq_ref, ssqs_ref, wb_ref = refs
    else:
        o_ref, os_ref, wb_ref = refs
    d_in = wb_ref.shape[0]
    i = pl.program_id(1)

    @pl.when(pl.program_id(1) == 0)
    def _():
        wb_ref[...] = w_ref[...].astype(BF16)
        acc_s = _split_dot(as_refs, wb_ref)
        if scaled:
            acc_s = acc_s * _row_scale(qs_ref, d_in)
        if has_res:
            acc_s = rs_ref[...] + acc_s
        os_ref[...] = acc_s
        if emit_norm:
            _emit_norm_parts(acc_s, g_ref, hgs_ref, ssqs_ref)

    acc = _split_dot(a_refs, wb_ref)
    if scaled:
        @pl.when(pl.program_id(0) == 0)
        def _():
            scale_s[i] = _row_scale(q_ref, d_in)

        acc = acc * scale_s[i]
    if has_res:
        acc = r_ref[...] + acc
    o_ref[...] = acc
    if emit_norm:
        _emit_norm_parts(acc, g_ref, hg_ref, ssq_ref)


def _mm_tiles(m, k, n):
    big = k <= 2048
    return min(m, 1024 if big else 512), min(n, 1024 if big else 512)


def _mm(a_parts, as_parts, w, l, res=None, res_s=None, ssq=None, ssq_s=None, gain=None, gain_l=0):
    n_parts = len(a_parts)
    m, kp = a_parts[0].shape
    ms = as_parts[0].shape[0]
    k, n = w.shape[1], w.shape[2]
    assert kp * n_parts == k
    tm, tn = _mm_tiles(m, k, n)
    assert m % tm == 0 and n % tn == 0
    nt = n // tn
    in_specs = ([pl.BlockSpec((tm, kp), lambda j, i: (i, 0))] * n_parts
                + [pl.BlockSpec((ms, kp), lambda j, i: (0, 0))] * n_parts
                + [pl.BlockSpec((None, k, tn), lambda j, i: (l, 0, j))])
    args = [*a_parts, *as_parts, w]
    if res is not None:
        in_specs += [pl.BlockSpec((tm, tn), lambda j, i: (i, j)), pl.BlockSpec((ms, tn), lambda j, i: (0, j))]
        args += [res, res_s]
    scratch = [pltpu.VMEM((k, tn), BF16)]
    if ssq is not None:
        in_specs += [pl.BlockSpec((ssq.shape[0], tm, LANES), lambda j, i: (0, jnp.where(j == 0, i, 0), 0)),
                     pl.BlockSpec(ssq_s.shape, lambda j, i: (0, 0, 0))]
        args += [ssq, ssq_s]
        scratch.append(pltpu.VMEM((m // tm, tm, 1), F32))
    out_shape = [jax.ShapeDtypeStruct((m, n), F32), jax.ShapeDtypeStruct((ms, n), F32)]
    out_specs = [pl.BlockSpec((tm, tn), lambda j, i: (i, j)), pl.BlockSpec((ms, tn), lambda j, i: (0, j))]
    if gain is not None:
        in_specs.append(pl.BlockSpec((None, 1, tn), lambda j, i: (gain_l, 0, j)))
        args.append(gain.reshape(-1, 1, n))
        out_shape += [jax.ShapeDtypeStruct((m, n), BF16), jax.ShapeDtypeStruct((ms, n), BF16),
                      jax.ShapeDtypeStruct((nt, m, LANES), F32), jax.ShapeDtypeStruct((nt, ms, LANES), F32)]
        out_specs += [pl.BlockSpec((tm, tn), lambda j, i: (i, j)), pl.BlockSpec((ms, tn), lambda j, i: (0, j)),
                      pl.BlockSpec((None, tm, LANES), lambda j, i: (j, i, 0)),
                      pl.BlockSpec((None, ms, LANES), lambda j, i: (j, 0, 0))]
    return pl.pallas_call(
        functools.partial(_mm_kernel, n_parts=n_parts, has_res=res is not None, scaled=ssq is not None,
                          emit_norm=gain is not None),
        out_shape=tuple(out_shape),
        grid=(nt, m // tm),
        in_specs=in_specs,
        out_specs=tuple(out_specs),
        scratch_shapes=scratch,
        compiler_params=_params("arbitrary", "arbitrary"),
        name="matmul",
    )(*args)


def _conv_gate(a, g, p0, p1, cw_ref, cb_ref):
    row = lax.broadcasted_iota(jnp.int32, a.shape, 0)
    a1 = jnp.where(row == 0, p1, pltpu.roll(a, 1, 0))
    a2 = jnp.where(row == 0, p0, jnp.where(row == 1, p1, pltpu.roll(a, 2, 0)))
    conv = cb_ref[...] + cw_ref[0:1, :] * a2
    conv = conv + cw_ref[1:2, :] * a1
    conv = conv + cw_ref[2:3, :] * a
    return _silu(conv) * g


def _ffn_in_kernel(h_ref, hs_ref, q_ref, qs_ref, bufs_ref, wa_ref, wb_ref, cw_ref, cb_ref,
                   o_ref, nb_ref, os_ref, nbs_ref, wa_s, wb_s, carry_s, *, tiles_per_seq):
    i = pl.program_id(1)
    d_in = wa_s.shape[0]

    @pl.when(i == 0)
    def _():
        wa_s[...] = wa_ref[...].astype(BF16)
        wb_s[...] = wb_ref[...].astype(BF16)
        hs = hs_ref[...]
        rs_s = _row_scale(qs_ref, d_in)
        a_s = _dot(hs, wa_s[...]) * rs_s
        conv = cb_ref[...] + cw_ref[0:1, :] * bufs_ref[0]
        conv = conv + cw_ref[1:2, :] * bufs_ref[1]
        conv = conv + cw_ref[2:3, :] * a_s
        os_ref[...] = (_silu(conv) * (_dot(hs, wb_s[...]) * rs_s)).astype(os_ref.dtype)
        nbs_ref[0] = bufs_ref[1]
        nbs_ref[1] = a_s

    h = h_ref[...]
    rs = _row_scale(q_ref, d_in)
    a = _dot(h, wa_s[...]) * rs
    g = _dot(h, wb_s[...]) * rs
    tm = a.shape[0]
    seq_start = (i % tiles_per_seq) == 0
    p0 = jnp.where(seq_start, 0.0, carry_s[SUBLANES - 2:SUBLANES - 1, :])
    p1 = jnp.where(seq_start, 0.0, carry_s[SUBLANES - 1:SUBLANES, :])
    o_ref[...] = _conv_gate(a, g, p0, p1, cw_ref, cb_ref).astype(o_ref.dtype)
    carry_s[...] = a[tm - SUBLANES:tm, :]

    @pl.when((i % tiles_per_seq) == tiles_per_seq - 1)
    def _():
        nb_ref[...] = a[tm - (CONV_W - 1):tm, :]


FFN_IN_ROWS = 1024


def _ffn_in(h, h_s, ssq, ssq_s, buf_s, w, cw, cb, l, bsz, t):
    m, k = h.shape
    db = h_s.shape[0]
    f = w.shape[2] // 2
    tm = min(t, FFN_IN_ROWS)
    tn = 512
    assert t % tm == 0 and f % tn == 0 and tm >= SUBLANES
    nf = f // tn
    tiles_per_seq = t // tm
    return pl.pallas_call(
        functools.partial(_ffn_in_kernel, tiles_per_seq=tiles_per_seq),
        out_shape=(jax.ShapeDtypeStruct((m, f), BF16),
                   jax.ShapeDtypeStruct((bsz, CONV_W - 1, f), F32),
                   jax.ShapeDtypeStruct((db, f), BF16),
                   jax.ShapeDtypeStruct((CONV_W - 1, db, f), F32)),
        grid=(nf, m // tm),
        in_specs=[pl.BlockSpec((tm, k), lambda j, i: (i, 0)),
                  pl.BlockSpec((db, k), lambda j, i: (0, 0)),
                  pl.BlockSpec((ssq.shape[0], tm, LANES), lambda j, i: (0, i, 0)),
                  pl.BlockSpec(ssq_s.shape, lambda j, i: (0, 0, 0)),
                  pl.BlockSpec((None, CONV_W - 1, db, tn), lambda j, i: (l, 0, 0, j)),
                  pl.BlockSpec((None, k, tn), lambda j, i: (l, 0, j)),
                  pl.BlockSpec((None, k, tn), lambda j, i: (l, 0, nf + j)),
                  pl.BlockSpec((None, CONV_W, tn), lambda j, i: (l, 0, j)),
                  pl.BlockSpec((None, 1, tn), lambda j, i: (l, 0, j))],
        out_specs=(pl.BlockSpec((tm, tn), lambda j, i: (i, j)),
                   pl.BlockSpec((None, CONV_W - 1, tn), lambda j, i: (i // tiles_per_seq, 0, j)),
                   pl.BlockSpec((db, tn), lambda j, i: (0, j)),
                   pl.BlockSpec((CONV_W - 1, db, tn), lambda j, i: (0, 0, j))),
        scratch_shapes=[pltpu.VMEM((k, tn), BF16), pltpu.VMEM((k, tn), BF16), pltpu.VMEM((SUBLANES, tn), F32)],
        compiler_params=_params("arbitrary", "arbitrary"),
        name="ffn_in_conv",
    )(h, h_s, ssq, ssq_s, buf_s, w, w, cw, cb.reshape(-1, 1, f))


HGRN_HEADS_PER_STEP = 8


def _hgrn_lower_bound(lbp, e):
    mx = jnp.max(lbp, axis=0, keepdims=True)
    ex = jnp.exp(lbp - mx)
    sm = ex / jnp.sum(ex, axis=0, keepdims=True)
    lb = jnp.zeros_like(sm[0:1])
    for l in range(1, e + 1):
        lb = lb + sm[l:l + 1]
    return lb


def _hgrn_gates(zf, zi, lb):
    logf = _logaddexp(jnp.log(lb), jnp.log1p(-lb) + _log_sigmoid(zf))
    return logf, 1.0 - jnp.exp(logf), _silu(zi)


def _hgrn_diag(qi, ki, vi, bi):
    half = SUBLANES
    row = lax.broadcasted_iota(jnp.int32, (half, HEAD_DIM), 0)
    q_lo, q_hi = qi[:half], qi[half:]
    b_lo, b_hi = bi[:half], bi[half:]
    o_lo = jnp.zeros((half, HEAD_DIM), F32)
    o_hi = jnp.zeros((half, HEAD_DIM), F32)
    for s in range(SUB_CHUNK):
        ks, vs, bs = ki[s:s + 1], vi[s:s + 1], bi[s:s + 1]
        if s < half:
            dec = jnp.exp(jnp.where(row >= s, b_lo - bs, NEG_INF))
            o_lo = o_lo + jnp.sum(q_lo * ks * dec, axis=1, keepdims=True) * vs
            dec = jnp.exp(b_hi - bs)
        else:
            dec = jnp.exp(jnp.where(row >= s - half, b_hi - bs, NEG_INF))
        o_hi = o_hi + jnp.sum(q_hi * ks * dec, axis=1, keepdims=True) * vs
    return jnp.concatenate([o_lo, o_hi], axis=0)


def _rel(x, b, i):
    r0 = i * SUB_CHUNK
    return x if i == 0 else x - b[r0 - 1:r0]


def _hgrn_intra_mxu(zq, kk, vv, b):
    C, cs = SCAN_CHUNK, SUB_CHUNK
    q_parts, k_parts = [], []
    for i in range(C // cs):
        r0, r1 = i * cs, (i + 1) * cs
        qt = (zq[r0:r1] * jnp.exp(_rel(b[r0:r1], b, i))).astype(BF16)
        kt = (kk[:r1] * jnp.exp(-_rel(b[:r1], b, i))).astype(BF16)
        q_rows = [qt]
        if r0 > 0:
            q_rows.insert(0, jnp.zeros((r0, HEAD_DIM), BF16))
        if r1 < C:
            q_rows.append(jnp.zeros((C - r1, HEAD_DIM), BF16))
            kt = jnp.concatenate([kt, jnp.zeros((C - r1, HEAD_DIM), BF16)], axis=0)
        q_parts.append(jnp.concatenate(q_rows, axis=0))
        k_parts.append(kt)
    return _dot_nt(jnp.concatenate(q_parts, axis=1), jnp.concatenate(k_parts, axis=1))


def _hgrn_intra_mxu_apply(att, vv):
    C = SCAN_CHUNK
    t_i = lax.broadcasted_iota(jnp.int32, (C, C), 0)
    s_i = lax.broadcasted_iota(jnp.int32, (C, C), 1)
    return _dot(jnp.where(s_i <= t_i, att, 0.0).astype(BF16), vv.astype(BF16))


def _hgrn_intra_exact(zq, kk, vv, b):
    cs = SUB_CHUNK
    parts = []
    for i in range(SCAN_CHUNK // cs):
        r0, r1 = i * cs, (i + 1) * cs
        qi, bi = zq[r0:r1], b[r0:r1]
        oi = _hgrn_diag(qi, kk[r0:r1], vv[r0:r1], bi)
        if i > 0:
            qt = qi * jnp.exp(_rel(bi, b, i))
            kt = kk[:r0] * jnp.exp(-_rel(b[:r0], b, i))
            att = _dot_nt(qt.astype(BF16), kt.astype(BF16))
            oi = oi + _dot(att.astype(BF16), vv[:r0].astype(BF16))
        parts.append(oi)
    return jnp.concatenate(parts, axis=0)


HGRN_SAFE_SUB_DECAY = -80.0


def _hgrn_prompt_kernel(q_ref, f_ref, i_ref, g_ref, lbp_ref, gn_ref, o_ref, s_ref, st_ref, kk_s, vv_s, b_s,
                        *, e, n_chunks):
    c = pl.program_id(2)
    C, cs = SCAN_CHUNK, SUB_CHUNK
    hp = HGRN_HEADS_PER_STEP
    hd = HEAD_DIM

    @pl.when(c == 0)
    def _():
        st_ref[...] = jnp.zeros_like(st_ref)

    lb = _hgrn_lower_bound(lbp_ref[...], e)
    gn = gn_ref[...]
    r_i = lax.broadcasted_iota(jnp.int32, (C, C), 0)
    c_i = lax.broadcasted_iota(jnp.int32, (C, C), 1)
    tri = (c_i <= r_i).astype(F32)

    def rows_of(ci):
        return pl.ds(pl.multiple_of(ci * C, C), C)

    def gates(ci, sub_decay):
        rows = rows_of(ci)
        logf, kk, vv = _hgrn_gates(f_ref[rows, :], i_ref[rows, :], lb)
        b = _dot(tri, logf, HIGHEST)
        kk_s[rows, :] = kk
        vv_s[rows, :] = vv
        b_s[rows, :] = b
        for i in range(C // cs):
            sub_decay = jnp.minimum(sub_decay, _rel(b[(i + 1) * cs - 1:(i + 1) * cs], b, i))
        return sub_decay

    sub_decay = lax.fori_loop(0, n_chunks, gates, jnp.zeros((1, hp * hd), F32))
    safe = jnp.min(sub_decay) >= HGRN_SAFE_SUB_DECAY

    def scan(intra_first, intra_second):
        def chunk(ci, carry):
            rows = rows_of(ci)
            zq, zg = q_ref[rows, :], g_ref[rows, :]
            kk_all, vv_all, b_all = kk_s[rows, :], vv_s[rows, :], b_s[rows, :]
            firsts = []
            for hh in range(hp):
                ln = slice(hh * hd, (hh + 1) * hd)
                q, kk, vv, b = zq[:, ln], kk_all[:, ln], vv_all[:, ln], b_all[:, ln]
                st = st_ref[hh]
                o_inter = _dot_nt((q * jnp.exp(b)).astype(BF16), st.astype(BF16))
                firsts.append((o_inter, intra_first(q, kk, vv, b)))
                bl = b[C - 1:C]
                khat = kk * jnp.exp(bl - b)
                st_ref[hh] = st * jnp.exp(bl) + _dot_tn(vv.astype(BF16), khat.astype(BF16))
            outs = []
            for hh, (o_inter, first) in enumerate(firsts):
                ln = slice(hh * hd, (hh + 1) * hd)
                o = o_inter + intra_second(first, vv_all[:, ln])
                on = o * lax.rsqrt(jnp.mean(o * o, axis=1, keepdims=True) + EPS) * gn[:, ln]
                outs.append(on * _sigmoid(zg[:, ln]))
            o_ref[rows, :] = jnp.concatenate(outs, axis=1).astype(o_ref.dtype)
            return carry
        lax.fori_loop(0, n_chunks, chunk, 0)

    @pl.when(safe)
    def _():
        scan(_hgrn_intra_mxu, _hgrn_intra_mxu_apply)

    @pl.when(jnp.logical_not(safe))
    def _():
        scan(_hgrn_intra_exact, lambda intra, vv: intra)

    @pl.when(c == pl.num_programs(2) - 1)
    def _():
        for hh in range(hp):
            s_ref[hh] = st_ref[hh].T


def _hgrn_prompt(z, lb_hgrn, gn, e, n_heads):
    bsz, t, _ = z.shape
    ct = min(t, 512)
    hp = HGRN_HEADS_PER_STEP
    assert t % ct == 0 and ct % SCAN_CHUNK == 0 and n_heads % hp == 0
    n_even = lb_hgrn.shape[0]
    hd = HEAD_DIM
    ng = n_heads // hp

    def col(off):
        return pl.BlockSpec((None, ct, hp * hd), lambda b, h, c, off=off: (b, c, off * ng + h))

    return pl.pallas_call(
        functools.partial(_hgrn_prompt_kernel, e=e, n_chunks=ct // SCAN_CHUNK),
        out_shape=(jax.ShapeDtypeStruct((bsz, t, n_heads * hd), BF16),
                   jax.ShapeDtypeStruct((bsz, n_heads, hd, hd), F32)),
        grid=(bsz, ng, t // ct),
        in_specs=[col(0), col(1), col(2), col(3),
                  pl.BlockSpec((n_even, hp * hd), lambda b, h, c: (0, h)),
                  pl.BlockSpec((None, 1, hp * hd), lambda b, h, c: (e, 0, h))],
        out_specs=(pl.BlockSpec((None, ct, hp * hd), lambda b, h, c: (b, c, h)),
                   pl.BlockSpec((None, hp, hd, hd), lambda b, h, c: (b, h, 0, 0))),
        scratch_shapes=[pltpu.VMEM((hp, hd, hd), F32)] + [pltpu.VMEM((ct, hp * hd), F32)] * 3,
        compiler_params=_params("arbitrary", "arbitrary", "arbitrary"),
        name="hgrn_prompt",
    )(z, z, z, z, lb_hgrn, gn.reshape(n_even, 1, -1))


def _hgrn_sample_kernel(z_ref, s0_ref, lbp_ref, gn_ref, o_ref, s_ref, *, e, n_heads):
    hd = HEAD_DIM
    lb_all = _hgrn_lower_bound(lbp_ref[...], e)
    for h in range(n_heads):
        lanes = slice(h * hd, (h + 1) * hd)
        zq = z_ref[h:h + 1, :]
        zg = z_ref[3 * n_heads + h:3 * n_heads + h + 1, :]
        logf, kk, vv = _hgrn_gates(z_ref[n_heads + h:n_heads + h + 1, :],
                                   z_ref[2 * n_heads + h:2 * n_heads + h + 1, :], lb_all[:, lanes])
        s_new = _row_to_col(jnp.exp(logf), hd) * s0_ref[h] + _row_to_col(kk, hd) * vv
        s_ref[h] = s_new
        o = jnp.sum(_row_to_col(zq, hd) * s_new, axis=0, keepdims=True)
        on = o * lax.rsqrt(jnp.mean(o * o, axis=1, keepdims=True) + EPS) * gn_ref[:, lanes]
        o_ref[h:h + 1, :] = on * _sigmoid(zg)


def _hgrn_sample(z3, s0, lb_hgrn, gn, e, n_heads):
    db, groups, hd = z3.shape
    n_even = lb_hgrn.shape[0]
    w = n_heads * hd
    return pl.pallas_call(
        functools.partial(_hgrn_sample_kernel, e=e, n_heads=n_heads),
        out_shape=(jax.ShapeDtypeStruct((db, n_heads, hd), F32),
                   jax.ShapeDtypeStruct((db, n_heads, hd, hd), F32)),
        grid=(db,),
        in_specs=[pl.BlockSpec((None, groups, hd), lambda b: (b, 0, 0)),
                  pl.BlockSpec((None, None, n_heads, hd, hd), lambda b: (e, b, 0, 0, 0)),
                  pl.BlockSpec((n_even, w), lambda b: (0, 0)),
                  pl.BlockSpec((None, 1, w), lambda b: (e, 0, 0))],
        out_specs=(pl.BlockSpec((None, n_heads, hd), lambda b: (b, 0, 0)),
                   pl.BlockSpec((None, n_heads, hd, hd), lambda b: (b, 0, 0, 0))),
        compiler_params=_params("arbitrary"),
        name="hgrn_sample",
    )(z3, s0, lb_hgrn, gn.reshape(n_even, 1, -1))


def _moba_rope_tables(pos):
    half = ROT_DIM // 2
    inv = ROPE_THETA ** (-np.arange(half, dtype=np.float64) / half)
    ang = np.asarray(pos, np.float64)[:, None] * inv[None, :]
    cos, sin = np.cos(ang), np.sin(ang)
    t = ang.shape[0]
    c = np.ones((t, HEAD_DIM))
    s1 = np.zeros((t, HEAD_DIM))
    s2 = np.zeros((t, HEAD_DIM))
    c[:, :half] = cos
    c[:, half:2 * half] = cos
    s1[:, :half] = -sin
    s2[:, half:2 * half] = sin
    return tuple(jnp.asarray(a, F32) for a in (c, s1, s2))


def _ret_rope_tables(pos, dk):
    half = dk // 2
    inv = RET_THETA ** (-np.arange(half, dtype=np.float64) / half)
    ang = np.asarray(pos, np.float64)[:, None] * inv[None, :]
    return jnp.asarray(np.cos(ang), F32), jnp.asarray(np.sin(ang), F32)


def _moba_rope(x, c, s1, s2):
    n = x.shape[-1]
    half = ROT_DIM // 2
    return x * c + pltpu.roll(x, n - half, 1) * s1 + pltpu.roll(x, half, 1) * s2


def _ret_rope(x, c, s):
    half = x.shape[-1] // 2
    x1, x2 = x[:, :half], x[:, half:]
    return jnp.concatenate([x1 * c - x2 * s, x2 * c + x1 * s], axis=1)


MOBA_MASK = -(2.0 ** 100)


def _moba_unselected(qr, km, n_past, topk):
    nbp = -(-n_past // SUBLANES) * SUBLANES
    km_hi, qr_hi = km.astype(BF16), qr.astype(BF16)
    km_lo, qr_lo = (km - km_hi.astype(F32)).astype(BF16), (qr - qr_hi.astype(F32)).astype(BF16)
    gate = (_dot_nt(km_hi, qr_hi) + _dot_nt(km_hi, qr_lo) + _dot_nt(km_lo, qr_hi))[0:nbp]
    row = lax.broadcasted_iota(jnp.int32, gate.shape, 0)
    rank = jnp.zeros(gate.shape, F32)
    for jp in range(n_past):
        gj = gate[jp:jp + 1, :]
        rank = rank + jnp.where(gj > gate, 1.0, jnp.where(gj == gate, jnp.where(row > jp, 1.0, 0.0), 0.0))
    uns = jnp.where(row < n_past, jnp.where(rank < topk, 0.0, -1.0), 0.0)
    uns = jnp.concatenate([uns, jnp.zeros((LANES - nbp, uns.shape[1]), F32)], axis=0)
    return uns.T


MOBA_HEADS_PER_STEP = 2


def _moba_prompt_kernel(q_ref, k_ref, v_ref, c_ref, s1_ref, s2_ref, o_ref, ko_ref, vo_ref,
                        kb_s, vb_s, km_s, *, nb, topk, scale):
    i = pl.program_id(2)
    blk = MOBA_BLOCK
    hd = HEAD_DIM
    hp = MOBA_HEADS_PER_STEP

    @pl.when(i == 0)
    def _():
        km_s[...] = jnp.zeros_like(km_s)
        lane = lax.broadcasted_iota(jnp.int32, (blk, LANES), 1)
        for j in range(nb):
            rows = slice(j * blk, (j + 1) * blk)
            for hh in range(hp):
                ln = slice(hh * hd, (hh + 1) * hd)
                kr = _moba_rope(k_ref[rows, ln], c_ref[rows, :], s1_ref[rows, :], s2_ref[rows, :])
                ko_ref[rows, ln] = kr
                kb_s[hh, rows, 0:hd] = kr.astype(BF16)
                kb_s[hh, rows, hd:hd + LANES] = jnp.where(lane == j, -MOBA_MASK, 0.0).astype(BF16)
                km_s[hh, j:j + 1, :] = jnp.mean(kr, axis=0, keepdims=True)
                v = v_ref[rows, ln]
                vo_ref[rows, ln] = v
                vb_s[hh, rows, :] = v.astype(BF16)

    rows = pl.ds(pl.multiple_of(i * blk, blk), blk)
    cq, s1q, s2q = c_ref[rows, :], s1_ref[rows, :], s2_ref[rows, :]
    r_i = lax.broadcasted_iota(jnp.int32, (blk, blk), 0)
    c_i = lax.broadcasted_iota(jnp.int32, (blk, blk), 1)

    for ii in range(nb):
        @pl.when(i == ii)
        def _(ii=ii):
            ncol = (ii + 1) * blk
            scores = []
            for hh in range(hp):
                qr = _moba_rope(q_ref[:, hh * hd:(hh + 1) * hd], cq, s1q, s2q)
                qb = (qr * scale).astype(BF16)
                if ii > topk:
                    uns = _moba_unselected(qr, km_s[hh], ii, topk)
                    s = _dot_nt(jnp.concatenate([qb, uns.astype(BF16)], axis=1), kb_s[hh, 0:ncol, :])
                else:
                    s = _dot_nt(qb, kb_s[hh, 0:ncol, 0:hd])
                scores.append(s)
            for hh, s in enumerate(scores):
                pieces = [s[:, :ii * blk]] if ii > 0 else []
                pieces.append(jnp.where(c_i <= r_i, s[:, ii * blk:], NEG_INF))
                s = jnp.concatenate(pieces, axis=1)
                m = jnp.max(s, axis=1, keepdims=True)
                p = jnp.exp(s - m)
                l = jnp.sum(p, axis=1, keepdims=True)
                o = _dot(p.astype(BF16), vb_s[hh, 0:ncol, :]) / l
                o_ref[:, hh * hd:(hh + 1) * hd] = o.astype(o_ref.dtype)


def _moba_prompt(z, tables, n_heads, q_off):
    bsz, t, _ = z.shape
    hd = HEAD_DIM
    blk = MOBA_BLOCK
    hp = MOBA_HEADS_PER_STEP
    assert t % blk == 0 and n_heads % hp == 0 and q_off % hp == 0
    nb = t // blk
    assert nb <= LANES
    ng = n_heads // hp
    topk = max(1, min(MOBA_TOPK, nb - 1))
    c, s1, s2 = tables
    full = pl.BlockSpec((t, hd), lambda b, h, i: (0, 0))

    def seq(off):
        return pl.BlockSpec((None, t, hp * hd), lambda b, h, i, off=off: (b, 0, off // hp + h))

    return pl.pallas_call(
        functools.partial(_moba_prompt_kernel, nb=nb, topk=topk, scale=hd ** -0.5),
        out_shape=(jax.ShapeDtypeStruct((bsz, t, n_heads * hd), BF16),
                   jax.ShapeDtypeStruct((bsz, t, n_heads * hd), F32),
                   jax.ShapeDtypeStruct((bsz, t, n_heads * hd), F32)),
        grid=(bsz, ng, nb),
        in_specs=[pl.BlockSpec((None, blk, hp * hd), lambda b, h, i: (b, i, q_off // hp + h)),
                  seq(q_off + n_heads), seq(q_off + 2 * n_heads), full, full, full],
        out_specs=(pl.BlockSpec((None, blk, hp * hd), lambda b, h, i: (b, i, h)), seq(0), seq(0)),
        scratch_shapes=[pltpu.VMEM((hp, t, hd + LANES), BF16), pltpu.VMEM((hp, t, hd), BF16),
                        pltpu.VMEM((hp, LANES, hd), F32)],
        compiler_params=_params("arbitrary", "arbitrary", "arbitrary"),
        name="moba_prompt",
    )(z, z, z, c, s1, s2)


GATE_PAGES_PER_STEP = 16


def _moba_gate_kernel(pt_ref, z_ref, c_ref, s1_ref, s2_ref, *rest, n_heads, q_off, topk):
    pages = rest[:GATE_PAGES_PER_STEP]
    gate_ref, idx_ref = rest[GATE_PAGES_PER_STEP:]
    s = pl.program_id(1)
    bps = GATE_PAGES_PER_STEP // PAGES_PER_BLOCK

    @pl.when(s == 0)
    def _():
        gate_ref[...] = jnp.full(gate_ref.shape, NEG_INF, F32)

    q = _moba_rope(z_ref[q_off:q_off + n_heads, :], c_ref[...], s1_ref[...], s2_ref[...])
    lane = lax.broadcasted_iota(jnp.int32, gate_ref.shape, 1)
    for blk in range(bps):
        ksum = jnp.zeros_like(q)
        for p in range(PAGES_PER_BLOCK):
            ksum = ksum + jnp.sum(pages[blk * PAGES_PER_BLOCK + p][...], axis=0)
        val = jnp.sum(q * (ksum * (1.0 / MOBA_BLOCK)), axis=1, keepdims=True)
        gate_ref[...] = jnp.where(lane == s * bps + blk, val, gate_ref[...])

    @pl.when(s == pl.num_programs(1) - 1)
    def _():
        g = gate_ref[...]
        lane_f = lane.astype(F32)
        out = jnp.zeros(g.shape, F32)
        for r in range(topk):
            mx = jnp.max(g, axis=1, keepdims=True)
            am = jnp.min(jnp.where(g == mx, lane_f, float(LANES)), axis=1, keepdims=True)
            out = jnp.where(lane_f == float(r), am, out)
            g = jnp.where(lane_f == am, NEG_INF, g)
        idx_ref[...] = out.astype(jnp.int32)


def _moba_sample_gate(z3, cache_k, page_table, e, tables, n_heads, q_off):
    db, groups, hd = z3.shape
    n_pages = page_table.shape[1]
    n_blocks = n_pages // PAGES_PER_BLOCK
    assert n_pages % GATE_PAGES_PER_STEP == 0 and n_blocks <= LANES
    topk = min(MOBA_TOPK, n_blocks)
    steps = n_pages // GATE_PAGES_PER_STEP
    row = pl.BlockSpec((1, hd), lambda b, s, pt: (0, 0))

    def page(p):
        return pl.BlockSpec((None, None, PAGE_SIZE, n_heads, hd),
                            lambda b, s, pt, p=p: (e, pt[b * n_pages + s * GATE_PAGES_PER_STEP + p], 0, 0, 0))

    gate, idx = pl.pallas_call(
        functools.partial(_moba_gate_kernel, n_heads=n_heads, q_off=q_off, topk=topk),
        out_shape=(jax.ShapeDtypeStruct((db, n_heads, LANES), F32),
                   jax.ShapeDtypeStruct((db, n_heads, LANES), jnp.int32)),
        grid_spec=pltpu.PrefetchScalarGridSpec(
            num_scalar_prefetch=1,
            grid=(db, steps),
            in_specs=[pl.BlockSpec((None, groups, hd), lambda b, s, pt: (b, 0, 0)), row, row, row]
                     + [page(p) for p in range(GATE_PAGES_PER_STEP)],
            out_specs=(pl.BlockSpec((None, n_heads, LANES), lambda b, s, pt: (b, 0, 0)),
                       pl.BlockSpec((None, n_heads, LANES), lambda b, s, pt: (b, 0, 0)))),
        compiler_params=_params("arbitrary", "arbitrary"),
        name="moba_sample_gate",
    )(page_table.reshape(-1), z3, *tables, *([cache_k] * GATE_PAGES_PER_STEP))
    return idx[:, :, :topk]


def _moba_sample_attn_kernel(pg_ref, z_ref, c_ref, s1_ref, s2_ref, *rest, n_heads, n_sel, q_off, scale):
    k_pages, v_pages = rest[:n_sel], rest[n_sel:2 * n_sel]
    o_ref, kn_ref, vn_ref = rest[2 * n_sel:]
    h = pl.program_id(1)
    hd = HEAD_DIM
    rows = PAGE_SIZE * n_heads
    q = _moba_rope(z_ref[pl.ds(q_off + h, 1), :], c_ref[...], s1_ref[...], s2_ref[...])
    kn = _moba_rope(z_ref[pl.ds(q_off + n_heads + h, 1), :], c_ref[...], s1_ref[...], s2_ref[...])
    vn = z_ref[pl.ds(q_off + 2 * n_heads + h, 1), :]
    kn_ref[...] = kn
    vn_ref[...] = vn
    s_own = jnp.sum(q * kn, axis=1, keepdims=True) * scale

    q8 = jnp.broadcast_to(q, (SUBLANES, hd)).astype(BF16)
    col = lax.broadcasted_iota(jnp.int32, (1, rows), 1)
    mine = (col % n_heads) == h
    scores = []
    for kp in k_pages:
        sc = _dot_nt(q8, kp[...].reshape(rows, hd).astype(BF16))[0:1] * scale
        scores.append(jnp.where(mine, sc, NEG_INF))
    m = s_own
    for sc in scores:
        m = jnp.maximum(m, jnp.max(sc, axis=1, keepdims=True))
    l = jnp.exp(s_own - m)
    acc = l * vn
    for sc, vp in zip(scores, v_pages):
        p = jnp.exp(sc - m)
        l = l + jnp.sum(p, axis=1, keepdims=True)
        p8 = jnp.broadcast_to(p, (SUBLANES, rows)).astype(BF16)
        acc = acc + _dot(p8, vp[...].reshape(rows, hd).astype(BF16))[0:1]
    o_ref[...] = acc / l


def _moba_sample_attn(z3, cache_k, cache_v, pages, e, tables, n_heads, q_off):
    db, groups, hd = z3.shape
    n_sel = pages.shape[2]
    row = pl.BlockSpec((1, hd), lambda b, h, pg: (0, 0))

    def page_spec(s):
        return pl.BlockSpec((None, None, PAGE_SIZE, n_heads, hd),
                            lambda b, h, pg, s=s: (e, pg[(b * n_heads + h) * n_sel + s], 0, 0, 0))

    out = pl.BlockSpec((None, None, 1, hd), lambda b, h, pg: (b, h, 0, 0))
    shape = jax.ShapeDtypeStruct((db, n_heads, 1, hd), F32)
    return pl.pallas_call(
        functools.partial(_moba_sample_attn_kernel, n_heads=n_heads, n_sel=n_sel, q_off=q_off, scale=hd ** -0.5),
        out_shape=(shape, shape, shape),
        grid_spec=pltpu.PrefetchScalarGridSpec(
            num_scalar_prefetch=1,
            grid=(db, n_heads),
            in_specs=[pl.BlockSpec((None, groups, hd), lambda b, h, pg: (b, 0, 0)), row, row, row]
                     + [page_spec(s) for s in range(n_sel)] * 2,
            out_specs=(out, out, out)),
        compiler_params=_params("arbitrary", "arbitrary"),
        name="moba_sample_attn",
    )(pages.reshape(-1), z3, *tables, *([cache_k] * n_sel), *([cache_v] * n_sel))


RET_CHUNK = 256


def _ret_log_gamma(h, shape):
    hf = jnp.full(shape, h, jnp.int32).astype(F32)
    return jnp.log(1.0 - jnp.exp2(-5.0 - hf))


def _layer_norm_gate(o, gn, g):
    oc = o - jnp.mean(o, axis=1, keepdims=True)
    on = oc * lax.rsqrt(jnp.mean(oc * oc, axis=1, keepdims=True) + EPS) * gn
    return on * _silu(g)


RET_HEADS_PER_STEP = 8


def _ret_prompt_kernel(q_ref, k_ref, v_ref, g_ref, c_ref, s_ref, gn_ref, o_ref, so_ref, st_s, *, dk):
    c = pl.program_id(2)
    C = RET_CHUNK
    hp = RET_HEADS_PER_STEP

    @pl.when(c == 0)
    def _():
        st_s[...] = jnp.zeros_like(st_s)

    rows = pl.ds(pl.multiple_of(c * C, C), C)
    cos, sin = c_ref[rows, :], s_ref[rows, :]
    t_sq = lax.broadcasted_iota(jnp.int32, (C, C), 0)
    s_sq = lax.broadcasted_iota(jnp.int32, (C, C), 1)
    t_w = lax.broadcasted_iota(jnp.int32, (C, dk), 0).astype(F32)

    for hh in range(hp):
        ln = slice(hh * dk, (hh + 1) * dk)
        qr = _ret_rope(q_ref[:, ln], cos, sin)
        kr = _ret_rope(k_ref[:, ln], cos, sin) * (dk ** -0.5)
        vb = v_ref[:, ln].astype(BF16)
        lg = _ret_log_gamma(pl.program_id(1) * hp + hh, (1, 1))
        dmat = jnp.exp(jnp.where(s_sq <= t_sq, (t_sq - s_sq).astype(F32) * lg, NEG_INF))
        inter = jnp.exp((t_w + 1.0) * lg)
        state_dec = jnp.exp((C - 1.0 - t_w) * lg)
        chunk_dec = jnp.exp(float(C) * lg)

        st = st_s[hh]
        qb = qr.astype(BF16)
        att = _dot_nt(qb, kr.astype(BF16)) * dmat
        o = _dot(att.astype(BF16), vb) + _dot(qb, st.astype(BF16)) * inter
        st_s[hh] = chunk_dec * st + _dot_tn((kr * state_dec).astype(BF16), vb)
        o_ref[:, ln] = _layer_norm_gate(o, gn_ref[:, ln], g_ref[:, ln]).astype(o_ref.dtype)

    @pl.when(c == pl.num_programs(2) - 1)
    def _():
        so_ref[...] = st_s[...]


def _ret_prompt(z, tables, gn, r, n_heads):
    bsz, t, width = z.shape
    dk = width // (4 * n_heads)
    C = RET_CHUNK
    hp = RET_HEADS_PER_STEP
    assert t % C == 0 and n_heads % hp == 0
    ng = n_heads // hp
    cos, sin = tables
    full = pl.BlockSpec((t, dk // 2), lambda b, h, c: (0, 0))

    def col(off):
        return pl.BlockSpec((None, C, hp * dk), lambda b, h, c, off=off: (b, c, off * ng + h))

    return pl.pallas_call(
        functools.partial(_ret_prompt_kernel, dk=dk),
        out_shape=(jax.ShapeDtypeStruct((bsz, t, n_heads * dk), BF16),
                   jax.ShapeDtypeStruct((bsz, n_heads, dk, dk), F32)),
        grid=(bsz, ng, t // C),
        in_specs=[col(0), col(1), col(2), col(3), full, full,
                  pl.BlockSpec((None, 1, hp * dk), lambda b, h, c: (r, 0, h))],
        out_specs=(pl.BlockSpec((None, C, hp * dk), lambda b, h, c: (b, c, h)),
                   pl.BlockSpec((None, hp, dk, dk), lambda b, h, c: (b, h, 0, 0))),
        scratch_shapes=[pltpu.VMEM((hp, dk, dk), F32)],
        compiler_params=_params("arbitrary", "arbitrary", "arbitrary"),
        name="retention_prompt",
    )(z, z, z, z, cos, sin, gn.reshape(gn.shape[0], 1, -1))


def _ret_sample_kernel(z_ref, s0_ref, c_ref, s_ref, gn_ref, o_ref, so_ref, *, n_heads, dk):
    cos, sin = c_ref[...], s_ref[...]
    for h in range(n_heads):
        qr = _ret_rope(z_ref[h:h + 1, :], cos, sin)
        kr = _ret_rope(z_ref[n_heads + h:n_heads + h + 1, :], cos, sin) * (dk ** -0.5)
        v = z_ref[2 * n_heads + h:2 * n_heads + h + 1, :]
        g = z_ref[3 * n_heads + h:3 * n_heads + h + 1, :]
        gamma = jnp.exp(_ret_log_gamma(h, (1, 1)))
        s0 = s0_ref[h]
        qs = jnp.sum(_row_to_col(qr, dk) * s0, axis=0, keepdims=True)
        o = jnp.sum(qr * kr, axis=1, keepdims=True) * v + qs * gamma
        so_ref[h] = gamma * s0 + _row_to_col(kr, dk) * v
        o_ref[h:h + 1, :] = _layer_norm_gate(o, gn_ref[h:h + 1, :], g)


def _ret_sample(z4, s0, tables, gn, r, n_heads):
    db, groups, dk = z4.shape
    cos, sin = tables
    row = pl.BlockSpec((1, dk // 2), lambda b: (0, 0))
    return pl.pallas_call(
        functools.partial(_ret_sample_kernel, n_heads=n_heads, dk=dk),
        out_shape=(jax.ShapeDtypeStruct((db, n_heads, dk), F32),
                   jax.ShapeDtypeStruct((db, n_heads, dk, dk), F32)),
        grid=(db,),
        in_specs=[pl.BlockSpec((None, groups, dk), lambda b: (b, 0, 0)),
                  pl.BlockSpec((None, None, n_heads, dk, dk), lambda b: (r, b, 0, 0, 0)), row, row,
                  pl.BlockSpec((None, n_heads, dk), lambda b: (r, 0, 0))],
        out_specs=(pl.BlockSpec((None, n_heads, dk), lambda b: (b, 0, 0)),
                   pl.BlockSpec((None, n_heads, dk, dk), lambda b: (b, 0, 0, 0))),
        compiler_params=_params("arbitrary"),
        name="retention_sample",
    )(z4, s0, cos, sin, gn)


def kernel(x_prompt, x_sample, cache_k, cache_v, page_table, state_hgrn, state_ret, state_conv,
           norm_mix, norm_ffn, norm_final, w_in_even, w_out_even, gn_hgrn, lb_hgrn,
           w_in_odd, w_out_odd, gn_ret, w_ffn_in, conv_w, conv_b, w_ffn_out):
    bsz, t, d = x_prompt.shape
    db, ts, _ = x_sample.shape
    assert ts == 1
    depth = norm_mix.shape[0]
    n_a = gn_hgrn.shape[1] // HEAD_DIM
    n_b = (w_in_even.shape[2] - 4 * gn_hgrn.shape[1]) // (3 * HEAD_DIM)
    assert n_b == SUBLANES
    n_pages = page_table.shape[1]
    assert n_pages % PAGES_PER_BLOCK == 0
    dk_c = w_in_odd.shape[2] // (4 * H_C)
    pos_p = np.arange(t)
    pos_s = np.array([n_pages * PAGE_SIZE])
    moba_tab_p, moba_tab_s = _moba_rope_tables(pos_p), _moba_rope_tables(pos_s)
    ret_tab_p, ret_tab_s = _ret_rope_tables(pos_p, dk_c), _ret_rope_tables(pos_s, dk_c)
    gn_ret3 = gn_ret.reshape(gn_ret.shape[0], H_C, dk_c)
    conv_buf = jnp.swapaxes(state_conv, 1, 2)
    mp = bsz * t

    xp = x_prompt.reshape(mp, d)
    xs = x_sample.reshape(db, d)
    k_p, v_p, h_p, r_p, c_p = [], [], [], [], []
    k_s, v_s, h_s, r_s, c_s = [], [], [], [], []
    (hp_, qp), (hs_, qs) = _norm_parts(xp, norm_mix, 0), _norm_parts(xs, norm_mix, 0)
    for l in range(depth):
        if l % 2 == 0:
            e = l // 2
            zp, zs = _mm([hp_], [hs_], w_in_even, e, ssq=qp, ssq_s=qs)
            zp = zp.reshape(bsz, t, -1)
            oa, st = _hgrn_prompt(zp, lb_hgrn, gn_hgrn, e, n_a)
            ob, k_rot, v_rows = _moba_prompt(zp, moba_tab_p, n_b, 4 * n_a)
            h_p.append(st)
            k_p.append(k_rot.reshape(bsz, t, n_b, HEAD_DIM))
            v_p.append(v_rows.reshape(bsz, t, n_b, HEAD_DIM))

            z3 = zs.reshape(db, -1, HEAD_DIM)
            oa_s, st = _hgrn_sample(z3, state_hgrn, lb_hgrn, gn_hgrn, e, n_a)
            idx = _moba_sample_gate(z3, cache_k, page_table, e, moba_tab_s, n_b, 4 * n_a)
            logical = idx[..., None] * PAGES_PER_BLOCK + jnp.arange(PAGES_PER_BLOCK, dtype=jnp.int32)
            pages = jnp.take_along_axis(page_table[:, None, :], logical.reshape(db, n_b, -1), axis=2)
            ob_s, k_rot, v_rows = _moba_sample_attn(z3, cache_k, cache_v, pages, e, moba_tab_s, n_b, 4 * n_a)
            h_s.append(st)
            k_s.append(k_rot.reshape(db, ts, n_b, HEAD_DIM))
            v_s.append(v_rows.reshape(db, ts, n_b, HEAD_DIM))

            xp, xs, hp_, hs_, qp, qs = _mm(
                [oa.reshape(mp, -1), ob.reshape(mp, -1)],
                [oa_s.reshape(db, -1).astype(BF16), ob_s.reshape(db, -1).astype(BF16)],
                w_out_even, e, res=xp, res_s=xs, gain=norm_ffn, gain_l=l)
        else:
            r = l // 2
            zp, zs = _mm([hp_], [hs_], w_in_odd, r, ssq=qp, ssq_s=qs)
            o, st = _ret_prompt(zp.reshape(bsz, t, -1), ret_tab_p, gn_ret, r, H_C)
            r_p.append(st)
            o_s, st = _ret_sample(zs.reshape(db, 4 * H_C, dk_c), state_ret, ret_tab_s, gn_ret3, r, H_C)
            r_s.append(st)
            xp, xs, hp_, hs_, qp, qs = _mm([o.reshape(mp, -1)], [o_s.reshape(db, -1).astype(BF16)], w_out_odd, r,
                                           res=xp, res_s=xs, gain=norm_ffn, gain_l=l)
        act, buf, act_s, buf_s = _ffn_in(hp_, hs_, qp, qs, conv_buf, w_ffn_in, conv_w, conv_b, l, bsz, t)
        c_p.append(buf)
        c_s.append(jnp.swapaxes(buf_s, 0, 1))
        if l + 1 < depth:
            xp, xs, hp_, hs_, qp, qs = _mm([act], [act_s], w_ffn_out, l, res=xp, res_s=xs,
                                           gain=norm_mix, gain_l=l + 1)
        else:
            xp, xs = _mm([act], [act_s], w_ffn_out, l, res=xp, res_s=xs)
    y_p = _rmsnorm(xp, norm_final.reshape(1, d), 0, F32).reshape(bsz, t, d)
    y_s = _rmsnorm(xs, norm_final.reshape(1, d), 0, F32).reshape(db, ts, d)
    st = jnp.stack
    return (y_p, y_s, st(k_p), st(v_p), st(k_s), st(v_s), st(h_p), st(h_s), st(r_p), st(r_s), st(c_p), st(c_s))
```
